```python
import jax, jax.numpy as jnp
from jax import lax
import numpy as np

D_MODEL = 1024
BATCH = 8
SEQ = 2048
DEPTH = 1
DEC_BATCH = 128
DEC_SEQ = 4
PAST_LEN = 16384
PAGE_SIZE = 128

GLA_HEADS = 4
GLA_DK = 64
GLA_DV = 128
GLA_RANK = 16
GLA_TEMP = 16.0
RET_HEADS = 4
RET_DK = 64
RET_DV = 128
ROPE_BASE = 10000.0
CHUNK = 32
N_MEM = 256
XA_HEADS = 4
XA_DH = D_MODEL // XA_HEADS
PEER_HEADS = 8
PEER_NKEYS = 128
PEER_N = PEER_NKEYS * PEER_NKEYS
PEER_DQ = 256
PEER_TOPK = 16
PEER_BLOCK = 256
EPS = 1e-6

GLA_QK = GLA_HEADS * GLA_DK
GLA_V = GLA_HEADS * GLA_DV
RET_QK = RET_HEADS * RET_DK
RET_V = RET_HEADS * RET_DV
IN_SIZES = (GLA_QK, GLA_QK, GLA_V, GLA_RANK, GLA_V, RET_QK, RET_QK, RET_V, RET_V, D_MODEL, D_MODEL)
IN_WIDTH = 3 * GLA_QK + 2 * GLA_V + GLA_RANK - GLA_QK + 2 * RET_QK + 2 * RET_V + 2 * D_MODEL

kernel_name = 'gla_retnet_peer_hybrid_step'


def rms_norm(x, g):
    xf = x.astype(jnp.float32)
    y = xf * lax.rsqrt(jnp.mean(xf * xf, axis=-1, keepdims=True) + EPS)
    return (y * g.astype(jnp.float32)).astype(x.dtype)


def rotary(x, pos):
    half = x.shape[-1] // 2
    inv = ROPE_BASE ** (-jnp.arange(half, dtype=jnp.float32) / half)
    ang = pos.astype(jnp.float32)[:, None] * inv[None, :]
    cos = jnp.cos(ang)[None, :, None, :]
    sin = jnp.sin(ang)[None, :, None, :]
    xf = x.astype(jnp.float32)
    x1, x2 = xf[..., :half], xf[..., half:]
    return jnp.concatenate([x1 * cos - x2 * sin, x1 * sin + x2 * cos], axis=-1).astype(x.dtype)


def gated_linear_chunked(q, k, v, log_a, s0):
    B, S, H, dk = q.shape
    dv = v.shape[-1]
    C = CHUNK if S % CHUNK == 0 else S
    n = S // C

    def to_chunks(t):
        return t.astype(jnp.float32).reshape(B, n, C, H, t.shape[-1]).transpose(1, 0, 2, 3, 4)

    mask = jnp.tril(jnp.ones((C, C), dtype=bool))

    def step(state, inp):
        qc, kc, vc, lc = inp
        b = jnp.cumsum(lc, axis=1)
        b_last = b[:, -1]
        qe = qc * jnp.exp(b)
        ke = kc * jnp.exp(-b)
        inter = jnp.einsum('bchk,bhkv->bchv', qe, state)
        att = jnp.where(mask, jnp.einsum('bchk,bshk->bhcs', qe, ke), 0.0)
        intra = jnp.einsum('bhcs,bshv->bchv', att, vc)
        kd = kc * jnp.exp(b_last[:, None] - b)
        new_state = jnp.exp(b_last)[..., None] * state + jnp.einsum('bshk,bshv->bhkv', kd, vc)
        return new_state, inter + intra

    s_final, o = lax.scan(step, s0.astype(jnp.float32),
                          (to_chunks(q), to_chunks(k), to_chunks(v), to_chunks(log_a)))
    o = o.transpose(1, 0, 2, 3, 4).reshape(B, S, H, dv)
    return o.astype(v.dtype), s_final.astype(s0.dtype)


def token_mixers(xn, pos, s_gla0, s_ret0, w_in, w_a2, b_a, gla_head_norm, ret_head_norm, w_pa, w_pb, w_o):
    B, S, _ = xn.shape
    proj = xn @ w_in
    splits = np.cumsum(np.array(IN_SIZES))[:-1].tolist()
    gq, gk, gv, glr, gr, rq, rk, rv, rg, za, zb = jnp.split(proj, splits, axis=-1)

    def heads(t, h):
        return t.reshape(B, S, h, t.shape[-1] // h)

    log_a = jax.nn.log_sigmoid((glr @ w_a2 + b_a).astype(jnp.float32)) / GLA_TEMP
    o_a, s_gla = gated_linear_chunked(heads(gq, GLA_HEADS) * GLA_DK ** -0.5, heads(gk, GLA_HEADS),
                                      heads(gv, GLA_HEADS), heads(log_a, GLA_HEADS), s_gla0)
    o_a = (rms_norm(o_a, gla_head_norm) * jax.nn.silu(heads(gr, GLA_HEADS))).reshape(B, S, GLA_V)

    log_gamma = jnp.log1p(-(2.0 ** (-5.0 - jnp.arange(RET_HEADS, dtype=jnp.float32))))
    log_g = jnp.broadcast_to(log_gamma[None, None, :, None], (B, S, RET_HEADS, RET_DK))
    q_r = rotary(heads(rq, RET_HEADS), pos)
    k_r = rotary(heads(rk, RET_HEADS), pos) * RET_DK ** -0.5
    o_b, s_ret = gated_linear_chunked(q_r, k_r, heads(rv, RET_HEADS), log_g, s_ret0)
    o_b = (rms_norm(o_b, ret_head_norm) * jax.nn.silu(heads(rg, RET_HEADS))).reshape(B, S, RET_V)

    merged = jax.nn.sigmoid(za) * (o_a @ w_pa) + jax.nn.sigmoid(zb) * (o_b @ w_pb)
    return merged @ w_o, s_gla, s_ret


def memory_kv(mem, norm_mem, w_xk, w_xv):
    B, M, _ = mem.shape
    mn = rms_norm(mem, norm_mem)
    return (mn @ w_xk).reshape(B, M, XA_HEADS, XA_DH), (mn @ w_xv).reshape(B, M, XA_HEADS, XA_DH)


def cross_attn(hn, mk, mv, w_xq, w_xo):
    B, S, D = hn.shape
    q = (hn @ w_xq).reshape(B, S, XA_HEADS, XA_DH)
    s = jnp.einsum('bshd,bmhd->bhsm', q.astype(jnp.float32), mk.astype(jnp.float32)) * XA_DH ** -0.5
    p = jax.nn.softmax(s, axis=-1)
    o = jnp.einsum('bhsm,bmhd->bshd', p, mv.astype(jnp.float32)).astype(hn.dtype).reshape(B, S, D)
    return o @ w_xo


def peer(xn, peer_wq, peer_subkeys, peer_u, peer_v):
    B, S, D = xn.shape
    T = B * S
    xf = xn.reshape(T, D)
    q = (xf @ peer_wq).astype(jnp.float32).reshape(T, PEER_HEADS, PEER_DQ)
    half = PEER_DQ // 2
    sk = peer_subkeys.astype(jnp.float32)
    s1 = jnp.einsum('thd,hkd->thk', q[..., :half], sk[:, 0])
    s2 = jnp.einsum('thd,hkd->thk', q[..., half:], sk[:, 1])
    v1, i1 = lax.top_k(s1, PEER_TOPK)
    v2, i2 = lax.top_k(s2, PEER_TOPK)
    n_cand = PEER_TOPK * PEER_TOPK
    cand_s = (v1[..., :, None] + v2[..., None, :]).reshape(T, PEER_HEADS, n_cand)
    cand_i = (i1[..., :, None] * PEER_NKEYS + i2[..., None, :]).reshape(T, PEER_HEADS, n_cand)
    top_s, top_p = lax.top_k(cand_s, PEER_TOPK)
    idx = jnp.take_along_axis(cand_i, top_p, axis=-1).reshape(T, PEER_HEADS * PEER_TOPK)
    gate = jax.nn.softmax(top_s, axis=-1).reshape(T, PEER_HEADS * PEER_TOPK)
    blk = PEER_BLOCK if T % PEER_BLOCK == 0 else T
    nb = T // blk

    def expert_block(args):
        xb, ib, gb = args
        ue = jnp.take(peer_u, ib, axis=0)
        a = jnp.einsum('td,tkd->tk', xb, ue).astype(jnp.float32)
        w = (jax.nn.gelu(a, approximate=False) * gb).astype(xb.dtype)
        ve = jnp.take(peer_v, ib, axis=0)
        return jnp.einsum('tk,tkd->td', w, ve)

    out = lax.map(expert_block, (xf.reshape(nb, blk, D), idx.reshape(nb, blk, -1), gate.reshape(nb, blk, -1)))
    return out.reshape(B, S, D)


def layer(x, pos, mk, mv, s_gla0, s_ret0, norm_mix, w_in, w_a2, b_a, gla_head_norm, ret_head_norm,
          w_pa, w_pb, w_o, norm_xattn, w_xq, w_xo, norm_ffn, peer_wq, peer_subkeys, peer_u, peer_v):
    mix, s_gla, s_ret = token_mixers(rms_norm(x, norm_mix), pos, s_gla0, s_ret0, w_in, w_a2, b_a,
                                     gla_head_norm, ret_head_norm, w_pa, w_pb, w_o)
    h = x + mix
    h = h + cross_attn(rms_norm(h, norm_xattn), mk, mv, w_xq, w_xo)
    h = h + peer(rms_norm(h, norm_ffn), peer_wq, peer_subkeys, peer_u, peer_v)
    return h, s_gla, s_ret


def setup_inputs(seed: int = 0) -> dict:
    key = jax.random.key(seed)
    ks = iter(jax.random.split(key, 40))
    f32 = jnp.float32

    def nrm(shape, scale):
        return scale * jax.random.normal(next(ks), shape, f32)

    def gain(shape):
        return 1.0 + 0.05 * jax.random.normal(next(ks), shape, f32)

    L, D = DEPTH, D_MODEL
    return {
        'x_prompt': nrm((BATCH, SEQ, D), 1.0),
        'x_sample': nrm((DEC_BATCH, DEC_SEQ, D), 1.0),
        'mem_prompt': nrm((BATCH, N_MEM, D), 1.0),
        'state_gla': nrm((L, DEC_BATCH, GLA_HEADS, GLA_DK, GLA_DV), 0.5),
        'state_ret': nrm((L, DEC_BATCH, RET_HEADS, RET_DK, RET_DV), 0.5),
        'cache_mem_k': nrm((L, DEC_BATCH, N_MEM, XA_HEADS, XA_DH), 1.0),
        'cache_mem_v': nrm((L, DEC_BATCH, N_MEM, XA_HEADS, XA_DH), 1.0),
        'norm_mix': gain((L, D)),
        'w_in': nrm((L, D, IN_WIDTH), D ** -0.5),
        'w_a2': nrm((L, GLA_RANK, GLA_QK), GLA_RANK ** -0.5),
        'b_a': nrm((L, GLA_QK), 0.1),
        'gla_head_norm': gain((L, GLA_HEADS, GLA_DV)),
        'ret_head_norm': gain((L, RET_HEADS, RET_DV)),
        'w_pa': nrm((L, GLA_V, D), GLA_V ** -0.5),
        'w_pb': nrm((L, RET_V, D), RET_V ** -0.5),
        'w_o': nrm((L, D, D), D ** -0.5),
        'norm_xattn': gain((L, D)),
        'norm_mem': gain((L, D)),
        'w_xq': nrm((L, D, D), D ** -0.5),
        'w_xk': nrm((L, D, D), D ** -0.5),
        'w_xv': nrm((L, D, D), D ** -0.5),
        'w_xo': nrm((L, D, D), D ** -0.5),
        'norm_ffn': gain((L, D)),
        'peer_wq': nrm((L, D, PEER_HEADS * PEER_DQ), D ** -0.5),
        'peer_subkeys': nrm((L, PEER_HEADS, 2, PEER_NKEYS, PEER_DQ // 2), (PEER_DQ // 2) ** -0.5),
        'peer_u': nrm((L, PEER_N, D), D ** -0.5),
        'peer_v': nrm((L, PEER_N, D), 0.2),
        'norm_final': gain((D,)),
    }


def reference(x_prompt, x_sample, mem_prompt, state_gla, state_ret, cache_mem_k, cache_mem_v,
              norm_mix, w_in, w_a2, b_a, gla_head_norm, ret_head_norm, w_pa, w_pb, w_o,
              norm_xattn, norm_mem, w_xq, w_xk, w_xv, w_xo,
              norm_ffn, peer_wq, peer_subkeys, peer_u, peer_v, norm_final):
    Bp, Sp, _ = x_prompt.shape
    pos_p = jnp.arange(Sp, dtype=jnp.int32)
    pos_s = PAST_LEN + jnp.arange(x_sample.shape[1], dtype=jnp.int32)
    hp, hs = x_prompt, x_sample
    gla_p, ret_p, mk_p, mv_p, gla_s, ret_s = [], [], [], [], [], []
    for l in range(DEPTH):
        lw = (norm_mix[l], w_in[l], w_a2[l], b_a[l], gla_head_norm[l], ret_head_norm[l],
              w_pa[l], w_pb[l], w_o[l], norm_xattn[l], w_xq[l], w_xo[l],
              norm_ffn[l], peer_wq[l], peer_subkeys[l], peer_u[l], peer_v[l])
        mk, mv = memory_kv(mem_prompt, norm_mem[l], w_xk[l], w_xv[l])
        z_gla = jnp.zeros((Bp, GLA_HEADS, GLA_DK, GLA_DV), x_prompt.dtype)
        z_ret = jnp.zeros((Bp, RET_HEADS, RET_DK, RET_DV), x_prompt.dtype)
        hp, sg, sr = layer(hp, pos_p, mk, mv, z_gla, z_ret, *lw)
        gla_p.append(sg)
        ret_p.append(sr)
        mk_p.append(mk)
        mv_p.append(mv)
        hs, sg, sr = layer(hs, pos_s, cache_mem_k[l], cache_mem_v[l], state_gla[l], state_ret[l], *lw)
        gla_s.append(sg)
        ret_s.append(sr)
    y_prompt = rms_norm(hp, norm_final)
    y_sample = rms_norm(hs, norm_final)
    return (y_prompt, y_sample, jnp.stack(gla_p), jnp.stack(ret_p), jnp.stack(mk_p), jnp.stack(mv_p),
            jnp.stack(gla_s), jnp.stack(ret_s))
```

```python
import functools
import math

import jax
import jax.numpy as jnp
from jax import lax
from jax.experimental import pallas as pl
from jax.experimental.pallas import tpu as pltpu

f32 = jnp.float32
bf16 = jnp.bfloat16

D_MODEL = 1024
PAST_LEN = 16384
GLA_HEADS, GLA_DK, GLA_DV, GLA_RANK, GLA_TEMP = 4, 64, 128, 16, 16.0
RET_HEADS, RET_DK, RET_DV = 4, 64, 128
ROPE_BASE = 10000.0
N_MEM = 256
XA_HEADS = 4
XA_DH = D_MODEL // XA_HEADS
PEER_HEADS, PEER_NKEYS, PEER_DQ, PEER_TOPK = 8, 128, 256, 16
PEER_N = PEER_NKEYS * PEER_NKEYS
EPS = 1e-6

GLA_QK = GLA_HEADS * GLA_DK
GLA_V = GLA_HEADS * GLA_DV
RET_QK = RET_HEADS * RET_DK
RET_V = RET_HEADS * RET_DV
IN_SIZES = (GLA_QK, GLA_QK, GLA_V, GLA_RANK, GLA_V, RET_QK, RET_QK, RET_V, RET_V, D_MODEL, D_MODEL)

LANES = 128
SUBLANES = 8
RANK_PAD = LANES
GLA_COLS = 2 * GLA_QK + GLA_V + RANK_PAD + GLA_V
RET_COLS = 2 * RET_QK + 2 * RET_V + 2 * RET_QK
Z_COLS = 2 * D_MODEL
SCAN_CHUNK = 64
SAMPLE_PAD = SUBLANES
VMEM_LIMIT = 52 * 1024 * 1024


def _cparams(sem):
    return pltpu.CompilerParams(dimension_semantics=sem, vmem_limit_bytes=VMEM_LIMIT)


def _rms(x, g):
    return x * lax.rsqrt(jnp.mean(x * x, axis=-1, keepdims=True) + EPS) * g


def _mm(a, b):
    return jnp.dot(a.astype(bf16), b.astype(bf16), preferred_element_type=f32)


def _mm_nt(a, b):
    return lax.dot_general(a.astype(bf16), b.astype(bf16), (((1,), (1,)), ((), ())), preferred_element_type=f32)


def _mm_tn(a, b):
    return lax.dot_general(a.astype(bf16), b.astype(bf16), (((0,), (0,)), ((), ())), preferred_element_type=f32)


def _in_proj_body(x_ref, g_ref, wg_ref, wr_ref, wz_ref, og_ref, or_ref, oz_ref):
    xn = _rms(x_ref[...], g_ref[...]).astype(bf16)
    og_ref[...] = jnp.dot(xn, wg_ref[...], preferred_element_type=f32)
    or_ref[...] = jnp.dot(xn, wr_ref[...], preferred_element_type=f32)
    oz_ref[...] = jnp.dot(xn, wz_ref[...], preferred_element_type=f32)


def _in_proj(x, g, wg, wr, wz, tm=256):
    T = x.shape[0]
    row = lambda i: (i, 0)
    fixed = lambda i: (0, 0)
    return pl.pallas_call(
        _in_proj_body,
        grid=(T // tm,),
        in_specs=[pl.BlockSpec((tm, D_MODEL), row), pl.BlockSpec((1, D_MODEL), fixed),
                  pl.BlockSpec((D_MODEL, GLA_COLS), fixed), pl.BlockSpec((D_MODEL, RET_COLS), fixed),
                  pl.BlockSpec((D_MODEL, Z_COLS), fixed)],
        out_specs=[pl.BlockSpec((tm, GLA_COLS), row), pl.BlockSpec((tm, RET_COLS), row),
                   pl.BlockSpec((tm, Z_COLS), row)],
        out_shape=[jax.ShapeDtypeStruct((T, GLA_COLS), f32), jax.ShapeDtypeStruct((T, RET_COLS), f32),
                   jax.ShapeDtypeStruct((T, Z_COLS), f32)],
        compiler_params=_cparams(("parallel",)),
        name="in_proj",
    )(x, g, wg, wr, wz)


def _chunk_heads(q, k, v, b, gate, hnorm_ref, state_ref, o_ref, C):
    mid = C // 2 - 1
    b_mid = b[mid:mid + 1, :]
    b_last = b[C - 1:C, :]
    q_in = q * jnp.exp(b)
    q_e = q * jnp.exp(b - b_mid)
    k_e = k * jnp.exp(b_mid - b)
    k_d = k * jnp.exp(b_last - b)
    decay_col = jnp.exp(jnp.broadcast_to(b_last, (SUBLANES, b.shape[1])).T[:, 0:1])
    rows = lax.broadcasted_iota(jnp.int32, (C, C), 0)
    cols = lax.broadcasted_iota(jnp.int32, (C, C), 1)
    causal = rows >= cols
    for h in range(4):
        ks = slice(h * 64, (h + 1) * 64)
        vs = slice(h * 128, (h + 1) * 128)
        state = state_ref[h]
        v_h = v[:, vs]
        att = jnp.where(causal, _mm_nt(q_e[:, ks], k_e[:, ks]), 0.0)
        o_h = _mm(q_in[:, ks], state) + _mm(att, v_h)
        state_ref[h] = decay_col[ks, :] * state + _mm_tn(k_d[:, ks], v_h)
        g_h = gate[:, vs]
        o_ref[:, vs] = _rms(o_h, hnorm_ref[h:h + 1, :]) * (g_h * jax.nn.sigmoid(g_h))


def _scan_body(gla_ref, ret_ref, wa2_ref, ba_ref, gn_ref, rn_ref, cos_ref, sin_ref, lg_ref, sg0_ref, sr0_ref,
               oa_ref, ob_ref, sg_ref, sr_ref, sg_scr, sr_scr, *, C, valid):
    c = pl.program_id(1)

    @pl.when(c == 0)
    def _():
        sg_scr[...] = sg0_ref[0]
        sr_scr[...] = sr0_ref[0]

    row = lax.broadcasted_iota(jnp.int32, (C, 1), 0)
    tri = (lax.broadcasted_iota(jnp.int32, (C, C), 0) >= lax.broadcasted_iota(jnp.int32, (C, C), 1)).astype(f32)

    g = gla_ref[...]
    q = g[:, 0:GLA_QK] * GLA_DK ** -0.5
    k = g[:, GLA_QK:2 * GLA_QK]
    v = g[:, 2 * GLA_QK:2 * GLA_QK + GLA_V]
    lowrank = g[:, 2 * GLA_QK + GLA_V:2 * GLA_QK + GLA_V + RANK_PAD]
    gate = g[:, 2 * GLA_QK + GLA_V + RANK_PAD:]
    log_a = jax.nn.log_sigmoid(_mm(lowrank, wa2_ref[...]) + ba_ref[...]) / GLA_TEMP
    if valid < C:
        log_a = jnp.where(row < valid, log_a, 0.0)
    b = jnp.dot(tri, log_a, preferred_element_type=f32, precision=lax.Precision.HIGHEST)
    _chunk_heads(q, k, v, b, gate, gn_ref, sg_scr, oa_ref, C)

    r = ret_ref[...]
    cos = cos_ref[...]
    sin = sin_ref[...]
    q = r[:, 0:RET_QK] * cos + r[:, 2 * RET_QK + 2 * RET_V:3 * RET_QK + 2 * RET_V] * sin
    k = (r[:, RET_QK:2 * RET_QK] * cos + r[:, 3 * RET_QK + 2 * RET_V:] * sin) * RET_DK ** -0.5
    v = r[:, 2 * RET_QK:2 * RET_QK + RET_V]
    gate = r[:, 2 * RET_QK + RET_V:2 * RET_QK + 2 * RET_V]
    steps = jnp.minimum(row + 1, valid).astype(f32)
    b = steps * lg_ref[...]
    _chunk_heads(q, k, v, b, gate, rn_ref, sr_scr, ob_ref, C)

    @pl.when(c == pl.num_programs(1) - 1)
    def _():
        sg_ref[0] = sg_scr[...]
        sr_ref[0] = sr_scr[...]


def _scan(gla, ret, wa2, ba, gn, rn, cos, sin, lg, sg0, sr0, B, S, C, valid):
    n = S // C
    T = B * S
    tok = lambda b, c: (b * n + c, 0)
    fixed = lambda b, c: (0, 0)
    pos = lambda b, c: (c, 0)
    st = lambda b, c: (b, 0, 0, 0)
    st_block = (1, GLA_HEADS, GLA_DK, GLA_DV)
    return pl.pallas_call(
        functools.partial(_scan_body, C=C, valid=valid),
        grid=(B, n),
        in_specs=[pl.BlockSpec((C, GLA_COLS), tok), pl.BlockSpec((C, RET_COLS), tok),
                  pl.BlockSpec((RANK_PAD, GLA_QK), fixed), pl.BlockSpec((1, GLA_QK), fixed),
                  pl.BlockSpec((GLA_HEADS, GLA_DV), fixed), pl.BlockSpec((RET_HEADS, RET_DV), fixed),
                  pl.BlockSpec((C, RET_QK), pos), pl.BlockSpec((C, RET_QK), pos),
                  pl.BlockSpec((1, RET_QK), fixed),
                  pl.BlockSpec(st_block, st), pl.BlockSpec(st_block, st)],
        out_specs=[pl.BlockSpec((C, GLA_V), tok), pl.BlockSpec((C, RET_V), tok),
                   pl.BlockSpec(st_block, st), pl.BlockSpec(st_block, st)],
        out_shape=[jax.ShapeDtypeStruct((T, GLA_V), f32), jax.ShapeDtypeStruct((T, RET_V), f32),
                   jax.ShapeDtypeStruct((B,) + st_block[1:], f32), jax.ShapeDtypeStruct((B,) + st_block[1:], f32)],
        scratch_shapes=[pltpu.VMEM(st_block[1:], f32), pltpu.VMEM(st_block[1:], f32)],
        compiler_params=_cparams(("parallel", "arbitrary")),
        name="scan",
    )(gla, ret, wa2, ba, gn, rn, cos, sin, lg, sg0, sr0)


def _merge_body(x_ref, oa_ref, ob_ref, z_ref, wpa_ref, wpb_ref, wo_ref, gx_ref, wxq_ref, h_ref, q_ref):
    z = z_ref[...]
    merged = (jax.nn.sigmoid(z[:, :D_MODEL]) * _mm(oa_ref[...], wpa_ref[...])
              + jax.nn.sigmoid(z[:, D_MODEL:]) * _mm(ob_ref[...], wpb_ref[...]))
    h = x_ref[...] + _mm(merged, wo_ref[...])
    h_ref[...] = h
    q_ref[...] = _mm(_rms(h, gx_ref[...]), wxq_ref[...])


def _merge(x, oa, ob, z, wpa, wpb, wo, gx, wxq, tm=256):
    T = x.shape[0]
    row = lambda i: (i, 0)
    fixed = lambda i: (0, 0)
    return pl.pallas_call(
        _merge_body,
        grid=(T // tm,),
        in_specs=[pl.BlockSpec((tm, D_MODEL), row), pl.BlockSpec((tm, GLA_V), row), pl.BlockSpec((tm, RET_V), row),
                  pl.BlockSpec((tm, Z_COLS), row),
                  pl.BlockSpec((GLA_V, D_MODEL), fixed), pl.BlockSpec((RET_V, D_MODEL), fixed),
                  pl.BlockSpec((D_MODEL, D_MODEL), fixed), pl.BlockSpec((1, D_MODEL), fixed),
                  pl.BlockSpec((D_MODEL, D_MODEL), fixed)],
        out_specs=[pl.BlockSpec((tm, D_MODEL), row), pl.BlockSpec((tm, D_MODEL), row)],
        out_shape=[jax.ShapeDtypeStruct((T, D_MODEL), f32), jax.ShapeDtypeStruct((T, D_MODEL), f32)],
        compiler_params=_cparams(("parallel",)),
        name="merge",
    )(x, oa, ob, z, wpa, wpb, wo, gx, wxq)


def _mem_kv_body(m_ref, g_ref, wk_ref, wv_ref, k_ref, v_ref):
    mn = _rms(m_ref[...], g_ref[...]).astype(bf16)
    k_ref[...] = jnp.dot(mn, wk_ref[...], preferred_element_type=f32)
    v_ref[...] = jnp.dot(mn, wv_ref[...], preferred_element_type=f32)


def _mem_kv(mem, g, wk, wv, tm=256):
    T = mem.shape[0]
    row = lambda i: (i, 0)
    fixed = lambda i: (0, 0)
    return pl.pallas_call(
        _mem_kv_body,
        grid=(T // tm,),
        in_specs=[pl.BlockSpec((tm, D_MODEL), row), pl.BlockSpec((1, D_MODEL), fixed),
                  pl.BlockSpec((D_MODEL, D_MODEL), fixed), pl.BlockSpec((D_MODEL, D_MODEL), fixed)],
        out_specs=[pl.BlockSpec((tm, D_MODEL), row), pl.BlockSpec((tm, D_MODEL), row)],
        out_shape=[jax.ShapeDtypeStruct((T, D_MODEL), f32), jax.ShapeDtypeStruct((T, D_MODEL), f32)],
        compiler_params=_cparams(("parallel",)),
        name="mem_kv",
    )(mem, g, wk, wv)


def _xattn_body(q_ref, k_ref, v_ref, o_ref):
    q = q_ref[...]
    for h in range(XA_HEADS):
        hs = slice(h * XA_DH, (h + 1) * XA_DH)
        s = _mm_nt(q[:, hs], k_ref[0, :, hs]) * XA_DH ** -0.5
        p = jnp.exp(s - jnp.max(s, axis=-1, keepdims=True))
        p = p / jnp.sum(p, axis=-1, keepdims=True)
        o_ref[:, hs] = _mm(p, v_ref[0, :, hs])


def _xattn(q, mk, mv, B, S, tq):
    n = S // tq
    tok = lambda b, j: (b * n + j, 0)
    mem = lambda b, j: (b, 0, 0)
    return pl.pallas_call(
        _xattn_body,
        grid=(B, n),
        in_specs=[pl.BlockSpec((tq, D_MODEL), tok), pl.BlockSpec((1, N_MEM, D_MODEL), mem),
                  pl.BlockSpec((1, N_MEM, D_MODEL), mem)],
        out_specs=pl.BlockSpec((tq, D_MODEL), tok),
        out_shape=jax.ShapeDtypeStruct((B * S, D_MODEL), f32),
        compiler_params=_cparams(("parallel", "parallel")),
        name="xattn",
    )(q, mk, mv)


def _top16(cur):
    vals = []
    for _ in range(PEER_TOPK):
        m = jnp.max(cur, axis=0, keepdims=True)
        vals.append(m)
        cur = jnp.where(cur >= m, -jnp.inf, cur)
    return jnp.concatenate(vals, axis=0)


def _peer_front_body(h_ref, o_ref, wxo_ref, gf_ref, wq_ref, sk_ref,
                     h2_ref, hn_ref, s1_ref, e1_ref, s2_ref, e2_ref, tau_ref, hn_scr, *, tm):
    hd = pl.program_id(1)

    @pl.when(hd == 0)
    def _():
        h2 = h_ref[...] + _mm(o_ref[...], wxo_ref[...])
        h2_ref[...] = h2
        hn = _rms(h2, gf_ref[...]).astype(bf16)
        hn_scr[...] = hn
        hn_ref[...] = hn

    pq = jnp.dot(hn_scr[...], wq_ref[0], preferred_element_type=f32)
    half = PEER_DQ // 2
    s1 = _mm_nt(sk_ref[0, 0], pq[:, :half])
    s2 = _mm_nt(sk_ref[0, 1], pq[:, half:])
    for t in range(tm // LANES):
        ls = slice(t * LANES, (t + 1) * LANES)
        a1 = s1[:, ls]
        a2 = s2[:, ls]
        v1 = _top16(a1)
        v2 = _top16(a2)
        cand = [v1[0:1] + v2]
        cand += [v1[r:r + 1] + v2[0:8] for r in range(1, 8)]
        cand += [v1[8:16] + v2[0:1]]
        top = _top16(jnp.concatenate(cand, axis=0))
        z = jnp.sum(jnp.exp(top - top[0:1]), axis=0, keepdims=True)
        s1_ref[0, :, ls] = a1
        s2_ref[0, :, ls] = a2
        e1_ref[0, :, ls] = jnp.exp(a1 - v1[0:1]) / z
        e2_ref[0, :, ls] = jnp.exp(a2 - v2[0:1])
        tau_ref[0, :, ls] = top[PEER_TOPK - 1:PEER_TOPK]


def _peer_front(h, o, wxo, gf, wq, sk, tm=256):
    T = h.shape[0]
    row = lambda i, hd: (i, 0)
    fixed = lambda i, hd: (0, 0)
    head_t = lambda i, hd: (hd, 0, i)
    sc_shape = jax.ShapeDtypeStruct((PEER_HEADS, PEER_NKEYS, T), f32)
    sc_spec = pl.BlockSpec((1, PEER_NKEYS, tm), head_t)
    return pl.pallas_call(
        functools.partial(_peer_front_body, tm=tm),
        grid=(T // tm, PEER_HEADS),
        in_specs=[pl.BlockSpec((tm, D_MODEL), row), pl.BlockSpec((tm, D_MODEL), row),
                  pl.BlockSpec((D_MODEL, D_MODEL), fixed), pl.BlockSpec((1, D_MODEL), fixed),
                  pl.BlockSpec((1, D_MODEL, PEER_DQ), lambda i, hd: (hd, 0, 0)),
                  pl.BlockSpec((1, 2, PEER_NKEYS, PEER_DQ // 2), lambda i, hd: (hd, 0, 0, 0))],
        out_specs=[pl.BlockSpec((tm, D_MODEL), row), pl.BlockSpec((tm, D_MODEL), row),
                   sc_spec, sc_spec, sc_spec, sc_spec, pl.BlockSpec((1, 1, tm), head_t)],
        out_shape=[jax.ShapeDtypeStruct((T, D_MODEL), f32), jax.ShapeDtypeStruct((T, D_MODEL), bf16),
                   sc_shape, sc_shape, sc_shape, sc_shape, jax.ShapeDtypeStruct((PEER_HEADS, 1, T), f32)],
        scratch_shapes=[pltpu.VMEM((tm, D_MODEL), bf16)],
        compiler_params=_cparams(("parallel", "arbitrary")),
        name="peer_front",
    )(h, o, wxo, gf, wq, sk)


PEER_ROWS_PER_BLOCK = SUBLANES
PEER_BLOCK = PEER_ROWS_PER_BLOCK * PEER_NKEYS


def _peer_dense_body(hn_ref, u_ref, vt_ref, s1_ref, e1_ref, s2_ref, e2_ref, tau_ref, h2_ref, gfin_ref,
                     y_ref, at_scr, w_scr, acc_scr, *, tm):
    j = pl.program_id(1)

    @pl.when(j == 0)
    def _():
        acc_scr[...] = jnp.zeros_like(acc_scr)

    at_scr[...] = lax.dot_general(u_ref[...], hn_ref[...], (((1,), (1,)), ((), ())), preferred_element_type=f32)
    for t in range(tm // LANES):
        ls = slice(t * LANES, (t + 1) * LANES)
        for r in range(PEER_ROWS_PER_BLOCK):
            gate = jnp.zeros((PEER_NKEYS, LANES), f32)
            for hd in range(PEER_HEADS):
                picked = (s1_ref[hd, r:r + 1, ls] + s2_ref[hd, :, ls]) >= tau_ref[hd, :, ls]
                gate = gate + jnp.where(picked, e1_ref[hd, r:r + 1, ls] * e2_ref[hd, :, ls], 0.0)
            a = at_scr[r * PEER_NKEYS:(r + 1) * PEER_NKEYS, ls]
            act = 0.5 * a * (1.0 + lax.erf(a * math.sqrt(0.5)))
            w_scr[r * PEER_NKEYS:(r + 1) * PEER_NKEYS, ls] = (act * gate).astype(bf16)
    acc_scr[...] += jnp.dot(vt_ref[...], w_scr[...], preferred_element_type=f32)

    @pl.when(j == pl.num_programs(1) - 1)
    def _():
        y_ref[...] = _rms(h2_ref[...] + acc_scr[...].T, gfin_ref[...])


def _peer_dense(hn, u, vt, s1, e1, s2, e2, tau, h2, gfin, tm=512):
    T = hn.shape[0]
    nb = PEER_BLOCK
    row = lambda i, j: (i, 0)
    sc1 = pl.BlockSpec((PEER_HEADS, PEER_ROWS_PER_BLOCK, tm), lambda i, j: (0, j, i))
    sc2 = pl.BlockSpec((PEER_HEADS, PEER_NKEYS, tm), lambda i, j: (0, 0, i))
    return pl.pallas_call(
        functools.partial(_peer_dense_body, tm=tm),
        grid=(T // tm, PEER_N // nb),
        in_specs=[pl.BlockSpec((tm, D_MODEL), row),
                  pl.BlockSpec((nb, D_MODEL), lambda i, j: (j, 0)),
                  pl.BlockSpec((D_MODEL, nb), lambda i, j: (0, j)),
                  sc1, sc1, sc2, sc2,
                  pl.BlockSpec((PEER_HEADS, 1, tm), lambda i, j: (0, 0, i)),
                  pl.BlockSpec((tm, D_MODEL), row),
                  pl.BlockSpec((1, D_MODEL), lambda i, j: (0, 0))],
        out_specs=pl.BlockSpec((tm, D_MODEL), row),
        out_shape=jax.ShapeDtypeStruct((T, D_MODEL), f32),
        scratch_shapes=[pltpu.VMEM((nb, tm), f32), pltpu.VMEM((nb, tm), bf16), pltpu.VMEM((D_MODEL, tm), f32)],
        compiler_params=_cparams(("parallel", "arbitrary")),
        name="peer_dense",
    )(hn, u, vt, s1, e1, s2, e2, tau, h2, gfin)


def _rotary_tables(pos):
    half = RET_DK // 2
    inv = ROPE_BASE ** (-jnp.arange(half, dtype=f32) / half)
    ang = pos.astype(f32)[:, None] * inv[None, :]
    cos = jnp.cos(ang)
    sin = jnp.sin(ang)
    cos = jnp.tile(jnp.concatenate([cos, cos], axis=-1), (1, RET_HEADS))
    sin = jnp.tile(jnp.concatenate([-sin, sin], axis=-1), (1, RET_HEADS))
    return cos, sin


def _layer_weights(l, norm_mix, w_in, w_a2, b_a, gla_head_norm, ret_head_norm, w_pa, w_pb, w_o, norm_xattn,
                   norm_mem, w_xq, w_xk, w_xv, w_xo, norm_ffn, peer_wq, peer_subkeys, peer_u, peer_v):
    offs = [0]
    for s in IN_SIZES:
        offs.append(offs[-1] + s)
    cols = [w_in[l][:, offs[i]:offs[i + 1]] for i in range(len(IN_SIZES))]
    gq, gk, gv, glr, gr, rq, rk, rv, rg, za, zb = cols
    glr = jnp.pad(glr, ((0, 0), (0, RANK_PAD - GLA_RANK)))
    half = RET_DK // 2
    j = jnp.arange(RET_QK)
    partner = jnp.where((j % RET_DK) < half, j + half, j - half)
    w = {}
    w["gla"] = jnp.concatenate([gq, gk, gv, glr, gr], axis=1).astype(bf16)
    w["ret"] = jnp.concatenate([rq, rk, rv, rg, rq[:, partner], rk[:, partner]], axis=1).astype(bf16)
    w["z"] = jnp.concatenate([za, zb], axis=1).astype(bf16)
    w["a2"] = jnp.pad(w_a2[l], ((0, RANK_PAD - GLA_RANK), (0, 0))).astype(bf16)
    w["ba"] = b_a[l].reshape(1, GLA_QK)
    w["gn"] = gla_head_norm[l]
    w["rn"] = ret_head_norm[l]
    log_gamma = jnp.log1p(-(2.0 ** (-5.0 - jnp.arange(RET_HEADS, dtype=f32))))
    w["lg"] = jnp.repeat(log_gamma, RET_DK).reshape(1, RET_QK)
    w["norm_mix"] = norm_mix[l].reshape(1, D_MODEL)
    w["pa"] = w_pa[l].astype(bf16)
    w["pb"] = w_pb[l].astype(bf16)
    w["o"] = w_o[l].astype(bf16)
    w["norm_xattn"] = norm_xattn[l].reshape(1, D_MODEL)
    w["norm_mem"] = norm_mem[l].reshape(1, D_MODEL)
    w["xq"] = w_xq[l].astype(bf16)
    w["xk"] = w_xk[l].astype(bf16)
    w["xv"] = w_xv[l].astype(bf16)
    w["xo"] = w_xo[l].astype(bf16)
    w["norm_ffn"] = norm_ffn[l].reshape(1, D_MODEL)
    w["wq"] = peer_wq[l].reshape(D_MODEL, PEER_HEADS, PEER_DQ).transpose(1, 0, 2).astype(bf16)
    w["sk"] = peer_subkeys[l].astype(bf16)
    w["u"] = peer_u[l].astype(bf16)
    w["vt"] = peer_v[l].T.astype(bf16)
    return w


def _layer(x, B, S, C, valid, cos, sin, mk, mv, sg0, sr0, w, gfin, dense_tm):
    gla, ret, z = _in_proj(x, w["norm_mix"], w["gla"], w["ret"], w["z"])
    oa, ob, sg, sr = _scan(gla, ret, w["a2"], w["ba"], w["gn"], w["rn"], cos, sin, w["lg"], sg0, sr0, B, S, C, valid)
    h, q = _merge(x, oa, ob, z, w["pa"], w["pb"], w["o"], w["norm_xattn"], w["xq"])
    o = _xattn(q, mk, mv, B, S, min(S, 512))
    h2, hn, s1, e1, s2, e2, tau = _peer_front(h, o, w["xo"], w["norm_ffn"], w["wq"], w["sk"])
    y = _peer_dense(hn, w["u"], w["vt"], s1, e1, s2, e2, tau, h2, gfin, tm=dense_tm)
    return y, sg, sr


def kernel(x_prompt, x_sample, mem_prompt, state_gla, state_ret, cache_mem_k, cache_mem_v, norm_mix, w_in, w_a2,
           b_a, gla_head_norm, ret_head_norm, w_pa, w_pb, w_o, norm_xattn, norm_mem, w_xq, w_xk, w_xv, w_xo,
           norm_ffn, peer_wq, peer_subkeys, peer_u, peer_v, norm_final):
    depth = w_in.shape[0]
    assert depth == 1, "the final norm is fused into the layer's last kernel"
    Bp, Sp, _ = x_prompt.shape
    Bs, Ss, _ = x_sample.shape
    l = 0
    w = _layer_weights(l, norm_mix, w_in, w_a2, b_a, gla_head_norm, ret_head_norm, w_pa, w_pb, w_o, norm_xattn,
                       norm_mem, w_xq, w_xk, w_xv, w_xo, norm_ffn, peer_wq, peer_subkeys, peer_u, peer_v)
    gfin = norm_final.reshape(1, D_MODEL)

    cos_p, sin_p = _rotary_tables(jnp.arange(Sp, dtype=jnp.int32))
    mk, mv = _mem_kv(mem_prompt.reshape(Bp * N_MEM, D_MODEL), w["norm_mem"], w["xk"], w["xv"])
    zeros = jnp.zeros((Bp, GLA_HEADS, GLA_DK, GLA_DV), f32)
    yp, sgp, srp = _layer(x_prompt.reshape(Bp * Sp, D_MODEL), Bp, Sp, SCAN_CHUNK, SCAN_CHUNK, cos_p, sin_p,
                          mk.reshape(Bp, N_MEM, D_MODEL), mv.reshape(Bp, N_MEM, D_MODEL), zeros, zeros, w, gfin, 512)

    pad = SAMPLE_PAD - Ss
    xs = jnp.pad(x_sample, ((0, 0), (0, pad), (0, 0))).reshape(Bs * SAMPLE_PAD, D_MODEL)
    cos_s, sin_s = _rotary_tables(PAST_LEN + jnp.arange(SAMPLE_PAD, dtype=jnp.int32))
    ys, sgs, srs = _layer(xs, Bs, SAMPLE_PAD, SAMPLE_PAD, Ss, cos_s, sin_s,
                          cache_mem_k[l].reshape(Bs, N_MEM, D_MODEL), cache_mem_v[l].reshape(Bs, N_MEM, D_MODEL),
                          state_gla[l], state_ret[l], w, gfin, 512)
    ys = ys.reshape(Bs, SAMPLE_PAD, D_MODEL)[:, :Ss]

    kv_shape = (1, Bp, N_MEM, XA_HEADS, XA_DH)
    return (yp.reshape(Bp, Sp, D_MODEL), ys, sgp[None], srp[None], mk.reshape(kv_shape), mv.reshape(kv_shape),
            sgs[None], srs[None])
```

```python
import functools
import math

import jax
import jax.numpy as jnp
from jax import lax
from jax.experimental import pallas as pl
from jax.experimental.pallas import tpu as pltpu

f32 = jnp.float32
bf16 = jnp.bfloat16

D_MODEL = 1024
PAST_LEN = 16384
GLA_HEADS, GLA_DK, GLA_DV, GLA_RANK, GLA_TEMP = 4, 64, 128, 16, 16.0
RET_HEADS, RET_DK, RET_DV = 4, 64, 128
ROPE_BASE = 10000.0
N_MEM = 256
XA_HEADS = 4
XA_DH = D_MODEL // XA_HEADS
PEER_HEADS, PEER_NKEYS, PEER_DQ, PEER_TOPK = 8, 128, 256, 16
PEER_N = PEER_NKEYS * PEER_NKEYS
EPS = 1e-6

GLA_QK = GLA_HEADS * GLA_DK
GLA_V = GLA_HEADS * GLA_DV
RET_QK = RET_HEADS * RET_DK
RET_V = RET_HEADS * RET_DV
IN_SIZES = (GLA_QK, GLA_QK, GLA_V, GLA_RANK, GLA_V, RET_QK, RET_QK, RET_V, RET_V, D_MODEL, D_MODEL)

LANES = 128
SUBLANES = 8
RANK_PAD = LANES
GLA_COLS = 2 * GLA_QK + GLA_V + RANK_PAD + GLA_V
RET_COLS = 2 * RET_QK + 2 * RET_V + 2 * RET_QK
Z_COLS = 2 * D_MODEL
SCAN_CHUNK = 64
SCAN_SEQS = 4
SAMPLE_PAD = SUBLANES
XATTN_ROWS = 512
XATTN_SEQS = 8
VMEM_LIMIT = 52 * 1024 * 1024


def _cparams(sem):
    return pltpu.CompilerParams(dimension_semantics=sem, vmem_limit_bytes=VMEM_LIMIT)


def _rms(x, g):
    return x * lax.rsqrt(jnp.mean(x * x, axis=-1, keepdims=True) + EPS) * g


def _mm(a, b):
    return jnp.dot(a.astype(bf16), b.astype(bf16), preferred_element_type=f32)


def _mm_nt(a, b):
    return lax.dot_general(a.astype(bf16), b.astype(bf16), (((1,), (1,)), ((), ())), preferred_element_type=f32)


def _mm_tn(a, b):
    return lax.dot_general(a.astype(bf16), b.astype(bf16), (((0,), (0,)), ((), ())), preferred_element_type=f32)


def _in_proj_body(x_ref, g_ref, wg_ref, wr_ref, wz_ref, og_ref, or_ref, oz_ref):
    xn = _rms(x_ref[...], g_ref[...]).astype(bf16)
    og_ref[...] = jnp.dot(xn, wg_ref[...], preferred_element_type=f32)
    or_ref[...] = jnp.dot(xn, wr_ref[...], preferred_element_type=f32)
    oz_ref[...] = jnp.dot(xn, wz_ref[...], preferred_element_type=f32)


def _in_proj(x, g, wg, wr, wz, tm=256):
    T = x.shape[0]
    row = lambda i: (i, 0)
    fixed = lambda i: (0, 0)
    return pl.pallas_call(
        _in_proj_body,
        grid=(T // tm,),
        in_specs=[pl.BlockSpec((tm, D_MODEL), row), pl.BlockSpec((1, D_MODEL), fixed),
                  pl.BlockSpec((D_MODEL, GLA_COLS), fixed), pl.BlockSpec((D_MODEL, RET_COLS), fixed),
                  pl.BlockSpec((D_MODEL, Z_COLS), fixed)],
        out_specs=[pl.BlockSpec((tm, GLA_COLS), row), pl.BlockSpec((tm, RET_COLS), row),
                   pl.BlockSpec((tm, Z_COLS), row)],
        out_shape=[jax.ShapeDtypeStruct((T, GLA_COLS), f32), jax.ShapeDtypeStruct((T, RET_COLS), f32),
                   jax.ShapeDtypeStruct((T, Z_COLS), f32)],
        compiler_params=_cparams(("parallel",)),
        name="in_proj",
    )(x, g, wg, wr, wz)


def _chunk_heads(q, k, v, b, gate, hnorm_ref, state_ref, o_ref, C):
    mid = C // 2 - 1
    b_mid = b[mid:mid + 1, :]
    b_last = b[C - 1:C, :]
    q_in = q * jnp.exp(b)
    q_e = q * jnp.exp(b - b_mid)
    k_e = k * jnp.exp(b_mid - b)
    k_d = k * jnp.exp(b_last - b)
    decay_col = jnp.exp(jnp.broadcast_to(b_last, (SUBLANES, b.shape[1])).T[:, 0:1])
    rows = lax.broadcasted_iota(jnp.int32, (C, C), 0)
    cols = lax.broadcasted_iota(jnp.int32, (C, C), 1)
    causal = rows >= cols
    for h in range(4):
        ks = slice(h * 64, (h + 1) * 64)
        vs = slice(h * 128, (h + 1) * 128)
        state = state_ref[h]
        v_h = v[:, vs]
        att = jnp.where(causal, _mm_nt(q_e[:, ks], k_e[:, ks]), 0.0)
        o_h = _mm(q_in[:, ks], state) + _mm(att, v_h)
        state_ref[h] = decay_col[ks, :] * state + _mm_tn(k_d[:, ks], v_h)
        g_h = gate[:, vs]
        o_ref[:, vs] = _rms(o_h, hnorm_ref[h:h + 1, :]) * (g_h * jax.nn.sigmoid(g_h))


def _scan_body(gla_ref, ret_ref, wa2_ref, ba_ref, gn_ref, rn_ref, cos_ref, sin_ref, lg_ref, sg0_ref, sr0_ref,
               oa_ref, ob_ref, sg_ref, sr_ref, sg_scr, sr_scr, *, C, valid):
    c = pl.program_id(1)

    @pl.when(c == 0)
    def _():
        sg_scr[...] = sg0_ref[...]
        sr_scr[...] = sr0_ref[...]

    row = lax.broadcasted_iota(jnp.int32, (C, 1), 0)
    tri = (lax.broadcasted_iota(jnp.int32, (C, C), 0) >= lax.broadcasted_iota(jnp.int32, (C, C), 1)).astype(f32)
    cos = cos_ref[...]
    sin = sin_ref[...]

    for s in range(SCAN_SEQS):
        g = gla_ref[s]
        q = g[:, 0:GLA_QK] * GLA_DK ** -0.5
        k = g[:, GLA_QK:2 * GLA_QK]
        v = g[:, 2 * GLA_QK:2 * GLA_QK + GLA_V]
        lowrank = g[:, 2 * GLA_QK + GLA_V:2 * GLA_QK + GLA_V + RANK_PAD]
        gate = g[:, 2 * GLA_QK + GLA_V + RANK_PAD:]
        log_a = jax.nn.log_sigmoid(_mm(lowrank, wa2_ref[...]) + ba_ref[...]) / GLA_TEMP
        if valid < C:
            log_a = jnp.where(row < valid, log_a, 0.0)
        b = jnp.dot(tri, log_a, preferred_element_type=f32, precision=lax.Precision.HIGHEST)
        _chunk_heads(q, k, v, b, gate, gn_ref, sg_scr.at[s], oa_ref.at[s], C)

        r = ret_ref[s]
        q = r[:, 0:RET_QK] * cos + r[:, 2 * RET_QK + 2 * RET_V:3 * RET_QK + 2 * RET_V] * sin
        k = (r[:, RET_QK:2 * RET_QK] * cos + r[:, 3 * RET_QK + 2 * RET_V:] * sin) * RET_DK ** -0.5
        v = r[:, 2 * RET_QK:2 * RET_QK + RET_V]
        gate = r[:, 2 * RET_QK + RET_V:2 * RET_QK + 2 * RET_V]
        steps = jnp.minimum(row + 1, valid).astype(f32)
        b = steps * lg_ref[...]
        _chunk_heads(q, k, v, b, gate, rn_ref, sr_scr.at[s], ob_ref.at[s], C)

    @pl.when(c == pl.num_programs(1) - 1)
    def _():
        sg_ref[...] = sg_scr[...]
        sr_ref[...] = sr_scr[...]


def _scan(gla, ret, wa2, ba, gn, rn, cos, sin, lg, sg0, sr0, B, S, C, valid):
    n = S // C
    G = SCAN_SEQS
    assert B % G == 0
    gla = gla.reshape(B, S, GLA_COLS)
    ret = ret.reshape(B, S, RET_COLS)
    tok = lambda b, c: (b, c, 0)
    fixed = lambda b, c: (0, 0)
    pos = lambda b, c: (c, 0)
    st = lambda b, c: (b, 0, 0, 0)
    st_block = (G, GLA_HEADS, GLA_DK, GLA_DV)
    oa, ob, sg, sr = pl.pallas_call(
        functools.partial(_scan_body, C=C, valid=valid),
        grid=(B // G, n),
        in_specs=[pl.BlockSpec((G, C, GLA_COLS), tok), pl.BlockSpec((G, C, RET_COLS), tok),
                  pl.BlockSpec((RANK_PAD, GLA_QK), fixed), pl.BlockSpec((1, GLA_QK), fixed),
                  pl.BlockSpec((GLA_HEADS, GLA_DV), fixed), pl.BlockSpec((RET_HEADS, RET_DV), fixed),
                  pl.BlockSpec((C, RET_QK), pos), pl.BlockSpec((C, RET_QK), pos),
                  pl.BlockSpec((1, RET_QK), fixed),
                  pl.BlockSpec(st_block, st), pl.BlockSpec(st_block, st)],
        out_specs=[pl.BlockSpec((G, C, GLA_V), tok), pl.BlockSpec((G, C, RET_V), tok),
                   pl.BlockSpec(st_block, st), pl.BlockSpec(st_block, st)],
        out_shape=[jax.ShapeDtypeStruct((B, S, GLA_V), f32), jax.ShapeDtypeStruct((B, S, RET_V), f32),
                   jax.ShapeDtypeStruct((B,) + st_block[1:], f32), jax.ShapeDtypeStruct((B,) + st_block[1:], f32)],
        scratch_shapes=[pltpu.VMEM(st_block, f32), pltpu.VMEM(st_block, f32)],
        compiler_params=_cparams(("parallel", "arbitrary")),
        name="scan",
    )(gla, ret, wa2, ba, gn, rn, cos, sin, lg, sg0, sr0)
    return oa.reshape(B * S, GLA_V), ob.reshape(B * S, RET_V), sg, sr


def _merge_body(x_ref, oa_ref, ob_ref, z_ref, wpa_ref, wpb_ref, wo_ref, gx_ref, wxq_ref, h_ref, q_ref):
    z = z_ref[...]
    merged = (jax.nn.sigmoid(z[:, :D_MODEL]) * _mm(oa_ref[...], wpa_ref[...])
              + jax.nn.sigmoid(z[:, D_MODEL:]) * _mm(ob_ref[...], wpb_ref[...]))
    h = x_ref[...] + _mm(merged, wo_ref[...])
    h_ref[...] = h
    q_ref[...] = _mm(_rms(h, gx_ref[...]), wxq_ref[...])


def _merge(x, oa, ob, z, wpa, wpb, wo, gx, wxq, tm=256):
    T = x.shape[0]
    row = lambda i: (i, 0)
    fixed = lambda i: (0, 0)
    return pl.pallas_call(
        _merge_body,
        grid=(T // tm,),
        in_specs=[pl.BlockSpec((tm, D_MODEL), row), pl.BlockSpec((tm, GLA_V), row), pl.BlockSpec((tm, RET_V), row),
                  pl.BlockSpec((tm, Z_COLS), row),
                  pl.BlockSpec((GLA_V, D_MODEL), fixed), pl.BlockSpec((RET_V, D_MODEL), fixed),
                  pl.BlockSpec((D_MODEL, D_MODEL), fixed), pl.BlockSpec((1, D_MODEL), fixed),
                  pl.BlockSpec((D_MODEL, D_MODEL), fixed)],
        out_specs=[pl.BlockSpec((tm, D_MODEL), row), pl.BlockSpec((tm, D_MODEL), row)],
        out_shape=[jax.ShapeDtypeStruct((T, D_MODEL), f32), jax.ShapeDtypeStruct((T, D_MODEL), f32)],
        compiler_params=_cparams(("parallel",)),
        name="merge",
    )(x, oa, ob, z, wpa, wpb, wo, gx, wxq)


def _mem_kv_body(m_ref, g_ref, wk_ref, wv_ref, k_ref, v_ref):
    mn = _rms(m_ref[...], g_ref[...]).astype(bf16)
    k_ref[...] = jnp.dot(mn, wk_ref[...], preferred_element_type=f32)
    v_ref[...] = jnp.dot(mn, wv_ref[...], preferred_element_type=f32)


def _mem_kv(mem, g, wk, wv, tm=256):
    T = mem.shape[0]
    row = lambda i: (i, 0)
    fixed = lambda i: (0, 0)
    return pl.pallas_call(
        _mem_kv_body,
        grid=(T // tm,),
        in_specs=[pl.BlockSpec((tm, D_MODEL), row), pl.BlockSpec((1, D_MODEL), fixed),
                  pl.BlockSpec((D_MODEL, D_MODEL), fixed), pl.BlockSpec((D_MODEL, D_MODEL), fixed)],
        out_specs=[pl.BlockSpec((tm, D_MODEL), row), pl.BlockSpec((tm, D_MODEL), row)],
        out_shape=[jax.ShapeDtypeStruct((T, D_MODEL), f32), jax.ShapeDtypeStruct((T, D_MODEL), f32)],
        compiler_params=_cparams(("parallel",)),
        name="mem_kv",
    )(mem, g, wk, wv)


def _xattn_body(q_ref, k_ref, v_ref, o_ref, *, seqs, tq):
    for s in range(seqs):
        rows = slice(s * tq, (s + 1) * tq)
        q = q_ref[rows, :]
        for h in range(XA_HEADS):
            hs = slice(h * XA_DH, (h + 1) * XA_DH)
            sc = _mm_nt(q[:, hs], k_ref[s, :, hs]) * XA_DH ** -0.5
            p = jnp.exp(sc - jnp.max(sc, axis=-1, keepdims=True))
            p = p / jnp.sum(p, axis=-1, keepdims=True)
            o_ref[rows, hs] = _mm(p, v_ref[s, :, hs])


def _xattn(q, mk, mv, B, S, tq, seqs):
    n = S // tq
    assert seqs == 1 or n == 1
    tok = lambda b, j: (b * n + j, 0)
    mem_spec = pl.BlockSpec((seqs, N_MEM, D_MODEL), lambda b, j: (b, 0, 0))
    return pl.pallas_call(
        functools.partial(_xattn_body, seqs=seqs, tq=tq),
        grid=(B // seqs, n),
        in_specs=[pl.BlockSpec((seqs * tq, D_MODEL), tok), mem_spec, mem_spec],
        out_specs=pl.BlockSpec((seqs * tq, D_MODEL), tok),
        out_shape=jax.ShapeDtypeStruct((B * S, D_MODEL), f32),
        compiler_params=_cparams(("parallel", "parallel")),
        name="xattn",
    )(q, mk, mv)


def _merge_exchange_network(n):
    pairs = []
    p = 1
    while p < n:
        k = p
        while k >= 1:
            for j in range(k % p, n - k, 2 * k):
                for i in range(min(k, n - j - k)):
                    if (i + j) // (2 * p) == (i + j + k) // (2 * p):
                        pairs.append((i + j, i + j + k))
            k //= 2
        p *= 2
    return pairs


def _compare_exchange(x, i, j):
    a, b = x[i], x[j]
    if b is None:
        return
    if a is None:
        x[i], x[j] = b, None
        return
    x[i], x[j] = jnp.maximum(a, b), jnp.minimum(a, b)


def _top16(tiles):
    n = PEER_TOPK
    x = list(tiles) + [None] * (n - len(tiles))
    for i, j in _merge_exchange_network(n):
        _compare_exchange(x, i, j)
    for shift in (4, 2, 1):
        merged = []
        for i in range(n):
            a, b = x[i], x[n - 1 - i]
            b = None if b is None else pltpu.roll(b, shift, 0)
            merged.append(b if a is None else (a if b is None else jnp.maximum(a, b)))
        x = merged
        d = n // 2
        while d >= 1:
            for i in range(n):
                if (i & d) == 0:
                    _compare_exchange(x, i, i + d)
            d //= 2
    return x


def _rows_to_sublanes(v, sub):
    out = v[SUBLANES - 1]
    for r in range(SUBLANES - 2, -1, -1):
        out = jnp.where(sub == r, v[r], out)
    return out


def _peer_front_body(h_ref, o_ref, wxo_ref, gf_ref, wq_ref, sk_ref,
                     h2_ref, hn_ref, s1_ref, e1_ref, s2_ref, e2_ref, tau_ref, hn_scr, *, tm):
    hd = pl.program_id(1)

    @pl.when(hd == 0)
    def _():
        h2 = h_ref[...] + _mm(o_ref[...], wxo_ref[...])
        h2_ref[...] = h2
        hn = _rms(h2, gf_ref[...]).astype(bf16)
        hn_scr[...] = hn
        hn_ref[...] = hn

    pq = jnp.dot(hn_scr[...], wq_ref[0], preferred_element_type=f32)
    half = PEER_DQ // 2
    s1 = _mm_nt(sk_ref[0, 0], pq[:, :half])
    s2 = _mm_nt(sk_ref[0, 1], pq[:, half:])
    sub = lax.broadcasted_iota(jnp.int32, (SUBLANES, LANES), 0)
    for t in range(tm // LANES):
        ls = slice(t * LANES, (t + 1) * LANES)
        a1 = s1[:, ls]
        a2 = s2[:, ls]
        v1 = _top16([a1[SUBLANES * i:SUBLANES * (i + 1)] for i in range(PEER_NKEYS // SUBLANES)])
        v2 = _top16([a2[SUBLANES * i:SUBLANES * (i + 1)] for i in range(PEER_NKEYS // SUBLANES)])
        v1_hi = _rows_to_sublanes(v1[SUBLANES:], sub)
        v2_lo = _rows_to_sublanes(v2[:SUBLANES], sub)
        v2_hi = _rows_to_sublanes(v2[SUBLANES:], sub)
        cand = [v1[0] + v2_lo, v1[0] + v2_hi]
        cand += [v1[r] + v2_lo for r in range(1, SUBLANES)]
        cand += [v1_hi + v2[0]]
        top = _top16(cand)
        e = (jnp.exp(_rows_to_sublanes(top[:SUBLANES], sub) - top[0])
             + jnp.exp(_rows_to_sublanes(top[SUBLANES:], sub) - top[0]))
        for shift in (4, 2, 1):
            e = e + pltpu.roll(e, shift, 0)
        s1_ref[0, :, ls] = a1
        s2_ref[0, :, ls] = a2
        e1_ref[0, :, ls] = jnp.exp(a1 - v1[0][0:1]) / e[0:1]
        e2_ref[0, :, ls] = jnp.exp(a2 - v2[0][0:1])
        tau_ref[0, :, ls] = top[PEER_TOPK - 1][0:1]


def _peer_front(h, o, wxo, gf, wq, sk, tm=256):
    T = h.shape[0]
    row = lambda i, hd: (i, 0)
    fixed = lambda i, hd: (0, 0)
    head_t = lambda i, hd: (hd, 0, i)
    sc_shape = jax.ShapeDtypeStruct((PEER_HEADS, PEER_NKEYS, T), f32)
    sc_spec = pl.BlockSpec((1, PEER_NKEYS, tm), head_t)
    return pl.pallas_call(
        functools.partial(_peer_front_body, tm=tm),
        grid=(T // tm, PEER_HEADS),
        in_specs=[pl.BlockSpec((tm, D_MODEL), row), pl.BlockSpec((tm, D_MODEL), row),
                  pl.BlockSpec((D_MODEL, D_MODEL), fixed), pl.BlockSpec((1, D_MODEL), fixed),
                  pl.BlockSpec((1, D_MODEL, PEER_DQ), lambda i, hd: (hd, 0, 0)),
                  pl.BlockSpec((1, 2, PEER_NKEYS, PEER_DQ // 2), lambda i, hd: (hd, 0, 0, 0))],
        out_specs=[pl.BlockSpec((tm, D_MODEL), row), pl.BlockSpec((tm, D_MODEL), row),
                   sc_spec, sc_spec, sc_spec, sc_spec, pl.BlockSpec((1, 1, tm), head_t)],
        out_shape=[jax.ShapeDtypeStruct((T, D_MODEL), f32), jax.ShapeDtypeStruct((T, D_MODEL), bf16),
                   sc_shape, sc_shape, sc_shape, sc_shape, jax.ShapeDtypeStruct((PEER_HEADS, 1, T), f32)],
        scratch_shapes=[pltpu.VMEM((tm, D_MODEL), bf16)],
        compiler_params=_cparams(("parallel", "arbitrary")),
        name="peer_front",
    )(h, o, wxo, gf, wq, sk)


PEER_ROWS_PER_BLOCK = SUBLANES
PEER_BLOCK = PEER_ROWS_PER_BLOCK * PEER_NKEYS
GATE_ROWS = 4 * SUBLANES


def _peer_dense_body(hn_ref, u_ref, vt_ref, s1_ref, e1_ref, s2_ref, e2_ref, tau_ref, h2_ref, gfin_ref,
                     y_ref, at_scr, w_scr, acc_scr, *, tm):
    j = pl.program_id(1)

    @pl.when(j == 0)
    def _():
        acc_scr[...] = jnp.zeros_like(acc_scr)

    at_scr[...] = lax.dot_general(u_ref[...], hn_ref[...], (((1,), (1,)), ((), ())), preferred_element_type=f32)
    chunks = PEER_NKEYS // GATE_ROWS

    def gate_tile(i, carry):
        ls = pl.ds(pl.multiple_of((i // chunks) * LANES, LANES), LANES)
        c0 = pl.multiple_of((i % chunks) * GATE_ROWS, GATE_ROWS)
        gates = [None] * PEER_ROWS_PER_BLOCK
        for hd in range(PEER_HEADS):
            s2 = s2_ref[hd, pl.ds(c0, GATE_ROWS), ls]
            e2 = e2_ref[hd, pl.ds(c0, GATE_ROWS), ls]
            tau = tau_ref[hd, :, ls]
            for r in range(PEER_ROWS_PER_BLOCK):
                picked = (s1_ref[hd, r:r + 1, ls] + s2) >= tau
                term = jnp.where(picked, e1_ref[hd, r:r + 1, ls] * e2, 0.0)
                gates[r] = term if gates[r] is None else gates[r] + term
        for r in range(PEER_ROWS_PER_BLOCK):
            rs = pl.ds(pl.multiple_of(r * PEER_NKEYS + c0, GATE_ROWS), GATE_ROWS)
            a = at_scr[rs, ls]
            act = 0.5 * a * (1.0 + lax.erf(a * math.sqrt(0.5)))
            w_scr[rs, ls] = (act * gates[r]).astype(bf16)
        return carry

    lax.fori_loop(0, (tm // LANES) * chunks, gate_tile, 0)
    acc_scr[...] += jnp.dot(vt_ref[...], w_scr[...], preferred_element_type=f32)

    @pl.when(j == pl.num_programs(1) - 1)
    def _():
        y_ref[...] = _rms(h2_ref[...] + acc_scr[...].T, gfin_ref[...])


def _peer_dense(hn, u, vt, s1, e1, s2, e2, tau, h2, gfin, tm=1024):
    T = hn.shape[0]
    nb = PEER_BLOCK
    assert T % tm == 0
    row = lambda i, j: (i, 0)
    once = pl.Buffered(1)
    sc1 = pl.BlockSpec((PEER_HEADS, PEER_ROWS_PER_BLOCK, tm), lambda i, j: (0, j, i))
    sc2 = pl.BlockSpec((PEER_HEADS, PEER_NKEYS, tm), lambda i, j: (0, 0, i), pipeline_mode=once)
    return pl.pallas_call(
        functools.partial(_peer_dense_body, tm=tm),
        grid=(T // tm, PEER_N // nb),
        in_specs=[pl.BlockSpec((tm, D_MODEL), row),
                  pl.BlockSpec((nb, D_MODEL), lambda i, j: (j, 0)),
                  pl.BlockSpec((D_MODEL, nb), lambda i, j: (0, j)),
                  sc1, sc1, sc2, sc2,
                  pl.BlockSpec((PEER_HEADS, 1, tm), lambda i, j: (0, 0, i)),
                  pl.BlockSpec((tm, D_MODEL), row, pipeline_mode=once),
                  pl.BlockSpec((1, D_MODEL), lambda i, j: (0, 0))],
        out_specs=pl.BlockSpec((tm, D_MODEL), row),
        out_shape=jax.ShapeDtypeStruct((T, D_MODEL), f32),
        scratch_shapes=[pltpu.VMEM((nb, tm), f32), pltpu.VMEM((nb, tm), bf16), pltpu.VMEM((D_MODEL, tm), f32)],
        compiler_params=_cparams(("parallel", "arbitrary")),
        name="peer_dense",
    )(hn, u, vt, s1, e1, s2, e2, tau, h2, gfin)


def _rotary_tables(pos):
    half = RET_DK // 2
    inv = ROPE_BASE ** (-jnp.arange(half, dtype=f32) / half)
    ang = pos.astype(f32)[:, None] * inv[None, :]
    cos = jnp.cos(ang)
    sin = jnp.sin(ang)
    cos = jnp.tile(jnp.concatenate([cos, cos], axis=-1), (1, RET_HEADS))
    sin = jnp.tile(jnp.concatenate([-sin, sin], axis=-1), (1, RET_HEADS))
    return cos, sin


def _layer_weights(l, norm_mix, w_in, w_a2, b_a, gla_head_norm, ret_head_norm, w_pa, w_pb, w_o, norm_xattn,
                   norm_mem, w_xq, w_xk, w_xv, w_xo, norm_ffn, peer_wq, peer_subkeys, peer_u, peer_v):
    offs = [0]
    for s in IN_SIZES:
        offs.append(offs[-1] + s)
    cols = [w_in[l][:, offs[i]:offs[i + 1]] for i in range(len(IN_SIZES))]
    gq, gk, gv, glr, gr, rq, rk, rv, rg, za, zb = cols
    glr = jnp.pad(glr, ((0, 0), (0, RANK_PAD - GLA_RANK)))
    half = RET_DK // 2
    j = jnp.arange(RET_QK)
    partner = jnp.where((j % RET_DK) < half, j + half, j - half)
    w = {}
    w["gla"] = jnp.concatenate([gq, gk, gv, glr, gr], axis=1).astype(bf16)
    w["ret"] = jnp.concatenate([rq, rk, rv, rg, rq[:, partner], rk[:, partner]], axis=1).astype(bf16)
    w["z"] = jnp.concatenate([za, zb], axis=1).astype(bf16)
    w["a2"] = jnp.pad(w_a2[l], ((0, RANK_PAD - GLA_RANK), (0, 0))).astype(bf16)
    w["ba"] = b_a[l].reshape(1, GLA_QK)
    w["gn"] = gla_head_norm[l]
    w["rn"] = ret_head_norm[l]
    log_gamma = jnp.log1p(-(2.0 ** (-5.0 - jnp.arange(RET_HEADS, dtype=f32))))
    w["lg"] = jnp.repeat(log_gamma, RET_DK).reshape(1, RET_QK)
    w["norm_mix"] = norm_mix[l].reshape(1, D_MODEL)
    w["pa"] = w_pa[l].astype(bf16)
    w["pb"] = w_pb[l].astype(bf16)
    w["o"] = w_o[l].astype(bf16)
    w["norm_xattn"] = norm_xattn[l].reshape(1, D_MODEL)
    w["norm_mem"] = norm_mem[l].reshape(1, D_MODEL)
    w["xq"] = w_xq[l].astype(bf16)
    w["xk"] = w_xk[l].astype(bf16)
    w["xv"] = w_xv[l].astype(bf16)
    w["xo"] = w_xo[l].astype(bf16)
    w["norm_ffn"] = norm_ffn[l].reshape(1, D_MODEL)
    w["wq"] = peer_wq[l].reshape(D_MODEL, PEER_HEADS, PEER_DQ).transpose(1, 0, 2).astype(bf16)
    w["sk"] = peer_subkeys[l].astype(bf16)
    w["u"] = peer_u[l].astype(bf16)
    w["vt"] = peer_v[l].T.astype(bf16)
    return w


def _layer(x, B, S, C, valid, cos, sin, mk, mv, sg0, sr0, w, gfin):
    gla, ret, z = _in_proj(x, w["norm_mix"], w["gla"], w["ret"], w["z"])
    oa, ob, sg, sr = _scan(gla, ret, w["a2"], w["ba"], w["gn"], w["rn"], cos, sin, w["lg"], sg0, sr0, B, S, C, valid)
    h, q = _merge(x, oa, ob, z, w["pa"], w["pb"], w["o"], w["norm_xattn"], w["xq"])
    tq = min(S, XATTN_ROWS)
    o = _xattn(q, mk, mv, B, S, tq, XATTN_SEQS if tq == S else 1)
    h2, hn, s1, e1, s2, e2, tau = _peer_front(h, o, w["xo"], w["norm_ffn"], w["wq"], w["sk"])
    y = _peer_dense(hn, w["u"], w["vt"], s1, e1, s2, e2, tau, h2, gfin)
    return y, sg, sr


def kernel(x_prompt, x_sample, mem_prompt, state_gla, state_ret, cache_mem_k, cache_mem_v, norm_mix, w_in, w_a2,
           b_a, gla_head_norm, ret_head_norm, w_pa, w_pb, w_o, norm_xattn, norm_mem, w_xq, w_xk, w_xv, w_xo,
           norm_ffn, peer_wq, peer_subkeys, peer_u, peer_v, norm_final):
    depth = w_in.shape[0]
    assert depth == 1, "the final norm is fused into the layer's last kernel"
    Bp, Sp, _ = x_prompt.shape
    Bs, Ss, _ = x_sample.shape
    l = 0
    w = _layer_weights(l, norm_mix, w_in, w_a2, b_a, gla_head_norm, ret_head_norm, w_pa, w_pb, w_o, norm_xattn,
                       norm_mem, w_xq, w_xk, w_xv, w_xo, norm_ffn, peer_wq, peer_subkeys, peer_u, peer_v)
    gfin = norm_final.reshape(1, D_MODEL)

    cos_p, sin_p = _rotary_tables(jnp.arange(Sp, dtype=jnp.int32))
    mk, mv = _mem_kv(mem_prompt.reshape(Bp * N_MEM, D_MODEL), w["norm_mem"], w["xk"], w["xv"])
    zeros = jnp.zeros((Bp, GLA_HEADS, GLA_DK, GLA_DV), f32)
    yp, sgp, srp = _layer(x_prompt.reshape(Bp * Sp, D_MODEL), Bp, Sp, SCAN_CHUNK, SCAN_CHUNK, cos_p, sin_p,
                          mk.reshape(Bp, N_MEM, D_MODEL), mv.reshape(Bp, N_MEM, D_MODEL), zeros, zeros, w, gfin)

    pad = SAMPLE_PAD - Ss
    xs = jnp.pad(x_sample, ((0, 0), (0, pad), (0, 0))).reshape(Bs * SAMPLE_PAD, D_MODEL)
    cos_s, sin_s = _rotary_tables(PAST_LEN + jnp.arange(SAMPLE_PAD, dtype=jnp.int32))
    ys, sgs, srs = _layer(xs, Bs, SAMPLE_PAD, SAMPLE_PAD, Ss, cos_s, sin_s,
                          cache_mem_k[l].reshape(Bs, N_MEM, D_MODEL).astype(bf16),
                          cache_mem_v[l].reshape(Bs, N_MEM, D_MODEL).astype(bf16),
                          state_gla[l], state_ret[l], w, gfin)
    ys = ys.reshape(Bs, SAMPLE_PAD, D_MODEL)[:, :Ss]

    kv_shape = (1, Bp, N_MEM, XA_HEADS, XA_DH)
    return (yp.reshape(Bp, Sp, D_MODEL), ys, sgp[None], srp[None], mk.reshape(kv_shape), mv.reshape(kv_shape),
            sgs[None], srs[None])
```

```python
import functools
import math

import jax
import jax.numpy as jnp
from jax import lax
from jax.experimental import pallas as pl
from jax.experimental.pallas import tpu as pltpu

f32 = jnp.float32
bf16 = jnp.bfloat16

D_MODEL = 1024
PAST_LEN = 16384
GLA_HEADS, GLA_DK, GLA_DV, GLA_RANK, GLA_TEMP = 4, 64, 128, 16, 16.0
RET_HEADS, RET_DK, RET_DV = 4, 64, 128
ROPE_BASE = 10000.0
N_MEM = 256
XA_HEADS = 4
XA_DH = D_MODEL // XA_HEADS
PEER_HEADS, PEER_NKEYS, PEER_DQ, PEER_TOPK = 8, 128, 256, 16
PEER_N = PEER_NKEYS * PEER_NKEYS
EPS = 1e-6

GLA_QK = GLA_HEADS * GLA_DK
GLA_V = GLA_HEADS * GLA_DV
RET_QK = RET_HEADS * RET_DK
RET_V = RET_HEADS * RET_DV
IN_SIZES = (GLA_QK, GLA_QK, GLA_V, GLA_RANK, GLA_V, RET_QK, RET_QK, RET_V, RET_V, D_MODEL, D_MODEL)

LANES = 128
SUBLANES = 8
RANK_PAD = LANES
GLA_COLS = 2 * GLA_QK + GLA_V + RANK_PAD + GLA_V
RET_COLS = 2 * RET_QK + 2 * RET_V + 2 * RET_QK
Z_COLS = 2 * D_MODEL
SCAN_CHUNK = 64
SCAN_SEQS = 4
SAMPLE_PAD = SUBLANES
XATTN_ROWS = 512
XATTN_SEQS = 8
VMEM_LIMIT = 52 * 1024 * 1024


def _cparams(sem):
    return pltpu.CompilerParams(dimension_semantics=sem, vmem_limit_bytes=VMEM_LIMIT)


def _rms(x, g):
    return x * lax.rsqrt(jnp.mean(x * x, axis=-1, keepdims=True) + EPS) * g


def _mm(a, b):
    return jnp.dot(a.astype(bf16), b.astype(bf16), preferred_element_type=f32)


def _mm_nt(a, b):
    return lax.dot_general(a.astype(bf16), b.astype(bf16), (((1,), (1,)), ((), ())), preferred_element_type=f32)


def _mm_tn(a, b):
    return lax.dot_general(a.astype(bf16), b.astype(bf16), (((0,), (0,)), ((), ())), preferred_element_type=f32)


def _in_proj_body(x_ref, g_ref, wg_ref, wr_ref, wz_ref, og_ref, or_ref, oz_ref):
    xn = _rms(x_ref[...], g_ref[...]).astype(bf16)
    og_ref[...] = jnp.dot(xn, wg_ref[...], preferred_element_type=f32)
    or_ref[...] = jnp.dot(xn, wr_ref[...], preferred_element_type=f32)
    oz_ref[...] = jnp.dot(xn, wz_ref[...], preferred_element_type=f32)


def _in_proj(x, g, wg, wr, wz, tm=256):
    T = x.shape[0]
    row = lambda i: (i, 0)
    fixed = lambda i: (0, 0)
    return pl.pallas_call(
        _in_proj_body,
        grid=(T // tm,),
        in_specs=[pl.BlockSpec((tm, D_MODEL), row), pl.BlockSpec((1, D_MODEL), fixed),
                  pl.BlockSpec((D_MODEL, GLA_COLS), fixed), pl.BlockSpec((D_MODEL, RET_COLS), fixed),
                  pl.BlockSpec((D_MODEL, Z_COLS), fixed)],
        out_specs=[pl.BlockSpec((tm, GLA_COLS), row), pl.BlockSpec((tm, RET_COLS), row),
                   pl.BlockSpec((tm, Z_COLS), row)],
        out_shape=[jax.ShapeDtypeStruct((T, GLA_COLS), f32), jax.ShapeDtypeStruct((T, RET_COLS), f32),
                   jax.ShapeDtypeStruct((T, Z_COLS), f32)],
        compiler_params=_cparams(("parallel",)),
        name="in_proj",
    )(x, g, wg, wr, wz)


def _chunk_heads(q, k, v, b, gate, hnorm_ref, state_ref, o_ref, C):
    mid = C // 2 - 1
    b_mid = b[mid:mid + 1, :]
    b_last = b[C - 1:C, :]
    q_in = q * jnp.exp(b)
    q_e = q * jnp.exp(b - b_mid)
    k_e = k * jnp.exp(b_mid - b)
    k_d = k * jnp.exp(b_last - b)
    decay_col = jnp.exp(jnp.broadcast_to(b_last, (SUBLANES, b.shape[1])).T[:, 0:1])
    lane_head = lax.broadcasted_iota(jnp.int32, (C, 4 * 64), 1) // 64
    stack = lambda x: jnp.concatenate([jnp.where(lane_head == h, x, 0.0) for h in range(4)], axis=0)
    rows = lax.broadcasted_iota(jnp.int32, (4 * C, C), 0) % C
    cols = lax.broadcasted_iota(jnp.int32, (4 * C, C), 1)
    state = state_ref[...]
    att = jnp.where(rows >= cols, _mm_nt(stack(q_e), k_e), 0.0)
    intra = _mm(att, v)
    inter = _mm(stack(q_in), state)
    update = _mm_tn(k_d, v)
    for h in range(4):
        ks = slice(h * 64, (h + 1) * 64)
        vs = slice(h * 128, (h + 1) * 128)
        cs = slice(h * C, (h + 1) * C)
        o_h = inter[cs, :] + intra[cs, vs]
        state_ref[ks, :] = decay_col[ks, :] * state[ks, :] + update[ks, vs]
        g_h = gate[:, vs]
        o_ref[:, vs] = _rms(o_h, hnorm_ref[h:h + 1, :]) * (g_h * jax.nn.sigmoid(g_h))


def _scan_body(gla_ref, ret_ref, wa2_ref, ba_ref, gn_ref, rn_ref, cos_ref, sin_ref, lg_ref, sg0_ref, sr0_ref,
               oa_ref, ob_ref, sg_ref, sr_ref, sg_scr, sr_scr, *, C, valid):
    c = pl.program_id(1)

    @pl.when(c == 0)
    def _():
        sg_scr[...] = sg0_ref[...]
        sr_scr[...] = sr0_ref[...]

    row = lax.broadcasted_iota(jnp.int32, (C, 1), 0)
    tri = (lax.broadcasted_iota(jnp.int32, (C, C), 0) >= lax.broadcasted_iota(jnp.int32, (C, C), 1)).astype(bf16)
    cos = cos_ref[...]
    sin = sin_ref[...]

    for s in range(SCAN_SEQS):
        g = gla_ref[s]
        q = g[:, 0:GLA_QK] * GLA_DK ** -0.5
        k = g[:, GLA_QK:2 * GLA_QK]
        v = g[:, 2 * GLA_QK:2 * GLA_QK + GLA_V]
        lowrank = g[:, 2 * GLA_QK + GLA_V:2 * GLA_QK + GLA_V + RANK_PAD]
        gate = g[:, 2 * GLA_QK + GLA_V + RANK_PAD:]
        log_a = jax.nn.log_sigmoid(_mm(lowrank, wa2_ref[...]) + ba_ref[...]) / GLA_TEMP
        if valid < C:
            log_a = jnp.where(row < valid, log_a, 0.0)
        hi = log_a.astype(bf16)
        rest = log_a - hi.astype(f32)
        mid = rest.astype(bf16)
        lo = (rest - mid.astype(f32)).astype(bf16)
        parts = jnp.dot(tri, jnp.concatenate([hi, mid, lo], axis=1), preferred_element_type=f32)
        b = parts[:, 0:GLA_QK] + parts[:, GLA_QK:2 * GLA_QK] + parts[:, 2 * GLA_QK:]
        _chunk_heads(q, k, v, b, gate, gn_ref, sg_scr.at[s], oa_ref.at[s], C)

        r = ret_ref[s]
        q = r[:, 0:RET_QK] * cos + r[:, 2 * RET_QK + 2 * RET_V:3 * RET_QK + 2 * RET_V] * sin
        k = (r[:, RET_QK:2 * RET_QK] * cos + r[:, 3 * RET_QK + 2 * RET_V:] * sin) * RET_DK ** -0.5
        v = r[:, 2 * RET_QK:2 * RET_QK + RET_V]
        gate = r[:, 2 * RET_QK + RET_V:2 * RET_QK + 2 * RET_V]
        steps = jnp.minimum(row + 1, valid).astype(f32)
        b = steps * lg_ref[...]
        _chunk_heads(q, k, v, b, gate, rn_ref, sr_scr.at[s], ob_ref.at[s], C)

    @pl.when(c == pl.num_programs(1) - 1)
    def _():
        sg_ref[...] = sg_scr[...]
        sr_ref[...] = sr_scr[...]


def _scan(gla, ret, wa2, ba, gn, rn, cos, sin, lg, sg0, sr0, B, S, C, valid):
    n = S // C
    G = SCAN_SEQS
    assert B % G == 0
    gla = gla.reshape(B, S, GLA_COLS)
    ret = ret.reshape(B, S, RET_COLS)
    tok = lambda b, c: (b, c, 0)
    fixed = lambda b, c: (0, 0)
    pos = lambda b, c: (c, 0)
    st = lambda b, c: (b, 0, 0)
    st_block = (G, GLA_HEADS * GLA_DK, GLA_DV)
    sg0 = sg0.reshape(B, GLA_HEADS * GLA_DK, GLA_DV)
    sr0 = sr0.reshape(B, RET_HEADS * RET_DK, RET_DV)
    oa, ob, sg, sr = pl.pallas_call(
        functools.partial(_scan_body, C=C, valid=valid),
        grid=(B // G, n),
        in_specs=[pl.BlockSpec((G, C, GLA_COLS), tok), pl.BlockSpec((G, C, RET_COLS), tok),
                  pl.BlockSpec((RANK_PAD, GLA_QK), fixed), pl.BlockSpec((1, GLA_QK), fixed),
                  pl.BlockSpec((GLA_HEADS, GLA_DV), fixed), pl.BlockSpec((RET_HEADS, RET_DV), fixed),
                  pl.BlockSpec((C, RET_QK), pos), pl.BlockSpec((C, RET_QK), pos),
                  pl.BlockSpec((1, RET_QK), fixed),
                  pl.BlockSpec(st_block, st), pl.BlockSpec(st_block, st)],
        out_specs=[pl.BlockSpec((G, C, GLA_V), tok), pl.BlockSpec((G, C, RET_V), tok),
                   pl.BlockSpec(st_block, st), pl.BlockSpec(st_block, st)],
        out_shape=[jax.ShapeDtypeStruct((B, S, GLA_V), f32), jax.ShapeDtypeStruct((B, S, RET_V), f32),
                   jax.ShapeDtypeStruct((B,) + st_block[1:], f32), jax.ShapeDtypeStruct((B,) + st_block[1:], f32)],
        scratch_shapes=[pltpu.VMEM(st_block, f32), pltpu.VMEM(st_block, f32)],
        compiler_params=_cparams(("parallel", "arbitrary")),
        name="scan",
    )(gla, ret, wa2, ba, gn, rn, cos, sin, lg, sg0, sr0)
    state_shape = (B, GLA_HEADS, GLA_DK, GLA_DV)
    return oa.reshape(B * S, GLA_V), ob.reshape(B * S, RET_V), sg.reshape(state_shape), sr.reshape(state_shape)


def _merge_body(x_ref, oa_ref, ob_ref, z_ref, wpa_ref, wpb_ref, wo_ref, gx_ref, wxq_ref, h_ref, q_ref):
    z = z_ref[...]
    merged = (jax.nn.sigmoid(z[:, :D_MODEL]) * _mm(oa_ref[...], wpa_ref[...])
              + jax.nn.sigmoid(z[:, D_MODEL:]) * _mm(ob_ref[...], wpb_ref[...]))
    h = x_ref[...] + _mm(merged, wo_ref[...])
    h_ref[...] = h
    q_ref[...] = _mm(_rms(h, gx_ref[...]), wxq_ref[...])


def _merge(x, oa, ob, z, wpa, wpb, wo, gx, wxq, tm=256):
    T = x.shape[0]
    row = lambda i: (i, 0)
    fixed = lambda i: (0, 0)
    return pl.pallas_call(
        _merge_body,
        grid=(T // tm,),
        in_specs=[pl.BlockSpec((tm, D_MODEL), row), pl.BlockSpec((tm, GLA_V), row), pl.BlockSpec((tm, RET_V), row),
                  pl.BlockSpec((tm, Z_COLS), row),
                  pl.BlockSpec((GLA_V, D_MODEL), fixed), pl.BlockSpec((RET_V, D_MODEL), fixed),
                  pl.BlockSpec((D_MODEL, D_MODEL), fixed), pl.BlockSpec((1, D_MODEL), fixed),
                  pl.BlockSpec((D_MODEL, D_MODEL), fixed)],
        out_specs=[pl.BlockSpec((tm, D_MODEL), row), pl.BlockSpec((tm, D_MODEL), row)],
        out_shape=[jax.ShapeDtypeStruct((T, D_MODEL), f32), jax.ShapeDtypeStruct((T, D_MODEL), f32)],
        compiler_params=_cparams(("parallel",)),
        name="merge",
    )(x, oa, ob, z, wpa, wpb, wo, gx, wxq)


def _mem_kv_body(m_ref, g_ref, wk_ref, wv_ref, k_ref, v_ref):
    mn = _rms(m_ref[...], g_ref[...]).astype(bf16)
    k_ref[...] = jnp.dot(mn, wk_ref[...], preferred_element_type=f32)
    v_ref[...] = jnp.dot(mn, wv_ref[...], preferred_element_type=f32)


def _mem_kv(mem, g, wk, wv, tm=256):
    T = mem.shape[0]
    row = lambda i: (i, 0)
    fixed = lambda i: (0, 0)
    return pl.pallas_call(
        _mem_kv_body,
        grid=(T // tm,),
        in_specs=[pl.BlockSpec((tm, D_MODEL), row), pl.BlockSpec((1, D_MODEL), fixed),
                  pl.BlockSpec((D_MODEL, D_MODEL), fixed), pl.BlockSpec((D_MODEL, D_MODEL), fixed)],
        out_specs=[pl.BlockSpec((tm, D_MODEL), row), pl.BlockSpec((tm, D_MODEL), row)],
        out_shape=[jax.ShapeDtypeStruct((T, D_MODEL), f32), jax.ShapeDtypeStruct((T, D_MODEL), f32)],
        compiler_params=_cparams(("parallel",)),
        name="mem_kv",
    )(mem, g, wk, wv)


def _xattn_body(q_ref, k_ref, v_ref, o_ref, *, seqs, tq):
    for s in range(seqs):
        rows = slice(s * tq, (s + 1) * tq)
        q = q_ref[rows, :]
        for h in range(XA_HEADS):
            hs = slice(h * XA_DH, (h + 1) * XA_DH)
            sc = _mm_nt(q[:, hs], k_ref[s, :, hs]) * XA_DH ** -0.5
            p = jnp.exp(sc - jnp.max(sc, axis=-1, keepdims=True))
            p = p / jnp.sum(p, axis=-1, keepdims=True)
            o_ref[rows, hs] = _mm(p, v_ref[s, :, hs])


def _xattn(q, mk, mv, B, S, tq, seqs):
    n = S // tq
    assert seqs == 1 or n == 1
    tok = lambda b, j: (b * n + j, 0)
    mem_spec = pl.BlockSpec((seqs, N_MEM, D_MODEL), lambda b, j: (b, 0, 0))
    return pl.pallas_call(
        functools.partial(_xattn_body, seqs=seqs, tq=tq),
        grid=(B // seqs, n),
        in_specs=[pl.BlockSpec((seqs * tq, D_MODEL), tok), mem_spec, mem_spec],
        out_specs=pl.BlockSpec((seqs * tq, D_MODEL), tok),
        out_shape=jax.ShapeDtypeStruct((B * S, D_MODEL), f32),
        compiler_params=_cparams(("parallel", "parallel")),
        name="xattn",
    )(q, mk, mv)


def _merge_exchange_network(n):
    pairs = []
    p = 1
    while p < n:
        k = p
        while k >= 1:
            for j in range(k % p, n - k, 2 * k):
                for i in range(min(k, n - j - k)):
                    if (i + j) // (2 * p) == (i + j + k) // (2 * p):
                        pairs.append((i + j, i + j + k))
            k //= 2
        p *= 2
    return pairs


def _compare_exchange(x, i, j):
    a, b = x[i], x[j]
    if b is None:
        return
    if a is None:
        x[i], x[j] = b, None
        return
    x[i], x[j] = jnp.maximum(a, b), jnp.minimum(a, b)


def _top16(tiles):
    n = PEER_TOPK
    x = list(tiles) + [None] * (n - len(tiles))
    for i, j in _merge_exchange_network(n):
        _compare_exchange(x, i, j)
    for shift in (4, 2, 1):
        merged = []
        for i in range(n):
            a, b = x[i], x[n - 1 - i]
            b = None if b is None else pltpu.roll(b, shift, 0)
            merged.append(b if a is None else (a if b is None else jnp.maximum(a, b)))
        x = merged
        d = n // 2
        while d >= 1:
            for i in range(n):
                if (i & d) == 0:
                    _compare_exchange(x, i, i + d)
            d //= 2
    return x


def _sublane_sum(x):
    for shift in (4, 2, 1):
        x = x + pltpu.roll(x, shift, 0)
    return x


def _rows_to_sublanes(v, sub):
    out = v[SUBLANES - 1]
    for r in range(SUBLANES - 2, -1, -1):
        out = jnp.where(sub == r, v[r], out)
    return out


def _peer_front_body(h_ref, o_ref, wxo_ref, gf_ref, wq_ref, sk_ref,
                     h2_ref, hn_ref, cn1_ref, e1_ref, rk2_ref, e2_ref, hn_scr, *, tm):
    hd = pl.program_id(1)

    @pl.when(hd == 0)
    def _():
        h2 = h_ref[...] + _mm(o_ref[...], wxo_ref[...])
        h2_ref[...] = h2
        hn = _rms(h2, gf_ref[...]).astype(bf16)
        hn_scr[...] = hn
        hn_ref[...] = hn

    pq = jnp.dot(hn_scr[...], wq_ref[0], preferred_element_type=f32)
    half = PEER_DQ // 2
    s1 = _mm_nt(sk_ref[0, 0], pq[:, :half])
    s2 = _mm_nt(sk_ref[0, 1], pq[:, half:])
    sub = lax.broadcasted_iota(jnp.int32, (SUBLANES, LANES), 0)
    for t in range(tm // LANES):
        ls = slice(t * LANES, (t + 1) * LANES)
        a1 = s1[:, ls]
        a2 = s2[:, ls]
        v1 = _top16([a1[SUBLANES * i:SUBLANES * (i + 1)] for i in range(PEER_NKEYS // SUBLANES)])
        v2 = _top16([a2[SUBLANES * i:SUBLANES * (i + 1)] for i in range(PEER_NKEYS // SUBLANES)])
        v1_hi = _rows_to_sublanes(v1[SUBLANES:], sub)
        v2_lo = _rows_to_sublanes(v2[:SUBLANES], sub)
        v2_hi = _rows_to_sublanes(v2[SUBLANES:], sub)
        cand = [v1[0] + v2_lo, v1[0] + v2_hi]
        cand += [v1[r] + v2_lo for r in range(1, SUBLANES)]
        cand += [v1_hi + v2[0]]
        top = _top16(cand)
        z = _sublane_sum(jnp.exp(_rows_to_sublanes(top[:SUBLANES], sub) - top[0])
                         + jnp.exp(_rows_to_sublanes(top[SUBLANES:], sub) - top[0]))
        tau = top[PEER_TOPK - 1]
        picked = lambda c: jnp.where(c >= tau, 1.0, 0.0)
        count = [_sublane_sum(picked(cand[0]) + picked(cand[1]))]
        count += [_sublane_sum(picked(cand[r + 1])) for r in range(1, SUBLANES)]
        count += [picked(v1[r] + v2[0]) for r in range(SUBLANES, PEER_TOPK)]
        count1 = []
        rank2 = []
        for i in range(PEER_NKEYS // SUBLANES):
            k1 = a1[SUBLANES * i:SUBLANES * (i + 1)]
            k2 = a2[SUBLANES * i:SUBLANES * (i + 1)]
            c1 = jnp.zeros_like(k1)
            r2 = jnp.full_like(k2, float(PEER_TOPK))
            for r in range(PEER_TOPK - 1, -1, -1):
                c1 = jnp.where(k1 == v1[r], count[r], c1)
                r2 = jnp.where(k2 == v2[r], float(r), r2)
            count1.append(c1)
            rank2.append(r2)
        cn1_ref[0, :, ls] = jnp.concatenate(count1, axis=0)
        e1_ref[0, :, ls] = jnp.exp(a1 - v1[0][0:1]) / z[0:1]
        rk2_ref[0, :, ls] = jnp.concatenate(rank2, axis=0).astype(bf16)
        e2_ref[0, :, ls] = jnp.exp(a2 - v2[0][0:1]).astype(bf16)


def _peer_front(h, o, wxo, gf, wq, sk, tm=512):
    T = h.shape[0]
    assert T % tm == 0
    row = lambda i, hd: (i, 0)
    fixed = lambda i, hd: (0, 0)
    head_t = lambda i, hd: (hd, 0, i)
    words = jax.ShapeDtypeStruct((PEER_HEADS, PEER_NKEYS, T), f32)
    halfs = jax.ShapeDtypeStruct((PEER_HEADS, PEER_NKEYS, T), bf16)
    sc_spec = pl.BlockSpec((1, PEER_NKEYS, tm), head_t)
    return pl.pallas_call(
        functools.partial(_peer_front_body, tm=tm),
        grid=(T // tm, PEER_HEADS),
        in_specs=[pl.BlockSpec((tm, D_MODEL), row), pl.BlockSpec((tm, D_MODEL), row),
                  pl.BlockSpec((D_MODEL, D_MODEL), fixed), pl.BlockSpec((1, D_MODEL), fixed),
                  pl.BlockSpec((1, D_MODEL, PEER_DQ), lambda i, hd: (hd, 0, 0)),
                  pl.BlockSpec((1, 2, PEER_NKEYS, PEER_DQ // 2), lambda i, hd: (hd, 0, 0, 0))],
        out_specs=[pl.BlockSpec((tm, D_MODEL), row), pl.BlockSpec((tm, D_MODEL), row),
                   sc_spec, sc_spec, sc_spec, sc_spec],
        out_shape=[jax.ShapeDtypeStruct((T, D_MODEL), f32), jax.ShapeDtypeStruct((T, D_MODEL), bf16),
                   words, words, halfs, halfs],
        scratch_shapes=[pltpu.VMEM((tm, D_MODEL), bf16)],
        compiler_params=_cparams(("parallel", "arbitrary")),
        name="peer_front",
    )(h, o, wxo, gf, wq, sk)


PEER_ROWS_PER_BLOCK = SUBLANES
PEER_BLOCK = PEER_ROWS_PER_BLOCK * PEER_NKEYS
GATE_ROWS = 8 * SUBLANES
PEER_DENSE_TOKENS = 1024


def _peer_dense_body(hn_ref, u_ref, vt_ref, cn1_ref, e1_ref, rk2_ref, e2_ref, h2_ref, gfin_ref,
                     y_ref, at_scr, w_scr, acc_scr, *, tm):
    j = pl.program_id(1)

    @pl.when(j == 0)
    def _():
        acc_scr[...] = jnp.zeros_like(acc_scr)

    at_scr[...] = lax.dot_general(u_ref[...], hn_ref[...], (((1,), (1,)), ((), ())), preferred_element_type=f32)
    chunks = PEER_NKEYS // GATE_ROWS

    def row_tile(ref, hd, r, ls):
        return jnp.broadcast_to(ref[hd, r:r + 1, ls].astype(bf16), (GATE_ROWS, LANES))

    def gate_tile(i, carry):
        ls = pl.ds(pl.multiple_of((i // chunks) * LANES, LANES), LANES)
        c0 = pl.multiple_of((i % chunks) * GATE_ROWS, GATE_ROWS)
        gates = [None] * PEER_ROWS_PER_BLOCK
        for hd in range(PEER_HEADS):
            rk2 = rk2_ref[hd, pl.ds(c0, GATE_ROWS), ls]
            e2 = e2_ref[hd, pl.ds(c0, GATE_ROWS), ls]
            for r in range(PEER_ROWS_PER_BLOCK):
                picked = jnp.clip(row_tile(cn1_ref, hd, r, ls) - rk2, 0.0, 1.0)
                term = (row_tile(e1_ref, hd, r, ls) * e2) * picked
                gates[r] = term if gates[r] is None else gates[r] + term
        for r in range(PEER_ROWS_PER_BLOCK):
            rs = pl.ds(pl.multiple_of(r * PEER_NKEYS + c0, GATE_ROWS), GATE_ROWS)
            a = at_scr[rs, ls]
            act = 0.5 * a * (1.0 + lax.erf(a * math.sqrt(0.5)))
            w_scr[rs, ls] = act.astype(bf16) * gates[r]
        return carry

    lax.fori_loop(0, (tm // LANES) * chunks, gate_tile, 0)
    acc_scr[...] += jnp.dot(vt_ref[...], w_scr[...], preferred_element_type=f32)

    @pl.when(j == pl.num_programs(1) - 1)
    def _():
        y_ref[...] = _rms(h2_ref[...] + acc_scr[...].T, gfin_ref[...])


def _peer_dense(hn, u, vt, cn1, e1, rk2, e2, h2, gfin):
    T = hn.shape[0]
    nb = PEER_BLOCK
    tm = min(T, PEER_DENSE_TOKENS)
    assert T % tm == 0
    row = lambda i, j: (i, 0)
    once = pl.Buffered(1)
    sc1 = pl.BlockSpec((PEER_HEADS, PEER_ROWS_PER_BLOCK, tm), lambda i, j: (0, j, i))
    sc2 = pl.BlockSpec((PEER_HEADS, PEER_NKEYS, tm), lambda i, j: (0, 0, i), pipeline_mode=once)
    return pl.pallas_call(
        functools.partial(_peer_dense_body, tm=tm),
        grid=(T // tm, PEER_N // nb),
        in_specs=[pl.BlockSpec((tm, D_MODEL), row),
                  pl.BlockSpec((nb, D_MODEL), lambda i, j: (j, 0)),
                  pl.BlockSpec((D_MODEL, nb), lambda i, j: (0, j)),
                  sc1, sc1, sc2, sc2,
                  pl.BlockSpec((tm, D_MODEL), row, pipeline_mode=once),
                  pl.BlockSpec((1, D_MODEL), lambda i, j: (0, 0))],
        out_specs=pl.BlockSpec((tm, D_MODEL), row),
        out_shape=jax.ShapeDtypeStruct((T, D_MODEL), f32),
        scratch_shapes=[pltpu.VMEM((nb, tm), f32), pltpu.VMEM((nb, tm), bf16), pltpu.VMEM((D_MODEL, tm), f32)],
        compiler_params=_cparams(("parallel", "arbitrary")),
        name="peer_dense",
    )(hn, u, vt, cn1, e1, rk2, e2, h2, gfin)


def _rotary_tables(pos):
    half = RET_DK // 2
    inv = ROPE_BASE ** (-jnp.arange(half, dtype=f32) / half)
    ang = pos.astype(f32)[:, None] * inv[None, :]
    cos = jnp.cos(ang)
    sin = jnp.sin(ang)
    cos = jnp.tile(jnp.concatenate([cos, cos], axis=-1), (1, RET_HEADS))
    sin = jnp.tile(jnp.concatenate([-sin, sin], axis=-1), (1, RET_HEADS))
    return cos, sin


def _layer_weights(l, norm_mix, w_in, w_a2, b_a, gla_head_norm, ret_head_norm, w_pa, w_pb, w_o, norm_xattn,
                   norm_mem, w_xq, w_xk, w_xv, w_xo, norm_ffn, peer_wq, peer_subkeys, peer_u, peer_v):
    offs = [0]
    for s in IN_SIZES:
        offs.append(offs[-1] + s)
    cols = [w_in[l][:, offs[i]:offs[i + 1]] for i in range(len(IN_SIZES))]
    gq, gk, gv, glr, gr, rq, rk, rv, rg, za, zb = cols
    glr = jnp.pad(glr, ((0, 0), (0, RANK_PAD - GLA_RANK)))
    half = RET_DK // 2
    j = jnp.arange(RET_QK)
    partner = jnp.where((j % RET_DK) < half, j + half, j - half)
    w = {}
    w["gla"] = jnp.concatenate([gq, gk, gv, glr, gr], axis=1).astype(bf16)
    w["ret"] = jnp.concatenate([rq, rk, rv, rg, rq[:, partner], rk[:, partner]], axis=1).astype(bf16)
    w["z"] = jnp.concatenate([za, zb], axis=1).astype(bf16)
    w["a2"] = jnp.pad(w_a2[l], ((0, RANK_PAD - GLA_RANK), (0, 0))).astype(bf16)
    w["ba"] = b_a[l].reshape(1, GLA_QK)
    w["gn"] = gla_head_norm[l]
    w["rn"] = ret_head_norm[l]
    log_gamma = jnp.log1p(-(2.0 ** (-5.0 - jnp.arange(RET_HEADS, dtype=f32))))
    w["lg"] = jnp.repeat(log_gamma, RET_DK).reshape(1, RET_QK)
    w["norm_mix"] = norm_mix[l].reshape(1, D_MODEL)
    w["pa"] = w_pa[l].astype(bf16)
    w["pb"] = w_pb[l].astype(bf16)
    w["o"] = w_o[l].astype(bf16)
    w["norm_xattn"] = norm_xattn[l].reshape(1, D_MODEL)
    w["norm_mem"] = norm_mem[l].reshape(1, D_MODEL)
    w["xq"] = w_xq[l].astype(bf16)
    w["xk"] = w_xk[l].astype(bf16)
    w["xv"] = w_xv[l].astype(bf16)
    w["xo"] = w_xo[l].astype(bf16)
    w["norm_ffn"] = norm_ffn[l].reshape(1, D_MODEL)
    w["wq"] = peer_wq[l].reshape(D_MODEL, PEER_HEADS, PEER_DQ).transpose(1, 0, 2).astype(bf16)
    w["sk"] = peer_subkeys[l].astype(bf16)
    w["u"] = peer_u[l].astype(bf16)
    w["vt"] = peer_v[l].T.astype(bf16)
    return w


def _layer(x, B, S, keep, C, valid, cos, sin, mk, mv, sg0, sr0, w, gfin):
    gla, ret, z = _in_proj(x, w["norm_mix"], w["gla"], w["ret"], w["z"])
    oa, ob, sg, sr = _scan(gla, ret, w["a2"], w["ba"], w["gn"], w["rn"], cos, sin, w["lg"], sg0, sr0, B, S, C, valid)
    h, q = _merge(x, oa, ob, z, w["pa"], w["pb"], w["o"], w["norm_xattn"], w["xq"])
    tq = min(S, XATTN_ROWS)
    o = _xattn(q, mk, mv, B, S, tq, XATTN_SEQS if tq == S else 1)
    if keep < S:
        h = h.reshape(B, S, D_MODEL)[:, :keep].reshape(B * keep, D_MODEL)
        o = o.reshape(B, S, D_MODEL)[:, :keep].reshape(B * keep, D_MODEL)
    h2, hn, cn1, e1, rk2, e2 = _peer_front(h, o, w["xo"], w["norm_ffn"], w["wq"], w["sk"])
    y = _peer_dense(hn, w["u"], w["vt"], cn1, e1, rk2, e2, h2, gfin)
    return y, sg, sr


def kernel(x_prompt, x_sample, mem_prompt, state_gla, state_ret, cache_mem_k, cache_mem_v, norm_mix, w_in, w_a2,
           b_a, gla_head_norm, ret_head_norm, w_pa, w_pb, w_o, norm_xattn, norm_mem, w_xq, w_xk, w_xv, w_xo,
           norm_ffn, peer_wq, peer_subkeys, peer_u, peer_v, norm_final):
    depth = w_in.shape[0]
    assert depth == 1, "the final norm is fused into the layer's last kernel"
    Bp, Sp, _ = x_prompt.shape
    Bs, Ss, _ = x_sample.shape
    l = 0
    w = _layer_weights(l, norm_mix, w_in, w_a2, b_a, gla_head_norm, ret_head_norm, w_pa, w_pb, w_o, norm_xattn,
                       norm_mem, w_xq, w_xk, w_xv, w_xo, norm_ffn, peer_wq, peer_subkeys, peer_u, peer_v)
    gfin = norm_final.reshape(1, D_MODEL)

    cos_p, sin_p = _rotary_tables(jnp.arange(Sp, dtype=jnp.int32))
    mk, mv = _mem_kv(mem_prompt.reshape(Bp * N_MEM, D_MODEL), w["norm_mem"], w["xk"], w["xv"])
    zeros = jnp.zeros((Bp, GLA_HEADS, GLA_DK, GLA_DV), f32)
    yp, sgp, srp = _layer(x_prompt.reshape(Bp * Sp, D_MODEL), Bp, Sp, Sp, SCAN_CHUNK, SCAN_CHUNK, cos_p, sin_p,
                          mk.reshape(Bp, N_MEM, D_MODEL), mv.reshape(Bp, N_MEM, D_MODEL), zeros, zeros, w, gfin)

    pad = SAMPLE_PAD - Ss
    xs = jnp.pad(x_sample, ((0, 0), (0, pad), (0, 0))).reshape(Bs * SAMPLE_PAD, D_MODEL)
    cos_s, sin_s = _rotary_tables(PAST_LEN + jnp.arange(SAMPLE_PAD, dtype=jnp.int32))
    ys, sgs, srs = _layer(xs, Bs, SAMPLE_PAD, Ss, SAMPLE_PAD, Ss, cos_s, sin_s,
                          cache_mem_k[l].reshape(Bs, N_MEM, D_MODEL).astype(bf16),
                          cache_mem_v[l].reshape(Bs, N_MEM, D_MODEL).astype(bf16),
                          state_gla[l], state_ret[l], w, gfin)
    ys = ys.reshape(Bs, Ss, D_MODEL)

    kv_shape = (1, Bp, N_MEM, XA_HEADS, XA_DH)
    return (yp.reshape(Bp, Sp, D_MODEL), ys, sgp[None], srp[None], mk.reshape(kv_shape), mv.reshape(kv_shape),
            sgs[None], srs[None])
```

```python
import functools
import math

import jax
import jax.numpy as jnp
from jax import lax
from jax.experimental import pallas as pl
from jax.experimental.pallas import tpu as pltpu

f32 = jnp.float32
bf16 = jnp.bfloat16

D_MODEL = 1024
PAST_LEN = 16384
GLA_HEADS, GLA_DK, GLA_DV, GLA_RANK, GLA_TEMP = 4, 64, 128, 16, 16.0
RET_HEADS, RET_DK, RET_DV = 4, 64, 128
ROPE_BASE = 10000.0
N_MEM = 256
XA_HEADS = 4
XA_DH = D_MODEL // XA_HEADS
PEER_HEADS, PEER_NKEYS, PEER_DQ, PEER_TOPK = 8, 128, 256, 16
PEER_N = PEER_NKEYS * PEER_NKEYS
EPS = 1e-6

GLA_QK = GLA_HEADS * GLA_DK
GLA_V = GLA_HEADS * GLA_DV
RET_QK = RET_HEADS * RET_DK
RET_V = RET_HEADS * RET_DV
IN_SIZES = (GLA_QK, GLA_QK, GLA_V, GLA_RANK, GLA_V, RET_QK, RET_QK, RET_V, RET_V, D_MODEL, D_MODEL)

LANES = 128
SUBLANES = 8
RANK_PAD = LANES
GLA_COLS = 2 * GLA_QK + GLA_V + RANK_PAD + GLA_V
RET_COLS = 2 * RET_QK + 2 * RET_V + 2 * RET_QK
Z_COLS = 2 * D_MODEL
SCAN_CHUNK = 64
SCAN_SEQS = 4
SAMPLE_PAD = SUBLANES
XATTN_ROWS = 512
XATTN_SEQS = 8
VMEM_LIMIT = 52 * 1024 * 1024


def _cparams(sem):
    return pltpu.CompilerParams(dimension_semantics=sem, vmem_limit_bytes=VMEM_LIMIT)


def _rms(x, g):
    return x * lax.rsqrt(jnp.mean(x * x, axis=-1, keepdims=True) + EPS) * g


def _mm(a, b):
    return jnp.dot(a.astype(bf16), b.astype(bf16), preferred_element_type=f32)


def _mm_nt(a, b):
    return lax.dot_general(a.astype(bf16), b.astype(bf16), (((1,), (1,)), ((), ())), preferred_element_type=f32)


def _mm_tn(a, b):
    return lax.dot_general(a.astype(bf16), b.astype(bf16), (((0,), (0,)), ((), ())), preferred_element_type=f32)


def _in_proj_body(x_ref, g_ref, wg_ref, wr_ref, wz_ref, og_ref, or_ref, oz_ref):
    xn = _rms(x_ref[...], g_ref[...]).astype(bf16)
    og_ref[...] = jnp.dot(xn, wg_ref[...], preferred_element_type=f32)
    or_ref[...] = jnp.dot(xn, wr_ref[...], preferred_element_type=f32)
    oz_ref[...] = jnp.dot(xn, wz_ref[...], preferred_element_type=f32)


def _in_proj(x, g, wg, wr, wz, tm=256):
    T = x.shape[0]
    row = lambda i: (i, 0)
    fixed = lambda i: (0, 0)
    return pl.pallas_call(
        _in_proj_body,
        grid=(T // tm,),
        in_specs=[pl.BlockSpec((tm, D_MODEL), row), pl.BlockSpec((1, D_MODEL), fixed),
                  pl.BlockSpec((D_MODEL, GLA_COLS), fixed), pl.BlockSpec((D_MODEL, RET_COLS), fixed),
                  pl.BlockSpec((D_MODEL, Z_COLS), fixed)],
        out_specs=[pl.BlockSpec((tm, GLA_COLS), row), pl.BlockSpec((tm, RET_COLS), row),
                   pl.BlockSpec((tm, Z_COLS), row)],
        out_shape=[jax.ShapeDtypeStruct((T, GLA_COLS), f32), jax.ShapeDtypeStruct((T, RET_COLS), f32),
                   jax.ShapeDtypeStruct((T, Z_COLS), f32)],
        compiler_params=_cparams(("parallel",)),
        name="in_proj",
    )(x, g, wg, wr, wz)


def _chunk_heads(q, k, v, b, gate, hnorm_ref, state_ref, o_ref, C):
    mid = C // 2 - 1
    b_mid = b[mid:mid + 1, :]
    b_last = b[C - 1:C, :]
    q_in = q * jnp.exp(b)
    q_e = q * jnp.exp(b - b_mid)
    k_e = k * jnp.exp(b_mid - b)
    k_d = k * jnp.exp(b_last - b)
    decay_col = jnp.exp(jnp.broadcast_to(b_last, (SUBLANES, b.shape[1])).T[:, 0:1])
    lane_head = lax.broadcasted_iota(jnp.int32, (C, 4 * 64), 1) // 64
    stack = lambda x: jnp.concatenate([jnp.where(lane_head == h, x, 0.0) for h in range(4)], axis=0)
    rows = lax.broadcasted_iota(jnp.int32, (4 * C, C), 0) % C
    cols = lax.broadcasted_iota(jnp.int32, (4 * C, C), 1)
    state = state_ref[...]
    att = jnp.where(rows >= cols, _mm_nt(stack(q_e), k_e), 0.0)
    intra = _mm(att, v)
    inter = _mm(stack(q_in), state)
    update = _mm_tn(k_d, v)
    for h in range(4):
        ks = slice(h * 64, (h + 1) * 64)
        vs = slice(h * 128, (h + 1) * 128)
        cs = slice(h * C, (h + 1) * C)
        o_h = inter[cs, :] + intra[cs, vs]
        state_ref[ks, :] = decay_col[ks, :] * state[ks, :] + update[ks, vs]
        g_h = gate[:, vs]
        o_ref[:, vs] = _rms(o_h, hnorm_ref[h:h + 1, :]) * (g_h * jax.nn.sigmoid(g_h))


def _scan_body(gla_ref, ret_ref, wa2_ref, ba_ref, gn_ref, rn_ref, cos_ref, sin_ref, lg_ref, sg0_ref, sr0_ref,
               oa_ref, ob_ref, sg_ref, sr_ref, sg_scr, sr_scr, *, C, valid):
    c = pl.program_id(1)

    @pl.when(c == 0)
    def _():
        sg_scr[...] = sg0_ref[...]
        sr_scr[...] = sr0_ref[...]

    row = lax.broadcasted_iota(jnp.int32, (C, 1), 0)
    tri = (lax.broadcasted_iota(jnp.int32, (C, C), 0) >= lax.broadcasted_iota(jnp.int32, (C, C), 1)).astype(bf16)
    cos = cos_ref[...]
    sin = sin_ref[...]

    for s in range(SCAN_SEQS):
        g = gla_ref[s]
        q = g[:, 0:GLA_QK] * GLA_DK ** -0.5
        k = g[:, GLA_QK:2 * GLA_QK]
        v = g[:, 2 * GLA_QK:2 * GLA_QK + GLA_V]
        lowrank = g[:, 2 * GLA_QK + GLA_V:2 * GLA_QK + GLA_V + RANK_PAD]
        gate = g[:, 2 * GLA_QK + GLA_V + RANK_PAD:]
        log_a = jax.nn.log_sigmoid(_mm(lowrank, wa2_ref[...]) + ba_ref[...]) / GLA_TEMP
        if valid < C:
            log_a = jnp.where(row < valid, log_a, 0.0)
        hi = log_a.astype(bf16)
        rest = log_a - hi.astype(f32)
        mid = rest.astype(bf16)
        lo = (rest - mid.astype(f32)).astype(bf16)
        parts = jnp.dot(tri, jnp.concatenate([hi, mid, lo], axis=1), preferred_element_type=f32)
        b = parts[:, 0:GLA_QK] + parts[:, GLA_QK:2 * GLA_QK] + parts[:, 2 * GLA_QK:]
        _chunk_heads(q, k, v, b, gate, gn_ref, sg_scr.at[s], oa_ref.at[s], C)

        r = ret_ref[s]
        q = r[:, 0:RET_QK] * cos + r[:, 2 * RET_QK + 2 * RET_V:3 * RET_QK + 2 * RET_V] * sin
        k = (r[:, RET_QK:2 * RET_QK] * cos + r[:, 3 * RET_QK + 2 * RET_V:] * sin) * RET_DK ** -0.5
        v = r[:, 2 * RET_QK:2 * RET_QK + RET_V]
        gate = r[:, 2 * RET_QK + RET_V:2 * RET_QK + 2 * RET_V]
        steps = jnp.minimum(row + 1, valid).astype(f32)
        b = steps * lg_ref[...]
        _chunk_heads(q, k, v, b, gate, rn_ref, sr_scr.at[s], ob_ref.at[s], C)

    @pl.when(c == pl.num_programs(1) - 1)
    def _():
        sg_ref[...] = sg_scr[...]
        sr_ref[...] = sr_scr[...]


def _scan(gla, ret, wa2, ba, gn, rn, cos, sin, lg, sg0, sr0, B, S, C, valid):
    n = S // C
    G = SCAN_SEQS
    assert B % G == 0
    gla = gla.reshape(B, S, GLA_COLS)
    ret = ret.reshape(B, S, RET_COLS)
    tok = lambda b, c: (b, c, 0)
    fixed = lambda b, c: (0, 0)
    pos = lambda b, c: (c, 0)
    st = lambda b, c: (b, 0, 0)
    st_block = (G, GLA_HEADS * GLA_DK, GLA_DV)
    sg0 = sg0.reshape(B, GLA_HEADS * GLA_DK, GLA_DV)
    sr0 = sr0.reshape(B, RET_HEADS * RET_DK, RET_DV)
    oa, ob, sg, sr = pl.pallas_call(
        functools.partial(_scan_body, C=C, valid=valid),
        grid=(B // G, n),
        in_specs=[pl.BlockSpec((G, C, GLA_COLS), tok), pl.BlockSpec((G, C, RET_COLS), tok),
                  pl.BlockSpec((RANK_PAD, GLA_QK), fixed), pl.BlockSpec((1, GLA_QK), fixed),
                  pl.BlockSpec((GLA_HEADS, GLA_DV), fixed), pl.BlockSpec((RET_HEADS, RET_DV), fixed),
                  pl.BlockSpec((C, RET_QK), pos), pl.BlockSpec((C, RET_QK), pos),
                  pl.BlockSpec((1, RET_QK), fixed),
                  pl.BlockSpec(st_block, st), pl.BlockSpec(st_block, st)],
        out_specs=[pl.BlockSpec((G, C, GLA_V), tok), pl.BlockSpec((G, C, RET_V), tok),
                   pl.BlockSpec(st_block, st), pl.BlockSpec(st_block, st)],
        out_shape=[jax.ShapeDtypeStruct((B, S, GLA_V), f32), jax.ShapeDtypeStruct((B, S, RET_V), f32),
                   jax.ShapeDtypeStruct((B,) + st_block[1:], f32), jax.ShapeDtypeStruct((B,) + st_block[1:], f32)],
        scratch_shapes=[pltpu.VMEM(st_block, f32), pltpu.VMEM(st_block, f32)],
        compiler_params=_cparams(("parallel", "arbitrary")),
        name="scan",
    )(gla, ret, wa2, ba, gn, rn, cos, sin, lg, sg0, sr0)
    state_shape = (B, GLA_HEADS, GLA_DK, GLA_DV)
    return oa.reshape(B * S, GLA_V), ob.reshape(B * S, RET_V), sg.reshape(state_shape), sr.reshape(state_shape)


def _merge_body(x_ref, oa_ref, ob_ref, z_ref, wpa_ref, wpb_ref, wo_ref, gx_ref, wxq_ref, h_ref, q_ref):
    z = z_ref[...]
    merged = (jax.nn.sigmoid(z[:, :D_MODEL]) * _mm(oa_ref[...], wpa_ref[...])
              + jax.nn.sigmoid(z[:, D_MODEL:]) * _mm(ob_ref[...], wpb_ref[...]))
    h = x_ref[...] + _mm(merged, wo_ref[...])
    h_ref[...] = h
    q_ref[...] = _mm(_rms(h, gx_ref[...]), wxq_ref[...])


def _merge(x, oa, ob, z, wpa, wpb, wo, gx, wxq, tm=256):
    T = x.shape[0]
    row = lambda i: (i, 0)
    fixed = lambda i: (0, 0)
    return pl.pallas_call(
        _merge_body,
        grid=(T // tm,),
        in_specs=[pl.BlockSpec((tm, D_MODEL), row), pl.BlockSpec((tm, GLA_V), row), pl.BlockSpec((tm, RET_V), row),
                  pl.BlockSpec((tm, Z_COLS), row),
                  pl.BlockSpec((GLA_V, D_MODEL), fixed), pl.BlockSpec((RET_V, D_MODEL), fixed),
                  pl.BlockSpec((D_MODEL, D_MODEL), fixed), pl.BlockSpec((1, D_MODEL), fixed),
                  pl.BlockSpec((D_MODEL, D_MODEL), fixed)],
        out_specs=[pl.BlockSpec((tm, D_MODEL), row), pl.BlockSpec((tm, D_MODEL), row)],
        out_shape=[jax.ShapeDtypeStruct((T, D_MODEL), f32), jax.ShapeDtypeStruct((T, D_MODEL), f32)],
        compiler_params=_cparams(("parallel",)),
        name="merge",
    )(x, oa, ob, z, wpa, wpb, wo, gx, wxq)


def _mem_kv_body(m_ref, g_ref, wk_ref, wv_ref, k_ref, v_ref):
    mn = _rms(m_ref[...], g_ref[...]).astype(bf16)
    k_ref[...] = jnp.dot(mn, wk_ref[...], preferred_element_type=f32)
    v_ref[...] = jnp.dot(mn, wv_ref[...], preferred_element_type=f32)


def _mem_kv(mem, g, wk, wv, tm=256):
    T = mem.shape[0]
    row = lambda i: (i, 0)
    fixed = lambda i: (0, 0)
    return pl.pallas_call(
        _mem_kv_body,
        grid=(T // tm,),
        in_specs=[pl.BlockSpec((tm, D_MODEL), row), pl.BlockSpec((1, D_MODEL), fixed),
                  pl.BlockSpec((D_MODEL, D_MODEL), fixed), pl.BlockSpec((D_MODEL, D_MODEL), fixed)],
        out_specs=[pl.BlockSpec((tm, D_MODEL), row), pl.BlockSpec((tm, D_MODEL), row)],
        out_shape=[jax.ShapeDtypeStruct((T, D_MODEL), f32), jax.ShapeDtypeStruct((T, D_MODEL), f32)],
        compiler_params=_cparams(("parallel",)),
        name="mem_kv",
    )(mem, g, wk, wv)


def _xattn_body(q_ref, k_ref, v_ref, o_ref, *, seqs, tq):
    for s in range(seqs):
        rows = slice(s * tq, (s + 1) * tq)
        q = q_ref[rows, :]
        for h in range(XA_HEADS):
            hs = slice(h * XA_DH, (h + 1) * XA_DH)
            sc = _mm_nt(q[:, hs], k_ref[s, :, hs]) * XA_DH ** -0.5
            p = jnp.exp(sc - jnp.max(sc, axis=-1, keepdims=True))
            p = p / jnp.sum(p, axis=-1, keepdims=True)
            o_ref[rows, hs] = _mm(p, v_ref[s, :, hs])


def _xattn(q, mk, mv, B, S, tq, seqs):
    n = S // tq
    assert seqs == 1 or n == 1
    tok = lambda b, j: (b * n + j, 0)
    mem_spec = pl.BlockSpec((seqs, N_MEM, D_MODEL), lambda b, j: (b, 0, 0))
    return pl.pallas_call(
        functools.partial(_xattn_body, seqs=seqs, tq=tq),
        grid=(B // seqs, n),
        in_specs=[pl.BlockSpec((seqs * tq, D_MODEL), tok), mem_spec, mem_spec],
        out_specs=pl.BlockSpec((seqs * tq, D_MODEL), tok),
        out_shape=jax.ShapeDtypeStruct((B * S, D_MODEL), f32),
        compiler_params=_cparams(("parallel", "parallel")),
        name="xattn",
    )(q, mk, mv)


def _xattn_cache_body(q_ref, k_hbm, v_hbm, o_ref, kbuf, vbuf, sem, *, seqs, tq, layer):
    i = pl.program_id(0)
    n = pl.num_programs(0)

    def copies(step, slot):
        out = []
        for s in range(seqs):
            for h in range(XA_HEADS):
                hs = pl.ds(h * XA_DH, XA_DH)
                b = step * seqs + s
                out.append(pltpu.make_async_copy(k_hbm.at[layer, b, :, h, :], kbuf.at[slot, s, :, hs],
                                                 sem.at[slot, 0, s, h]))
                out.append(pltpu.make_async_copy(v_hbm.at[layer, b, :, h, :], vbuf.at[slot, s, :, hs],
                                                 sem.at[slot, 1, s, h]))
        return out

    @pl.when(i == 0)
    def _():
        for c in copies(0, 0):
            c.start()

    @pl.when(i + 1 < n)
    def _():
        for c in copies(i + 1, (i + 1) % 2):
            c.start()

    slot = i % 2
    for c in copies(i, slot):
        c.wait()
    for s in range(seqs):
        rows = slice(s * tq, (s + 1) * tq)
        q = q_ref[rows, :]
        for h in range(XA_HEADS):
            hs = slice(h * XA_DH, (h + 1) * XA_DH)
            sc = _mm_nt(q[:, hs], kbuf[slot, s, :, hs]) * XA_DH ** -0.5
            p = jnp.exp(sc - jnp.max(sc, axis=-1, keepdims=True))
            p = p / jnp.sum(p, axis=-1, keepdims=True)
            o_ref[rows, hs] = _mm(p, vbuf[slot, s, :, hs])


def _xattn_cache(q, cache_k, cache_v, layer, B, tq, seqs):
    assert B % seqs == 0
    tok = lambda b: (b, 0)
    buf = pltpu.VMEM((2, seqs, N_MEM, D_MODEL), cache_k.dtype)
    return pl.pallas_call(
        functools.partial(_xattn_cache_body, seqs=seqs, tq=tq, layer=layer),
        grid=(B // seqs,),
        in_specs=[pl.BlockSpec((seqs * tq, D_MODEL), tok), pl.BlockSpec(memory_space=pl.ANY),
                  pl.BlockSpec(memory_space=pl.ANY)],
        out_specs=pl.BlockSpec((seqs * tq, D_MODEL), tok),
        out_shape=jax.ShapeDtypeStruct((B * tq, D_MODEL), f32),
        scratch_shapes=[buf, buf, pltpu.SemaphoreType.DMA((2, 2, seqs, XA_HEADS))],
        compiler_params=_cparams(("arbitrary",)),
        name="xattn_cache",
    )(q, cache_k, cache_v)


def _merge_exchange_network(n):
    pairs = []
    p = 1
    while p < n:
        k = p
        while k >= 1:
            for j in range(k % p, n - k, 2 * k):
                for i in range(min(k, n - j - k)):
                    if (i + j) // (2 * p) == (i + j + k) // (2 * p):
                        pairs.append((i + j, i + j + k))
            k //= 2
        p *= 2
    return pairs


def _compare_exchange(x, i, j):
    a, b = x[i], x[j]
    if b is None:
        return
    if a is None:
        x[i], x[j] = b, None
        return
    x[i], x[j] = jnp.maximum(a, b), jnp.minimum(a, b)


def _top16(tiles):
    n = PEER_TOPK
    x = list(tiles) + [None] * (n - len(tiles))
    for i, j in _merge_exchange_network(n):
        _compare_exchange(x, i, j)
    for shift in (4, 2, 1):
        merged = []
        for i in range(n):
            a, b = x[i], x[n - 1 - i]
            b = None if b is None else pltpu.roll(b, shift, 0)
            merged.append(b if a is None else (a if b is None else jnp.maximum(a, b)))
        x = merged
        d = n // 2
        while d >= 1:
            for i in range(n):
                if (i & d) == 0:
                    _compare_exchange(x, i, i + d)
            d //= 2
    return x


def _sublane_sum(x):
    for shift in (4, 2, 1):
        x = x + pltpu.roll(x, shift, 0)
    return x


def _rows_to_sublanes(v, sub):
    out = v[SUBLANES - 1]
    for r in range(SUBLANES - 2, -1, -1):
        out = jnp.where(sub == r, v[r], out)
    return out


def _peer_front_body(h_ref, o_ref, wxo_ref, gf_ref, wq_ref, sk_ref,
                     h2_ref, hn_ref, cn1_ref, e1_ref, rk2_ref, e2_ref, hn_scr, *, tm):
    hd = pl.program_id(1)

    @pl.when(hd == 0)
    def _():
        h2 = h_ref[...] + _mm(o_ref[...], wxo_ref[...])
        h2_ref[...] = h2
        hn = _rms(h2, gf_ref[...]).astype(bf16)
        hn_scr[...] = hn
        hn_ref[...] = hn

    pq = jnp.dot(hn_scr[...], wq_ref[0], preferred_element_type=f32)
    half = PEER_DQ // 2
    s1 = _mm_nt(sk_ref[0, 0], pq[:, :half])
    s2 = _mm_nt(sk_ref[0, 1], pq[:, half:])
    sub = lax.broadcasted_iota(jnp.int32, (SUBLANES, LANES), 0)
    for t in range(tm // LANES):
        ls = slice(t * LANES, (t + 1) * LANES)
        a1 = s1[:, ls]
        a2 = s2[:, ls]
        v1 = _top16([a1[SUBLANES * i:SUBLANES * (i + 1)] for i in range(PEER_NKEYS // SUBLANES)])
        v2 = _top16([a2[SUBLANES * i:SUBLANES * (i + 1)] for i in range(PEER_NKEYS // SUBLANES)])
        v1_hi = _rows_to_sublanes(v1[SUBLANES:], sub)
        v2_lo = _rows_to_sublanes(v2[:SUBLANES], sub)
        v2_hi = _rows_to_sublanes(v2[SUBLANES:], sub)
        cand = [v1[0] + v2_lo, v1[0] + v2_hi]
        cand += [v1[r] + v2_lo for r in range(1, SUBLANES)]
        cand += [v1_hi + v2[0]]
        top = _top16(cand)
        z = _sublane_sum(jnp.exp(_rows_to_sublanes(top[:SUBLANES], sub) - top[0])
                         + jnp.exp(_rows_to_sublanes(top[SUBLANES:], sub) - top[0]))
        tau = top[PEER_TOPK - 1]
        picked = lambda c: jnp.where(c >= tau, 1.0, 0.0)
        count = [_sublane_sum(picked(cand[0]) + picked(cand[1]))]
        count += [_sublane_sum(picked(cand[r + 1])) for r in range(1, SUBLANES)]
        count += [picked(v1[r] + v2[0]) for r in range(SUBLANES, PEER_TOPK)]
        count1 = []
        rank2 = []
        for i in range(PEER_NKEYS // SUBLANES):
            k1 = a1[SUBLANES * i:SUBLANES * (i + 1)]
            k2 = a2[SUBLANES * i:SUBLANES * (i + 1)]
            c1 = jnp.zeros_like(k1)
            r2 = jnp.full_like(k2, float(PEER_TOPK))
            for r in range(PEER_TOPK - 1, -1, -1):
                c1 = jnp.where(k1 == v1[r], count[r], c1)
                r2 = jnp.where(k2 == v2[r], float(r), r2)
            count1.append(c1)
            rank2.append(r2)
        cn1_ref[0, :, ls] = jnp.concatenate(count1, axis=0)
        e1_ref[0, :, ls] = jnp.exp(a1 - v1[0][0:1]) / z[0:1]
        rk2_ref[0, :, ls] = jnp.concatenate(rank2, axis=0).astype(bf16)
        e2_ref[0, :, ls] = jnp.exp(a2 - v2[0][0:1]).astype(bf16)


def _peer_front(h, o, wxo, gf, wq, sk, tm=512):
    T = h.shape[0]
    assert T % tm == 0
    row = lambda i, hd: (i, 0)
    fixed = lambda i, hd: (0, 0)
    head_t = lambda i, hd: (hd, 0, i)
    words = jax.ShapeDtypeStruct((PEER_HEADS, PEER_NKEYS, T), f32)
    halfs = jax.ShapeDtypeStruct((PEER_HEADS, PEER_NKEYS, T), bf16)
    sc_spec = pl.BlockSpec((1, PEER_NKEYS, tm), head_t)
    return pl.pallas_call(
        functools.partial(_peer_front_body, tm=tm),
        grid=(T // tm, PEER_HEADS),
        in_specs=[pl.BlockSpec((tm, D_MODEL), row), pl.BlockSpec((tm, D_MODEL), row),
                  pl.BlockSpec((D_MODEL, D_MODEL), fixed), pl.BlockSpec((1, D_MODEL), fixed),
                  pl.BlockSpec((1, D_MODEL, PEER_DQ), lambda i, hd: (hd, 0, 0)),
                  pl.BlockSpec((1, 2, PEER_NKEYS, PEER_DQ // 2), lambda i, hd: (hd, 0, 0, 0))],
        out_specs=[pl.BlockSpec((tm, D_MODEL), row), pl.BlockSpec((tm, D_MODEL), row),
                   sc_spec, sc_spec, sc_spec, sc_spec],
        out_shape=[jax.ShapeDtypeStruct((T, D_MODEL), f32), jax.ShapeDtypeStruct((T, D_MODEL), bf16),
                   words, words, halfs, halfs],
        scratch_shapes=[pltpu.VMEM((tm, D_MODEL), bf16)],
        compiler_params=_cparams(("parallel", "arbitrary")),
        name="peer_front",
    )(h, o, wxo, gf, wq, sk)


PEER_ROWS_PER_BLOCK = SUBLANES
PEER_BLOCK = PEER_ROWS_PER_BLOCK * PEER_NKEYS
GATE_ROWS = 8 * SUBLANES
PEER_DENSE_TOKENS = 1024


def _peer_dense_body(hn_ref, u_ref, vt_ref, cn1_ref, e1_ref, rk2_ref, e2_ref, h2_ref, gfin_ref,
                     y_ref, at_scr, w_scr, acc_scr, *, tm):
    j = pl.program_id(1)

    @pl.when(j == 0)
    def _():
        acc_scr[...] = jnp.zeros_like(acc_scr)

    at_scr[...] = lax.dot_general(u_ref[...], hn_ref[...], (((1,), (1,)), ((), ())), preferred_element_type=f32)
    chunks = PEER_NKEYS // GATE_ROWS

    def row_tile(ref, hd, r, ls):
        return jnp.broadcast_to(ref[hd, r:r + 1, ls].astype(bf16), (GATE_ROWS, LANES))

    def gate_tile(i, carry):
        ls = pl.ds(pl.multiple_of((i // chunks) * LANES, LANES), LANES)
        c0 = pl.multiple_of((i % chunks) * GATE_ROWS, GATE_ROWS)
        gates = [None] * PEER_ROWS_PER_BLOCK
        for hd in range(PEER_HEADS):
            rk2 = rk2_ref[hd, pl.ds(c0, GATE_ROWS), ls]
            e2 = e2_ref[hd, pl.ds(c0, GATE_ROWS), ls]
            for r in range(PEER_ROWS_PER_BLOCK):
                picked = jnp.clip(row_tile(cn1_ref, hd, r, ls) - rk2, 0.0, 1.0)
                term = (row_tile(e1_ref, hd, r, ls) * e2) * picked
                gates[r] = term if gates[r] is None else gates[r] + term
        for r in range(PEER_ROWS_PER_BLOCK):
            rs = pl.ds(pl.multiple_of(r * PEER_NKEYS + c0, GATE_ROWS), GATE_ROWS)
            a = at_scr[rs, ls]
            act = 0.5 * a * (1.0 + lax.erf(a * math.sqrt(0.5)))
            w_scr[rs, ls] = act.astype(bf16) * gates[r]
        return carry

    lax.fori_loop(0, (tm // LANES) * chunks, gate_tile, 0)
    acc_scr[...] += jnp.dot(vt_ref[...], w_scr[...], preferred_element_type=f32)

    @pl.when(j == pl.num_programs(1) - 1)
    def _():
        y_ref[...] = _rms(h2_ref[...] + acc_scr[...].T, gfin_ref[...])


def _peer_dense(hn, u, vt, cn1, e1, rk2, e2, h2, gfin):
    T = hn.shape[0]
    nb = PEER_BLOCK
    tm = min(T, PEER_DENSE_TOKENS)
    assert T % tm == 0
    row = lambda i, j: (i, 0)
    once = pl.Buffered(1)
    sc1 = pl.BlockSpec((PEER_HEADS, PEER_ROWS_PER_BLOCK, tm), lambda i, j: (0, j, i))
    sc2 = pl.BlockSpec((PEER_HEADS, PEER_NKEYS, tm), lambda i, j: (0, 0, i), pipeline_mode=once)
    return pl.pallas_call(
        functools.partial(_peer_dense_body, tm=tm),
        grid=(T // tm, PEER_N // nb),
        in_specs=[pl.BlockSpec((tm, D_MODEL), row),
                  pl.BlockSpec((nb, D_MODEL), lambda i, j: (j, 0)),
                  pl.BlockSpec((D_MODEL, nb), lambda i, j: (0, j)),
                  sc1, sc1, sc2, sc2,
                  pl.BlockSpec((tm, D_MODEL), row, pipeline_mode=once),
                  pl.BlockSpec((1, D_MODEL), lambda i, j: (0, 0))],
        out_specs=pl.BlockSpec((tm, D_MODEL), row),
        out_shape=jax.ShapeDtypeStruct((T, D_MODEL), f32),
        scratch_shapes=[pltpu.VMEM((nb, tm), f32), pltpu.VMEM((nb, tm), bf16), pltpu.VMEM((D_MODEL, tm), f32)],
        compiler_params=_cparams(("parallel", "arbitrary")),
        name="peer_dense",
    )(hn, u, vt, cn1, e1, rk2, e2, h2, gfin)


def _rotary_tables(pos):
    half = RET_DK // 2
    inv = ROPE_BASE ** (-jnp.arange(half, dtype=f32) / half)
    ang = pos.astype(f32)[:, None] * inv[None, :]
    cos = jnp.cos(ang)
    sin = jnp.sin(ang)
    cos = jnp.tile(jnp.concatenate([cos, cos], axis=-1), (1, RET_HEADS))
    sin = jnp.tile(jnp.concatenate([-sin, sin], axis=-1), (1, RET_HEADS))
    return cos, sin


def _layer_weights(l, norm_mix, w_in, w_a2, b_a, gla_head_norm, ret_head_norm, w_pa, w_pb, w_o, norm_xattn,
                   norm_mem, w_xq, w_xk, w_xv, w_xo, norm_ffn, peer_wq, peer_subkeys, peer_u, peer_v):
    offs = [0]
    for s in IN_SIZES:
        offs.append(offs[-1] + s)
    cols = [w_in[l][:, offs[i]:offs[i + 1]] for i in range(len(IN_SIZES))]
    gq, gk, gv, glr, gr, rq, rk, rv, rg, za, zb = cols
    glr = jnp.pad(glr, ((0, 0), (0, RANK_PAD - GLA_RANK)))
    half = RET_DK // 2
    j = jnp.arange(RET_QK)
    partner = jnp.where((j % RET_DK) < half, j + half, j - half)
    w = {}
    w["gla"] = jnp.concatenate([gq, gk, gv, glr, gr], axis=1).astype(bf16)
    w["ret"] = jnp.concatenate([rq, rk, rv, rg, rq[:, partner], rk[:, partner]], axis=1).astype(bf16)
    w["z"] = jnp.concatenate([za, zb], axis=1).astype(bf16)
    w["a2"] = jnp.pad(w_a2[l], ((0, RANK_PAD - GLA_RANK), (0, 0))).astype(bf16)
    w["ba"] = b_a[l].reshape(1, GLA_QK)
    w["gn"] = gla_head_norm[l]
    w["rn"] = ret_head_norm[l]
    log_gamma = jnp.log1p(-(2.0 ** (-5.0 - jnp.arange(RET_HEADS, dtype=f32))))
    w["lg"] = jnp.repeat(log_gamma, RET_DK).reshape(1, RET_QK)
    w["norm_mix"] = norm_mix[l].reshape(1, D_MODEL)
    w["pa"] = w_pa[l].astype(bf16)
    w["pb"] = w_pb[l].astype(bf16)
    w["o"] = w_o[l].astype(bf16)
    w["norm_xattn"] = norm_xattn[l].reshape(1, D_MODEL)
    w["norm_mem"] = norm_mem[l].reshape(1, D_MODEL)
    w["xq"] = w_xq[l].astype(bf16)
    w["xk"] = w_xk[l].astype(bf16)
    w["xv"] = w_xv[l].astype(bf16)
    w["xo"] = w_xo[l].astype(bf16)
    w["norm_ffn"] = norm_ffn[l].reshape(1, D_MODEL)
    w["wq"] = peer_wq[l].reshape(D_MODEL, PEER_HEADS, PEER_DQ).transpose(1, 0, 2).astype(bf16)
    w["sk"] = peer_subkeys[l].astype(bf16)
    w["u"] = peer_u[l].astype(bf16)
    w["vt"] = peer_v[l].T.astype(bf16)
    return w


def _layer(x, B, S, keep, C, valid, cos, sin, xattn, sg0, sr0, w, gfin):
    gla, ret, z = _in_proj(x, w["norm_mix"], w["gla"], w["ret"], w["z"])
    oa, ob, sg, sr = _scan(gla, ret, w["a2"], w["ba"], w["gn"], w["rn"], cos, sin, w["lg"], sg0, sr0, B, S, C, valid)
    h, q = _merge(x, oa, ob, z, w["pa"], w["pb"], w["o"], w["norm_xattn"], w["xq"])
    o = xattn(q)
    if keep < S:
        h = h.reshape(B, S, D_MODEL)[:, :keep].reshape(B * keep, D_MODEL)
        o = o.reshape(B, S, D_MODEL)[:, :keep].reshape(B * keep, D_MODEL)
    h2, hn, cn1, e1, rk2, e2 = _peer_front(h, o, w["xo"], w["norm_ffn"], w["wq"], w["sk"])
    y = _peer_dense(hn, w["u"], w["vt"], cn1, e1, rk2, e2, h2, gfin)
    return y, sg, sr


def kernel(x_prompt, x_sample, mem_prompt, state_gla, state_ret, cache_mem_k, cache_mem_v, norm_mix, w_in, w_a2,
           b_a, gla_head_norm, ret_head_norm, w_pa, w_pb, w_o, norm_xattn, norm_mem, w_xq, w_xk, w_xv, w_xo,
           norm_ffn, peer_wq, peer_subkeys, peer_u, peer_v, norm_final):
    depth = w_in.shape[0]
    assert depth == 1, "the final norm is fused into the layer's last kernel"
    Bp, Sp, _ = x_prompt.shape
    Bs, Ss, _ = x_sample.shape
    l = 0
    w = _layer_weights(l, norm_mix, w_in, w_a2, b_a, gla_head_norm, ret_head_norm, w_pa, w_pb, w_o, norm_xattn,
                       norm_mem, w_xq, w_xk, w_xv, w_xo, norm_ffn, peer_wq, peer_subkeys, peer_u, peer_v)
    gfin = norm_final.reshape(1, D_MODEL)

    cos_p, sin_p = _rotary_tables(jnp.arange(Sp, dtype=jnp.int32))
    mk, mv = _mem_kv(mem_prompt.reshape(Bp * N_MEM, D_MODEL), w["norm_mem"], w["xk"], w["xv"])
    zeros = jnp.zeros((Bp, GLA_HEADS, GLA_DK, GLA_DV), f32)
    xattn_p = functools.partial(_xattn, mk=mk.reshape(Bp, N_MEM, D_MODEL), mv=mv.reshape(Bp, N_MEM, D_MODEL),
                                B=Bp, S=Sp, tq=XATTN_ROWS, seqs=1)
    yp, sgp, srp = _layer(x_prompt.reshape(Bp * Sp, D_MODEL), Bp, Sp, Sp, SCAN_CHUNK, SCAN_CHUNK, cos_p, sin_p,
                          xattn_p, zeros, zeros, w, gfin)

    pad = SAMPLE_PAD - Ss
    xs = jnp.pad(x_sample, ((0, 0), (0, pad), (0, 0))).reshape(Bs * SAMPLE_PAD, D_MODEL)
    cos_s, sin_s = _rotary_tables(PAST_LEN + jnp.arange(SAMPLE_PAD, dtype=jnp.int32))
    xattn_s = functools.partial(_xattn_cache, cache_k=cache_mem_k, cache_v=cache_mem_v, layer=l, B=Bs,
                                tq=SAMPLE_PAD, seqs=XATTN_SEQS)
    ys, sgs, srs = _layer(xs, Bs, SAMPLE_PAD, Ss, SAMPLE_PAD, Ss, cos_s, sin_s,
                          xattn_s, state_gla[l], state_ret[l], w, gfin)
    ys = ys.reshape(Bs, Ss, D_MODEL)

    kv_shape = (1, Bp, N_MEM, XA_HEADS, XA_DH)
    return (yp.reshape(Bp, Sp, D_MODEL), ys, sgp[None], srp[None], mk.reshape(kv_shape), mv.reshape(kv_shape),
            sgs[None], srs[None])
```

```python
import functools
import math

import jax
import jax.numpy as jnp
from jax import lax
from jax.experimental import pallas as pl
from jax.experimental.pallas import tpu as pltpu

f32 = jnp.float32
bf16 = jnp.bfloat16

D_MODEL = 1024
PAST_LEN = 16384
GLA_HEADS, GLA_DK, GLA_DV, GLA_RANK, GLA_TEMP = 4, 64, 128, 16, 16.0
RET_HEADS, RET_DK, RET_DV = 4, 64, 128
ROPE_BASE = 10000.0
N_MEM = 256
XA_HEADS = 4
XA_DH = D_MODEL // XA_HEADS
PEER_HEADS, PEER_NKEYS, PEER_DQ, PEER_TOPK = 8, 128, 256, 16
PEER_N = PEER_NKEYS * PEER_NKEYS
EPS = 1e-6

GLA_QK = GLA_HEADS * GLA_DK
GLA_V = GLA_HEADS * GLA_DV
RET_QK = RET_HEADS * RET_DK
RET_V = RET_HEADS * RET_DV
IN_SIZES = (GLA_QK, GLA_QK, GLA_V, GLA_RANK, GLA_V, RET_QK, RET_QK, RET_V, RET_V, D_MODEL, D_MODEL)

LANES = 128
SUBLANES = 8
RANK_PAD = LANES
GLA_COLS = 2 * GLA_QK + GLA_V + RANK_PAD + GLA_V
RET_COLS = 2 * RET_QK + 2 * RET_V + 2 * RET_QK
Z_COLS = 2 * D_MODEL
SCAN_CHUNK = 64
SCAN_SEQS = 4
SAMPLE_PAD = SUBLANES
XATTN_ROWS = 512
XATTN_SEQS = 8
VMEM_LIMIT = 52 * 1024 * 1024


def _cparams(sem):
    return pltpu.CompilerParams(dimension_semantics=sem, vmem_limit_bytes=VMEM_LIMIT)


def _rms(x, g):
    return x * lax.rsqrt(jnp.mean(x * x, axis=-1, keepdims=True) + EPS) * g


def _mm(a, b):
    return jnp.dot(a.astype(bf16), b.astype(bf16), preferred_element_type=f32)


def _mm_nt(a, b):
    return lax.dot_general(a.astype(bf16), b.astype(bf16), (((1,), (1,)), ((), ())), preferred_element_type=f32)


def _mm_tn(a, b):
    return lax.dot_general(a.astype(bf16), b.astype(bf16), (((0,), (0,)), ((), ())), preferred_element_type=f32)


def _in_proj_body(x_ref, g_ref, wg_ref, wr_ref, wz_ref, og_ref, or_ref, oz_ref):
    xn = _rms(x_ref[...], g_ref[...]).astype(bf16)
    og_ref[...] = jnp.dot(xn, wg_ref[...], preferred_element_type=f32)
    or_ref[...] = jnp.dot(xn, wr_ref[...], preferred_element_type=f32)
    oz_ref[...] = jnp.dot(xn, wz_ref[...], preferred_element_type=f32)


def _in_proj(x, g, wg, wr, wz, tm=256):
    T = x.shape[0]
    row = lambda i: (i, 0)
    fixed = lambda i: (0, 0)
    return pl.pallas_call(
        _in_proj_body,
        grid=(T // tm,),
        in_specs=[pl.BlockSpec((tm, D_MODEL), row), pl.BlockSpec((1, D_MODEL), fixed),
                  pl.BlockSpec((D_MODEL, GLA_COLS), fixed), pl.BlockSpec((D_MODEL, RET_COLS), fixed),
                  pl.BlockSpec((D_MODEL, Z_COLS), fixed)],
        out_specs=[pl.BlockSpec((tm, GLA_COLS), row), pl.BlockSpec((tm, RET_COLS), row),
                   pl.BlockSpec((tm, Z_COLS), row)],
        out_shape=[jax.ShapeDtypeStruct((T, GLA_COLS), f32), jax.ShapeDtypeStruct((T, RET_COLS), f32),
                   jax.ShapeDtypeStruct((T, Z_COLS), f32)],
        compiler_params=_cparams(("parallel",)),
        name="in_proj",
    )(x, g, wg, wr, wz)


def _chunk_heads(q, k, v, b, gate, hnorm_ref, state_ref, o_ref, C):
    mid = C // 2 - 1
    b_mid = b[mid:mid + 1, :]
    b_last = b[C - 1:C, :]
    q_in = q * jnp.exp(b)
    q_e = q * jnp.exp(b - b_mid)
    k_e = k * jnp.exp(b_mid - b)
    k_d = k * jnp.exp(b_last - b)
    decay_col = jnp.exp(jnp.broadcast_to(b_last, (SUBLANES, b.shape[1])).T[:, 0:1])
    lane_head = lax.broadcasted_iota(jnp.int32, (C, 4 * 64), 1) // 64
    stack = lambda x: jnp.concatenate([jnp.where(lane_head == h, x, 0.0) for h in range(4)], axis=0)
    rows = lax.broadcasted_iota(jnp.int32, (4 * C, C), 0) % C
    cols = lax.broadcasted_iota(jnp.int32, (4 * C, C), 1)
    state = state_ref[...]
    att = jnp.where(rows >= cols, _mm_nt(stack(q_e), k_e), 0.0)
    intra = _mm(att, v)
    inter = _mm(stack(q_in), state)
    update = _mm_tn(k_d, v)
    for h in range(4):
        ks = slice(h * 64, (h + 1) * 64)
        vs = slice(h * 128, (h + 1) * 128)
        cs = slice(h * C, (h + 1) * C)
        o_h = inter[cs, :] + intra[cs, vs]
        state_ref[ks, :] = decay_col[ks, :] * state[ks, :] + update[ks, vs]
        g_h = gate[:, vs]
        o_ref[:, vs] = _rms(o_h, hnorm_ref[h:h + 1, :]) * (g_h * jax.nn.sigmoid(g_h))


def _scan_body(gla_ref, ret_ref, wa2_ref, ba_ref, gn_ref, rn_ref, cos_ref, sin_ref, lg_ref, sg0_ref, sr0_ref,
               oa_ref, ob_ref, sg_ref, sr_ref, sg_scr, sr_scr, *, C, valid):
    c = pl.program_id(1)

    @pl.when(c == 0)
    def _():
        sg_scr[...] = sg0_ref[...]
        sr_scr[...] = sr0_ref[...]

    row = lax.broadcasted_iota(jnp.int32, (C, 1), 0)
    tri = (lax.broadcasted_iota(jnp.int32, (C, C), 0) >= lax.broadcasted_iota(jnp.int32, (C, C), 1)).astype(bf16)
    cos = cos_ref[...]
    sin = sin_ref[...]

    for s in range(SCAN_SEQS):
        g = gla_ref[s]
        q = g[:, 0:GLA_QK] * GLA_DK ** -0.5
        k = g[:, GLA_QK:2 * GLA_QK]
        v = g[:, 2 * GLA_QK:2 * GLA_QK + GLA_V]
        lowrank = g[:, 2 * GLA_QK + GLA_V:2 * GLA_QK + GLA_V + RANK_PAD]
        gate = g[:, 2 * GLA_QK + GLA_V + RANK_PAD:]
        log_a = jax.nn.log_sigmoid(_mm(lowrank, wa2_ref[...]) + ba_ref[...]) / GLA_TEMP
        if valid < C:
            log_a = jnp.where(row < valid, log_a, 0.0)
        hi = log_a.astype(bf16)
        rest = log_a - hi.astype(f32)
        mid = rest.astype(bf16)
        lo = (rest - mid.astype(f32)).astype(bf16)
        parts = jnp.dot(tri, jnp.concatenate([hi, mid, lo], axis=1), preferred_element_type=f32)
        b = parts[:, 0:GLA_QK] + parts[:, GLA_QK:2 * GLA_QK] + parts[:, 2 * GLA_QK:]
        _chunk_heads(q, k, v, b, gate, gn_ref, sg_scr.at[s], oa_ref.at[s], C)

        r = ret_ref[s]
        q = r[:, 0:RET_QK] * cos + r[:, 2 * RET_QK + 2 * RET_V:3 * RET_QK + 2 * RET_V] * sin
        k = (r[:, RET_QK:2 * RET_QK] * cos + r[:, 3 * RET_QK + 2 * RET_V:] * sin) * RET_DK ** -0.5
        v = r[:, 2 * RET_QK:2 * RET_QK + RET_V]
        gate = r[:, 2 * RET_QK + RET_V:2 * RET_QK + 2 * RET_V]
        steps = jnp.minimum(row + 1, valid).astype(f32)
        b = steps * lg_ref[...]
        _chunk_heads(q, k, v, b, gate, rn_ref, sr_scr.at[s], ob_ref.at[s], C)

    @pl.when(c == pl.num_programs(1) - 1)
    def _():
        sg_ref[...] = sg_scr[...]
        sr_ref[...] = sr_scr[...]


def _scan(gla, ret, wa2, ba, gn, rn, cos, sin, lg, sg0, sr0, B, S, C, valid):
    n = S // C
    G = SCAN_SEQS
    assert B % G == 0
    gla = gla.reshape(B, S, GLA_COLS)
    ret = ret.reshape(B, S, RET_COLS)
    tok = lambda b, c: (b, c, 0)
    fixed = lambda b, c: (0, 0)
    pos = lambda b, c: (c, 0)
    st = lambda b, c: (b, 0, 0)
    st_block = (G, GLA_HEADS * GLA_DK, GLA_DV)
    sg0 = sg0.reshape(B, GLA_HEADS * GLA_DK, GLA_DV)
    sr0 = sr0.reshape(B, RET_HEADS * RET_DK, RET_DV)
    oa, ob, sg, sr = pl.pallas_call(
        functools.partial(_scan_body, C=C, valid=valid),
        grid=(B // G, n),
        in_specs=[pl.BlockSpec((G, C, GLA_COLS), tok), pl.BlockSpec((G, C, RET_COLS), tok),
                  pl.BlockSpec((RANK_PAD, GLA_QK), fixed), pl.BlockSpec((1, GLA_QK), fixed),
                  pl.BlockSpec((GLA_HEADS, GLA_DV), fixed), pl.BlockSpec((RET_HEADS, RET_DV), fixed),
                  pl.BlockSpec((C, RET_QK), pos), pl.BlockSpec((C, RET_QK), pos),
                  pl.BlockSpec((1, RET_QK), fixed),
                  pl.BlockSpec(st_block, st), pl.BlockSpec(st_block, st)],
        out_specs=[pl.BlockSpec((G, C, GLA_V), tok), pl.BlockSpec((G, C, RET_V), tok),
                   pl.BlockSpec(st_block, st), pl.BlockSpec(st_block, st)],
        out_shape=[jax.ShapeDtypeStruct((B, S, GLA_V), f32), jax.ShapeDtypeStruct((B, S, RET_V), f32),
                   jax.ShapeDtypeStruct((B,) + st_block[1:], f32), jax.ShapeDtypeStruct((B,) + st_block[1:], f32)],
        scratch_shapes=[pltpu.VMEM(st_block, f32), pltpu.VMEM(st_block, f32)],
        compiler_params=_cparams(("parallel", "arbitrary")),
        name="scan",
    )(gla, ret, wa2, ba, gn, rn, cos, sin, lg, sg0, sr0)
    state_shape = (B, GLA_HEADS, GLA_DK, GLA_DV)
    return oa.reshape(B * S, GLA_V), ob.reshape(B * S, RET_V), sg.reshape(state_shape), sr.reshape(state_shape)


def _merge_body(x_ref, oa_ref, ob_ref, z_ref, wpa_ref, wpb_ref, wo_ref, gx_ref, wxq_ref, h_ref, q_ref):
    z = z_ref[...]
    merged = (jax.nn.sigmoid(z[:, :D_MODEL]) * _mm(oa_ref[...], wpa_ref[...])
              + jax.nn.sigmoid(z[:, D_MODEL:]) * _mm(ob_ref[...], wpb_ref[...]))
    h = x_ref[...] + _mm(merged, wo_ref[...])
    h_ref[...] = h
    q_ref[...] = _mm(_rms(h, gx_ref[...]), wxq_ref[...])


def _merge(x, oa, ob, z, wpa, wpb, wo, gx, wxq, tm=512):
    T = x.shape[0]
    row = lambda i: (i, 0)
    fixed = lambda i: (0, 0)
    return pl.pallas_call(
        _merge_body,
        grid=(T // tm,),
        in_specs=[pl.BlockSpec((tm, D_MODEL), row), pl.BlockSpec((tm, GLA_V), row), pl.BlockSpec((tm, RET_V), row),
                  pl.BlockSpec((tm, Z_COLS), row),
                  pl.BlockSpec((GLA_V, D_MODEL), fixed), pl.BlockSpec((RET_V, D_MODEL), fixed),
                  pl.BlockSpec((D_MODEL, D_MODEL), fixed), pl.BlockSpec((1, D_MODEL), fixed),
                  pl.BlockSpec((D_MODEL, D_MODEL), fixed)],
        out_specs=[pl.BlockSpec((tm, D_MODEL), row), pl.BlockSpec((tm, D_MODEL), row)],
        out_shape=[jax.ShapeDtypeStruct((T, D_MODEL), f32), jax.ShapeDtypeStruct((T, D_MODEL), f32)],
        compiler_params=_cparams(("parallel",)),
        name="merge",
    )(x, oa, ob, z, wpa, wpb, wo, gx, wxq)


def _mem_kv_body(m_ref, g_ref, wk_ref, wv_ref, k_ref, v_ref):
    mn = _rms(m_ref[...], g_ref[...]).astype(bf16)
    k_ref[...] = jnp.dot(mn, wk_ref[...], preferred_element_type=f32)
    v_ref[...] = jnp.dot(mn, wv_ref[...], preferred_element_type=f32)


def _mem_kv(mem, g, wk, wv, tm=256):
    T = mem.shape[0]
    row = lambda i: (i, 0)
    fixed = lambda i: (0, 0)
    return pl.pallas_call(
        _mem_kv_body,
        grid=(T // tm,),
        in_specs=[pl.BlockSpec((tm, D_MODEL), row), pl.BlockSpec((1, D_MODEL), fixed),
                  pl.BlockSpec((D_MODEL, D_MODEL), fixed), pl.BlockSpec((D_MODEL, D_MODEL), fixed)],
        out_specs=[pl.BlockSpec((tm, D_MODEL), row), pl.BlockSpec((tm, D_MODEL), row)],
        out_shape=[jax.ShapeDtypeStruct((T, D_MODEL), f32), jax.ShapeDtypeStruct((T, D_MODEL), f32)],
        compiler_params=_cparams(("parallel",)),
        name="mem_kv",
    )(mem, g, wk, wv)


def _xattn_body(q_ref, k_ref, v_ref, o_ref, *, seqs, tq):
    for s in range(seqs):
        rows = slice(s * tq, (s + 1) * tq)
        q = q_ref[rows, :]
        for h in range(XA_HEADS):
            hs = slice(h * XA_DH, (h + 1) * XA_DH)
            sc = _mm_nt(q[:, hs], k_ref[s, :, hs]) * XA_DH ** -0.5
            p = jnp.exp(sc - jnp.max(sc, axis=-1, keepdims=True))
            p = p / jnp.sum(p, axis=-1, keepdims=True)
            o_ref[rows, hs] = _mm(p, v_ref[s, :, hs])


def _xattn(q, mk, mv, B, S, tq, seqs):
    n = S // tq
    assert seqs == 1 or n == 1
    tok = lambda b, j: (b * n + j, 0)
    mem_spec = pl.BlockSpec((seqs, N_MEM, D_MODEL), lambda b, j: (b, 0, 0))
    return pl.pallas_call(
        functools.partial(_xattn_body, seqs=seqs, tq=tq),
        grid=(B // seqs, n),
        in_specs=[pl.BlockSpec((seqs * tq, D_MODEL), tok), mem_spec, mem_spec],
        out_specs=pl.BlockSpec((seqs * tq, D_MODEL), tok),
        out_shape=jax.ShapeDtypeStruct((B * S, D_MODEL), f32),
        compiler_params=_cparams(("parallel", "parallel")),
        name="xattn",
    )(q, mk, mv)


def _xattn_cache_body(q_ref, k_hbm, v_hbm, o_ref, kbuf, vbuf, sem, *, seqs, tq, layer):
    i = pl.program_id(0)
    n = pl.num_programs(0)

    def copies(step, slot):
        out = []
        for s in range(seqs):
            for h in range(XA_HEADS):
                hs = pl.ds(h * XA_DH, XA_DH)
                b = step * seqs + s
                out.append(pltpu.make_async_copy(k_hbm.at[layer, b, :, h, :], kbuf.at[slot, s, :, hs],
                                                 sem.at[slot, 0, s, h]))
                out.append(pltpu.make_async_copy(v_hbm.at[layer, b, :, h, :], vbuf.at[slot, s, :, hs],
                                                 sem.at[slot, 1, s, h]))
        return out

    @pl.when(i == 0)
    def _():
        for c in copies(0, 0):
            c.start()

    @pl.when(i + 1 < n)
    def _():
        for c in copies(i + 1, (i + 1) % 2):
            c.start()

    slot = i % 2
    for c in copies(i, slot):
        c.wait()
    lane_head = lax.broadcasted_iota(jnp.int32, (tq, D_MODEL), 1) // XA_DH
    for s in range(seqs):
        rows = slice(s * tq, (s + 1) * tq)
        q = q_ref[rows, :]
        stacked = jnp.concatenate([jnp.where(lane_head == h, q, 0.0) for h in range(XA_HEADS)], axis=0)
        sc = _mm_nt(stacked, kbuf[slot, s]) * XA_DH ** -0.5
        p = jnp.exp(sc - jnp.max(sc, axis=-1, keepdims=True))
        p = p / jnp.sum(p, axis=-1, keepdims=True)
        o = _mm(p, vbuf[slot, s])
        for h in range(XA_HEADS):
            hs = slice(h * XA_DH, (h + 1) * XA_DH)
            o_ref[rows, hs] = o[h * tq:(h + 1) * tq, hs]


def _xattn_cache(q, cache_k, cache_v, layer, B, tq, seqs):
    assert B % seqs == 0
    tok = lambda b: (b, 0)
    buf = pltpu.VMEM((2, seqs, N_MEM, D_MODEL), cache_k.dtype)
    return pl.pallas_call(
        functools.partial(_xattn_cache_body, seqs=seqs, tq=tq, layer=layer),
        grid=(B // seqs,),
        in_specs=[pl.BlockSpec((seqs * tq, D_MODEL), tok), pl.BlockSpec(memory_space=pl.ANY),
                  pl.BlockSpec(memory_space=pl.ANY)],
        out_specs=pl.BlockSpec((seqs * tq, D_MODEL), tok),
        out_shape=jax.ShapeDtypeStruct((B * tq, D_MODEL), f32),
        scratch_shapes=[buf, buf, pltpu.SemaphoreType.DMA((2, 2, seqs, XA_HEADS))],
        compiler_params=_cparams(("arbitrary",)),
        name="xattn_cache",
    )(q, cache_k, cache_v)


def _merge_exchange_network(n):
    pairs = []
    p = 1
    while p < n:
        k = p
        while k >= 1:
            for j in range(k % p, n - k, 2 * k):
                for i in range(min(k, n - j - k)):
                    if (i + j) // (2 * p) == (i + j + k) // (2 * p):
                        pairs.append((i + j, i + j + k))
            k //= 2
        p *= 2
    return pairs


def _compare_exchange(x, i, j):
    a, b = x[i], x[j]
    if b is None:
        return
    if a is None:
        x[i], x[j] = b, None
        return
    x[i], x[j] = jnp.maximum(a, b), jnp.minimum(a, b)


def _top16(tiles):
    n = PEER_TOPK
    x = list(tiles) + [None] * (n - len(tiles))
    for i, j in _merge_exchange_network(n):
        _compare_exchange(x, i, j)
    for shift in (4, 2, 1):
        merged = []
        for i in range(n):
            a, b = x[i], x[n - 1 - i]
            b = None if b is None else pltpu.roll(b, shift, 0)
            merged.append(b if a is None else (a if b is None else jnp.maximum(a, b)))
        x = merged
        d = n // 2
        while d >= 1:
            for i in range(n):
                if (i & d) == 0:
                    _compare_exchange(x, i, i + d)
            d //= 2
    return x


def _sublane_sum(x):
    for shift in (4, 2, 1):
        x = x + pltpu.roll(x, shift, 0)
    return x


def _rows_to_sublanes(v, sub):
    out = v[SUBLANES - 1]
    for r in range(SUBLANES - 2, -1, -1):
        out = jnp.where(sub == r, v[r], out)
    return out


def _peer_front_body(h_ref, o_ref, wxo_ref, gf_ref, wq_ref, sk_ref,
                     h2_ref, hn_ref, cn1_ref, e1_ref, rk2_ref, e2_ref, hn_scr, *, tm):
    hd = pl.program_id(1)

    @pl.when(hd == 0)
    def _():
        h2 = h_ref[...] + _mm(o_ref[...], wxo_ref[...])
        h2_ref[...] = h2
        hn = _rms(h2, gf_ref[...]).astype(bf16)
        hn_scr[...] = hn
        hn_ref[...] = hn

    pq = jnp.dot(hn_scr[...], wq_ref[0], preferred_element_type=f32)
    half = PEER_DQ // 2
    s1 = _mm_nt(sk_ref[0, 0], pq[:, :half])
    s2 = _mm_nt(sk_ref[0, 1], pq[:, half:])
    sub = lax.broadcasted_iota(jnp.int32, (SUBLANES, LANES), 0)
    for t in range(tm // LANES):
        ls = slice(t * LANES, (t + 1) * LANES)
        a1 = s1[:, ls]
        a2 = s2[:, ls]
        v1 = _top16([a1[SUBLANES * i:SUBLANES * (i + 1)] for i in range(PEER_NKEYS // SUBLANES)])
        v2 = _top16([a2[SUBLANES * i:SUBLANES * (i + 1)] for i in range(PEER_NKEYS // SUBLANES)])
        v1_hi = _rows_to_sublanes(v1[SUBLANES:], sub)
        v2_lo = _rows_to_sublanes(v2[:SUBLANES], sub)
        v2_hi = _rows_to_sublanes(v2[SUBLANES:], sub)
        cand = [v1[0] + v2_lo, v1[0] + v2_hi]
        cand += [v1[r] + v2_lo for r in range(1, SUBLANES)]
        cand += [v1_hi + v2[0]]
        top = _top16(cand)
        z = _sublane_sum(jnp.exp(_rows_to_sublanes(top[:SUBLANES], sub) - top[0])
                         + jnp.exp(_rows_to_sublanes(top[SUBLANES:], sub) - top[0]))
        tau = top[PEER_TOPK - 1]
        picked = lambda c: jnp.where(c >= tau, 1.0, 0.0)
        count = [_sublane_sum(picked(cand[0]) + picked(cand[1]))]
        count += [_sublane_sum(picked(cand[r + 1])) for r in range(1, SUBLANES)]
        count += [picked(v1[r] + v2[0]) for r in range(SUBLANES, PEER_TOPK)]
        count1 = []
        rank2 = []
        for i in range(PEER_NKEYS // SUBLANES):
            k1 = a1[SUBLANES * i:SUBLANES * (i + 1)]
            k2 = a2[SUBLANES * i:SUBLANES * (i + 1)]
            c1 = jnp.zeros_like(k1)
            r2 = jnp.full_like(k2, float(PEER_TOPK))
            for r in range(PEER_TOPK - 1, -1, -1):
                c1 = jnp.where(k1 == v1[r], count[r], c1)
                r2 = jnp.where(k2 == v2[r], float(r), r2)
            count1.append(c1)
            rank2.append(r2)
        cn1_ref[0, :, ls] = jnp.concatenate(count1, axis=0)
        e1_ref[0, :, ls] = jnp.exp(a1 - v1[0][0:1]) / (2.0 * z[0:1])
        rk2_ref[0, :, ls] = jnp.concatenate(rank2, axis=0).astype(bf16)
        e2_ref[0, :, ls] = jnp.exp(a2 - v2[0][0:1]).astype(bf16)


def _peer_front(h, o, wxo, gf, wq, sk, tm=512):
    T = h.shape[0]
    assert T % tm == 0
    row = lambda i, hd: (i, 0)
    fixed = lambda i, hd: (0, 0)
    head_t = lambda i, hd: (hd, 0, i)
    words = jax.ShapeDtypeStruct((PEER_HEADS, PEER_NKEYS, T), f32)
    halfs = jax.ShapeDtypeStruct((PEER_HEADS, PEER_NKEYS, T), bf16)
    sc_spec = pl.BlockSpec((1, PEER_NKEYS, tm), head_t)
    return pl.pallas_call(
        functools.partial(_peer_front_body, tm=tm),
        grid=(T // tm, PEER_HEADS),
        in_specs=[pl.BlockSpec((tm, D_MODEL), row), pl.BlockSpec((tm, D_MODEL), row),
                  pl.BlockSpec((D_MODEL, D_MODEL), fixed), pl.BlockSpec((1, D_MODEL), fixed),
                  pl.BlockSpec((1, D_MODEL, PEER_DQ), lambda i, hd: (hd, 0, 0)),
                  pl.BlockSpec((1, 2, PEER_NKEYS, PEER_DQ // 2), lambda i, hd: (hd, 0, 0, 0))],
        out_specs=[pl.BlockSpec((tm, D_MODEL), row), pl.BlockSpec((tm, D_MODEL), row),
                   sc_spec, sc_spec, sc_spec, sc_spec],
        out_shape=[jax.ShapeDtypeStruct((T, D_MODEL), f32), jax.ShapeDtypeStruct((T, D_MODEL), bf16),
                   words, words, halfs, halfs],
        scratch_shapes=[pltpu.VMEM((tm, D_MODEL), bf16)],
        compiler_params=_cparams(("parallel", "arbitrary")),
        name="peer_front",
    )(h, o, wxo, gf, wq, sk)


PEER_ROWS_PER_BLOCK = SUBLANES
PEER_BLOCK = PEER_ROWS_PER_BLOCK * PEER_NKEYS
GATE_ROWS = 8 * SUBLANES
PEER_DENSE_TOKENS = 1024


def _peer_dense_body(hn_ref, u_ref, vt_prev_ref, vt_last_ref, cn1_ref, e1_ref, rk2_ref, e2_ref, h2_ref, gfin_ref,
                     y_ref, at_scr, w_scr, acc_scr, *, tm):
    j = pl.program_id(1)

    @pl.when(j == 0)
    def _():
        acc_scr[...] = jnp.zeros_like(acc_scr)
        w_scr[...] = jnp.zeros_like(w_scr)

    acc_scr[...] += jnp.dot(vt_prev_ref[...], w_scr[...], preferred_element_type=f32)
    at_scr[...] = lax.dot_general(u_ref[...], hn_ref[...], (((1,), (1,)), ((), ())), preferred_element_type=f32)
    chunks = PEER_NKEYS // GATE_ROWS

    def row_tile(ref, hd, r, ls):
        return jnp.broadcast_to(ref[hd, r:r + 1, ls].astype(bf16), (GATE_ROWS, LANES))

    def gate_tile(i, carry):
        ls = pl.ds(pl.multiple_of((i // chunks) * LANES, LANES), LANES)
        c0 = pl.multiple_of((i % chunks) * GATE_ROWS, GATE_ROWS)
        gates = [None] * PEER_ROWS_PER_BLOCK
        for hd in range(PEER_HEADS):
            rk2 = rk2_ref[hd, pl.ds(c0, GATE_ROWS), ls]
            e2 = e2_ref[hd, pl.ds(c0, GATE_ROWS), ls]
            for r in range(PEER_ROWS_PER_BLOCK):
                picked = jnp.clip(row_tile(cn1_ref, hd, r, ls) - rk2, 0.0, 1.0)
                term = (row_tile(e1_ref, hd, r, ls) * e2) * picked
                gates[r] = term if gates[r] is None else gates[r] + term
        for r in range(PEER_ROWS_PER_BLOCK):
            rs = pl.ds(pl.multiple_of(r * PEER_NKEYS + c0, GATE_ROWS), GATE_ROWS)
            a = at_scr[rs, ls]
            act = a * (1.0 + lax.erf(a * math.sqrt(0.5)))
            w_scr[rs, ls] = act.astype(bf16) * gates[r]
        return carry

    lax.fori_loop(0, (tm // LANES) * chunks, gate_tile, 0)

    @pl.when(j == pl.num_programs(1) - 1)
    def _():
        out_t = acc_scr[...] + jnp.dot(vt_last_ref[...], w_scr[...], preferred_element_type=f32)
        y_ref[...] = _rms(h2_ref[...] + out_t.T, gfin_ref[...])


def _peer_dense(hn, u, vt, cn1, e1, rk2, e2, h2, gfin):
    T = hn.shape[0]
    nb = PEER_BLOCK
    tm = min(T, PEER_DENSE_TOKENS)
    assert T % tm == 0
    row = lambda i, j: (i, 0)
    once = pl.Buffered(1)
    sc1 = pl.BlockSpec((PEER_HEADS, PEER_ROWS_PER_BLOCK, tm), lambda i, j: (0, j, i))
    sc2 = pl.BlockSpec((PEER_HEADS, PEER_NKEYS, tm), lambda i, j: (0, 0, i), pipeline_mode=once)
    return pl.pallas_call(
        functools.partial(_peer_dense_body, tm=tm),
        grid=(T // tm, PEER_N // nb),
        in_specs=[pl.BlockSpec((tm, D_MODEL), row),
                  pl.BlockSpec((nb, D_MODEL), lambda i, j: (j, 0)),
                  pl.BlockSpec((D_MODEL, nb), lambda i, j: (0, jnp.maximum(j - 1, 0))),
                  pl.BlockSpec((D_MODEL, nb), lambda i, j: (0, PEER_N // nb - 1), pipeline_mode=once),
                  sc1, sc1, sc2, sc2,
                  pl.BlockSpec((tm, D_MODEL), row, pipeline_mode=once),
                  pl.BlockSpec((1, D_MODEL), lambda i, j: (0, 0))],
        out_specs=pl.BlockSpec((tm, D_MODEL), row),
        out_shape=jax.ShapeDtypeStruct((T, D_MODEL), f32),
        scratch_shapes=[pltpu.VMEM((nb, tm), f32), pltpu.VMEM((nb, tm), bf16), pltpu.VMEM((D_MODEL, tm), f32)],
        compiler_params=_cparams(("parallel", "arbitrary")),
        name="peer_dense",
    )(hn, u, vt, vt, cn1, e1, rk2, e2, h2, gfin)


def _rotary_tables(pos):
    half = RET_DK // 2
    inv = ROPE_BASE ** (-jnp.arange(half, dtype=f32) / half)
    ang = pos.astype(f32)[:, None] * inv[None, :]
    cos = jnp.cos(ang)
    sin = jnp.sin(ang)
    cos = jnp.tile(jnp.concatenate([cos, cos], axis=-1), (1, RET_HEADS))
    sin = jnp.tile(jnp.concatenate([-sin, sin], axis=-1), (1, RET_HEADS))
    return cos, sin


def _layer_weights(l, norm_mix, w_in, w_a2, b_a, gla_head_norm, ret_head_norm, w_pa, w_pb, w_o, norm_xattn,
                   norm_mem, w_xq, w_xk, w_xv, w_xo, norm_ffn, peer_wq, peer_subkeys, peer_u, peer_v):
    offs = [0]
    for s in IN_SIZES:
        offs.append(offs[-1] + s)
    cols = [w_in[l][:, offs[i]:offs[i + 1]] for i in range(len(IN_SIZES))]
    gq, gk, gv, glr, gr, rq, rk, rv, rg, za, zb = cols
    glr = jnp.pad(glr, ((0, 0), (0, RANK_PAD - GLA_RANK)))
    half = RET_DK // 2
    j = jnp.arange(RET_QK)
    partner = jnp.where((j % RET_DK) < half, j + half, j - half)
    w = {}
    w["gla"] = jnp.concatenate([gq, gk, gv, glr, gr], axis=1).astype(bf16)
    w["ret"] = jnp.concatenate([rq, rk, rv, rg, rq[:, partner], rk[:, partner]], axis=1).astype(bf16)
    w["z"] = jnp.concatenate([za, zb], axis=1).astype(bf16)
    w["a2"] = jnp.pad(w_a2[l], ((0, RANK_PAD - GLA_RANK), (0, 0))).astype(bf16)
    w["ba"] = b_a[l].reshape(1, GLA_QK)
    w["gn"] = gla_head_norm[l]
    w["rn"] = ret_head_norm[l]
    log_gamma = jnp.log1p(-(2.0 ** (-5.0 - jnp.arange(RET_HEADS, dtype=f32))))
    w["lg"] = jnp.repeat(log_gamma, RET_DK).reshape(1, RET_QK)
    w["norm_mix"] = norm_mix[l].reshape(1, D_MODEL)
    w["pa"] = w_pa[l].astype(bf16)
    w["pb"] = w_pb[l].astype(bf16)
    w["o"] = w_o[l].astype(bf16)
    w["norm_xattn"] = norm_xattn[l].reshape(1, D_MODEL)
    w["norm_mem"] = norm_mem[l].reshape(1, D_MODEL)
    w["xq"] = w_xq[l].astype(bf16)
    w["xk"] = w_xk[l].astype(bf16)
    w["xv"] = w_xv[l].astype(bf16)
    w["xo"] = w_xo[l].astype(bf16)
    w["norm_ffn"] = norm_ffn[l].reshape(1, D_MODEL)
    w["wq"] = peer_wq[l].reshape(D_MODEL, PEER_HEADS, PEER_DQ).transpose(1, 0, 2).astype(bf16)
    w["sk"] = peer_subkeys[l].astype(bf16)
    w["u"] = peer_u[l].astype(bf16)
    w["vt"] = peer_v[l].T.astype(bf16)
    return w


def _layer(x, B, S, keep, C, valid, cos, sin, xattn, sg0, sr0, w, gfin):
    gla, ret, z = _in_proj(x, w["norm_mix"], w["gla"], w["ret"], w["z"])
    oa, ob, sg, sr = _scan(gla, ret, w["a2"], w["ba"], w["gn"], w["rn"], cos, sin, w["lg"], sg0, sr0, B, S, C, valid)
    h, q = _merge(x, oa, ob, z, w["pa"], w["pb"], w["o"], w["norm_xattn"], w["xq"])
    o = xattn(q)
    if keep < S:
        h = h.reshape(B, S, D_MODEL)[:, :keep].reshape(B * keep, D_MODEL)
        o = o.reshape(B, S, D_MODEL)[:, :keep].reshape(B * keep, D_MODEL)
    h2, hn, cn1, e1, rk2, e2 = _peer_front(h, o, w["xo"], w["norm_ffn"], w["wq"], w["sk"])
    y = _peer_dense(hn, w["u"], w["vt"], cn1, e1, rk2, e2, h2, gfin)
    return y, sg, sr


def kernel(x_prompt, x_sample, mem_prompt, state_gla, state_ret, cache_mem_k, cache_mem_v, norm_mix, w_in, w_a2,
           b_a, gla_head_norm, ret_head_norm, w_pa, w_pb, w_o, norm_xattn, norm_mem, w_xq, w_xk, w_xv, w_xo,
           norm_ffn, peer_wq, peer_subkeys, peer_u, peer_v, norm_final):
    depth = w_in.shape[0]
    assert depth == 1, "the final norm is fused into the layer's last kernel"
    Bp, Sp, _ = x_prompt.shape
    Bs, Ss, _ = x_sample.shape
    l = 0
    w = _layer_weights(l, norm_mix, w_in, w_a2, b_a, gla_head_norm, ret_head_norm, w_pa, w_pb, w_o, norm_xattn,
                       norm_mem, w_xq, w_xk, w_xv, w_xo, norm_ffn, peer_wq, peer_subkeys, peer_u, peer_v)
    gfin = norm_final.reshape(1, D_MODEL)

    cos_p, sin_p = _rotary_tables(jnp.arange(Sp, dtype=jnp.int32))
    mk, mv = _mem_kv(mem_prompt.reshape(Bp * N_MEM, D_MODEL), w["norm_mem"], w["xk"], w["xv"])
    zeros = jnp.zeros((Bp, GLA_HEADS, GLA_DK, GLA_DV), f32)
    xattn_p = functools.partial(_xattn, mk=mk.reshape(Bp, N_MEM, D_MODEL), mv=mv.reshape(Bp, N_MEM, D_MODEL),
                                B=Bp, S=Sp, tq=XATTN_ROWS, seqs=1)
    yp, sgp, srp = _layer(x_prompt.reshape(Bp * Sp, D_MODEL), Bp, Sp, Sp, SCAN_CHUNK, SCAN_CHUNK, cos_p, sin_p,
                          xattn_p, zeros, zeros, w, gfin)

    pad = SAMPLE_PAD - Ss
    xs = jnp.pad(x_sample, ((0, 0), (0, pad), (0, 0))).reshape(Bs * SAMPLE_PAD, D_MODEL)
    cos_s, sin_s = _rotary_tables(PAST_LEN + jnp.arange(SAMPLE_PAD, dtype=jnp.int32))
    xattn_s = functools.partial(_xattn_cache, cache_k=cache_mem_k, cache_v=cache_mem_v, layer=l, B=Bs,
                                tq=SAMPLE_PAD, seqs=XATTN_SEQS)
    ys, sgs, srs = _layer(xs, Bs, SAMPLE_PAD, Ss, SAMPLE_PAD, Ss, cos_s, sin_s,
                          xattn_s, state_gla[l], state_ret[l], w, gfin)
    ys = ys.reshape(Bs, Ss, D_MODEL)

    kv_shape = (1, Bp, N_MEM, XA_HEADS, XA_DH)
    return (yp.reshape(Bp, Sp, D_MODEL), ys, sgp[None], srp[None], mk.reshape(kv_shape), mv.reshape(kv_shape),
            sgs[None], srs[None])
```

```python
import functools
import math

import jax
import jax.numpy as jnp
from jax import lax
from jax.experimental import pallas as pl
from jax.experimental.pallas import tpu as pltpu

f32 = jnp.float32
bf16 = jnp.bfloat16

D_MODEL = 1024
PAST_LEN = 16384
GLA_HEADS, GLA_DK, GLA_DV, GLA_RANK, GLA_TEMP = 4, 64, 128, 16, 16.0
RET_HEADS, RET_DK, RET_DV = 4, 64, 128
ROPE_BASE = 10000.0
N_MEM = 256
XA_HEADS = 4
XA_DH = D_MODEL // XA_HEADS
PEER_HEADS, PEER_NKEYS, PEER_DQ, PEER_TOPK = 8, 128, 256, 16
PEER_N = PEER_NKEYS * PEER_NKEYS
EPS = 1e-6

GLA_QK = GLA_HEADS * GLA_DK
GLA_V = GLA_HEADS * GLA_DV
RET_QK = RET_HEADS * RET_DK
RET_V = RET_HEADS * RET_DV
IN_SIZES = (GLA_QK, GLA_QK, GLA_V, GLA_RANK, GLA_V, RET_QK, RET_QK, RET_V, RET_V, D_MODEL, D_MODEL)

LANES = 128
SUBLANES = 8
RANK_PAD = LANES
GLA_COLS = 2 * GLA_QK + GLA_V + RANK_PAD + GLA_V
RET_COLS = 2 * RET_QK + 2 * RET_V + 2 * RET_QK
Z_COLS = 2 * D_MODEL
SCAN_CHUNK = 64
SCAN_SEQS = 4
SAMPLE_PAD = SUBLANES
XATTN_ROWS = 512
XATTN_SEQS = 8
VMEM_LIMIT = 52 * 1024 * 1024


def _cparams(sem):
    return pltpu.CompilerParams(dimension_semantics=sem, vmem_limit_bytes=VMEM_LIMIT)


def _rms(x, g):
    return x * lax.rsqrt(jnp.mean(x * x, axis=-1, keepdims=True) + EPS) * g


def _mm(a, b):
    return jnp.dot(a.astype(bf16), b.astype(bf16), preferred_element_type=f32)


def _mm_nt(a, b):
    return lax.dot_general(a.astype(bf16), b.astype(bf16), (((1,), (1,)), ((), ())), preferred_element_type=f32)


def _mm_tn(a, b):
    return lax.dot_general(a.astype(bf16), b.astype(bf16), (((0,), (0,)), ((), ())), preferred_element_type=f32)


def _in_proj_body(x_ref, g_ref, wg_ref, wr_ref, wz_ref, og_ref, or_ref, oz_ref):
    xn = _rms(x_ref[...], g_ref[...]).astype(bf16)
    og_ref[...] = jnp.dot(xn, wg_ref[...], preferred_element_type=f32)
    or_ref[...] = jnp.dot(xn, wr_ref[...], preferred_element_type=f32)
    oz_ref[...] = jnp.dot(xn, wz_ref[...], preferred_element_type=f32)


def _in_proj(x, g, wg, wr, wz, tm=256):
    T = x.shape[0]
    row = lambda i: (i, 0)
    fixed = lambda i: (0, 0)
    return pl.pallas_call(
        _in_proj_body,
        grid=(T // tm,),
        in_specs=[pl.BlockSpec((tm, D_MODEL), row), pl.BlockSpec((1, D_MODEL), fixed),
                  pl.BlockSpec((D_MODEL, GLA_COLS), fixed), pl.BlockSpec((D_MODEL, RET_COLS), fixed),
                  pl.BlockSpec((D_MODEL, Z_COLS), fixed)],
        out_specs=[pl.BlockSpec((tm, GLA_COLS), row), pl.BlockSpec((tm, RET_COLS), row),
                   pl.BlockSpec((tm, Z_COLS), row)],
        out_shape=[jax.ShapeDtypeStruct((T, GLA_COLS), f32), jax.ShapeDtypeStruct((T, RET_COLS), f32),
                   jax.ShapeDtypeStruct((T, Z_COLS), f32)],
        compiler_params=_cparams(("parallel",)),
        name="in_proj",
    )(x, g, wg, wr, wz)


def _chunk_heads(q, k, v, b, gate, hnorm_ref, state_ref, o_ref, C):
    mid = C // 2 - 1
    b_mid = b[mid:mid + 1, :]
    b_last = b[C - 1:C, :]
    q_in = q * jnp.exp(b)
    q_e = q * jnp.exp(b - b_mid)
    k_e = k * jnp.exp(b_mid - b)
    k_d = k * jnp.exp(b_last - b)
    decay_col = jnp.exp(jnp.broadcast_to(b_last, (SUBLANES, b.shape[1])).T[:, 0:1])
    lane_head = lax.broadcasted_iota(jnp.int32, (C, 4 * 64), 1) // 64
    stack = lambda x: jnp.concatenate([jnp.where(lane_head == h, x, 0.0) for h in range(4)], axis=0)
    rows = lax.broadcasted_iota(jnp.int32, (4 * C, C), 0) % C
    cols = lax.broadcasted_iota(jnp.int32, (4 * C, C), 1)
    state = state_ref[...]
    att = jnp.where(rows >= cols, _mm_nt(stack(q_e), k_e), 0.0)
    intra = _mm(att, v)
    inter = _mm(stack(q_in), state)
    update = _mm_tn(k_d, v)
    for h in range(4):
        ks = slice(h * 64, (h + 1) * 64)
        vs = slice(h * 128, (h + 1) * 128)
        cs = slice(h * C, (h + 1) * C)
        o_h = inter[cs, :] + intra[cs, vs]
        state_ref[ks, :] = decay_col[ks, :] * state[ks, :] + update[ks, vs]
        g_h = gate[:, vs]
        o_ref[:, vs] = _rms(o_h, hnorm_ref[h:h + 1, :]) * (g_h * jax.nn.sigmoid(g_h))


def _scan_body(gla_ref, ret_ref, wa2_ref, ba_ref, gn_ref, rn_ref, cos_ref, sin_ref, lg_ref, sg0_ref, sr0_ref,
               oa_ref, ob_ref, sg_ref, sr_ref, sg_scr, sr_scr, *, C, valid):
    c = pl.program_id(1)

    @pl.when(c == 0)
    def _():
        sg_scr[...] = sg0_ref[...]
        sr_scr[...] = sr0_ref[...]

    row = lax.broadcasted_iota(jnp.int32, (C, 1), 0)
    tri = (lax.broadcasted_iota(jnp.int32, (C, C), 0) >= lax.broadcasted_iota(jnp.int32, (C, C), 1)).astype(bf16)
    cos = cos_ref[...]
    sin = sin_ref[...]

    for s in range(SCAN_SEQS):
        g = gla_ref[s]
        q = g[:, 0:GLA_QK] * GLA_DK ** -0.5
        k = g[:, GLA_QK:2 * GLA_QK]
        v = g[:, 2 * GLA_QK:2 * GLA_QK + GLA_V]
        lowrank = g[:, 2 * GLA_QK + GLA_V:2 * GLA_QK + GLA_V + RANK_PAD]
        gate = g[:, 2 * GLA_QK + GLA_V + RANK_PAD:]
        log_a = jax.nn.log_sigmoid(_mm(lowrank, wa2_ref[...]) + ba_ref[...]) / GLA_TEMP
        if valid < C:
            log_a = jnp.where(row < valid, log_a, 0.0)
        hi = log_a.astype(bf16)
        rest = log_a - hi.astype(f32)
        mid = rest.astype(bf16)
        lo = (rest - mid.astype(f32)).astype(bf16)
        parts = jnp.dot(tri, jnp.concatenate([hi, mid, lo], axis=1), preferred_element_type=f32)
        b = parts[:, 0:GLA_QK] + parts[:, GLA_QK:2 * GLA_QK] + parts[:, 2 * GLA_QK:]
        _chunk_heads(q, k, v, b, gate, gn_ref, sg_scr.at[s], oa_ref.at[s], C)

        r = ret_ref[s]
        q = r[:, 0:RET_QK] * cos + r[:, 2 * RET_QK + 2 * RET_V:3 * RET_QK + 2 * RET_V] * sin
        k = (r[:, RET_QK:2 * RET_QK] * cos + r[:, 3 * RET_QK + 2 * RET_V:] * sin) * RET_DK ** -0.5
        v = r[:, 2 * RET_QK:2 * RET_QK + RET_V]
        gate = r[:, 2 * RET_QK + RET_V:2 * RET_QK + 2 * RET_V]
        steps = jnp.minimum(row + 1, valid).astype(f32)
        b = steps * lg_ref[...]
        _chunk_heads(q, k, v, b, gate, rn_ref, sr_scr.at[s], ob_ref.at[s], C)

    @pl.when(c == pl.num_programs(1) - 1)
    def _():
        sg_ref[...] = sg_scr[...]
        sr_ref[...] = sr_scr[...]


def _scan(gla, ret, wa2, ba, gn, rn, cos, sin, lg, sg0, sr0, B, S, C, valid):
    n = S // C
    G = SCAN_SEQS
    assert B % G == 0
    gla = gla.reshape(B, S, GLA_COLS)
    ret = ret.reshape(B, S, RET_COLS)
    tok = lambda b, c: (b, c, 0)
    fixed = lambda b, c: (0, 0)
    pos = lambda b, c: (c, 0)
    st = lambda b, c: (b, 0, 0)
    st_block = (G, GLA_HEADS * GLA_DK, GLA_DV)
    sg0 = sg0.reshape(B, GLA_HEADS * GLA_DK, GLA_DV)
    sr0 = sr0.reshape(B, RET_HEADS * RET_DK, RET_DV)
    oa, ob, sg, sr = pl.pallas_call(
        functools.partial(_scan_body, C=C, valid=valid),
        grid=(B // G, n),
        in_specs=[pl.BlockSpec((G, C, GLA_COLS), tok), pl.BlockSpec((G, C, RET_COLS), tok),
                  pl.BlockSpec((RANK_PAD, GLA_QK), fixed), pl.BlockSpec((1, GLA_QK), fixed),
                  pl.BlockSpec((GLA_HEADS, GLA_DV), fixed), pl.BlockSpec((RET_HEADS, RET_DV), fixed),
                  pl.BlockSpec((C, RET_QK), pos), pl.BlockSpec((C, RET_QK), pos),
                  pl.BlockSpec((1, RET_QK), fixed),
                  pl.BlockSpec(st_block, st), pl.BlockSpec(st_block, st)],
        out_specs=[pl.BlockSpec((G, C, GLA_V), tok), pl.BlockSpec((G, C, RET_V), tok),
                   pl.BlockSpec(st_block, st), pl.BlockSpec(st_block, st)],
        out_shape=[jax.ShapeDtypeStruct((B, S, GLA_V), f32), jax.ShapeDtypeStruct((B, S, RET_V), f32),
                   jax.ShapeDtypeStruct((B,) + st_block[1:], f32), jax.ShapeDtypeStruct((B,) + st_block[1:], f32)],
        scratch_shapes=[pltpu.VMEM(st_block, f32), pltpu.VMEM(st_block, f32)],
        compiler_params=_cparams(("parallel", "arbitrary")),
        name="scan",
    )(gla, ret, wa2, ba, gn, rn, cos, sin, lg, sg0, sr0)
    state_shape = (B, GLA_HEADS, GLA_DK, GLA_DV)
    return oa.reshape(B * S, GLA_V), ob.reshape(B * S, RET_V), sg.reshape(state_shape), sr.reshape(state_shape)


def _merge_body(x_ref, oa_ref, ob_ref, z_ref, wpa_ref, wpb_ref, wo_ref, gx_ref, wxq_ref, h_ref, q_ref):
    z = z_ref[...]
    merged = (jax.nn.sigmoid(z[:, :D_MODEL]) * _mm(oa_ref[...], wpa_ref[...])
              + jax.nn.sigmoid(z[:, D_MODEL:]) * _mm(ob_ref[...], wpb_ref[...]))
    h = x_ref[...] + _mm(merged, wo_ref[...])
    h_ref[...] = h
    q_ref[...] = _mm(_rms(h, gx_ref[...]), wxq_ref[...])


def _merge(x, oa, ob, z, wpa, wpb, wo, gx, wxq, tm=512):
    T = x.shape[0]
    row = lambda i: (i, 0)
    fixed = lambda i: (0, 0)
    return pl.pallas_call(
        _merge_body,
        grid=(T // tm,),
        in_specs=[pl.BlockSpec((tm, D_MODEL), row), pl.BlockSpec((tm, GLA_V), row), pl.BlockSpec((tm, RET_V), row),
                  pl.BlockSpec((tm, Z_COLS), row),
                  pl.BlockSpec((GLA_V, D_MODEL), fixed), pl.BlockSpec((RET_V, D_MODEL), fixed),
                  pl.BlockSpec((D_MODEL, D_MODEL), fixed), pl.BlockSpec((1, D_MODEL), fixed),
                  pl.BlockSpec((D_MODEL, D_MODEL), fixed)],
        out_specs=[pl.BlockSpec((tm, D_MODEL), row), pl.BlockSpec((tm, D_MODEL), row)],
        out_shape=[jax.ShapeDtypeStruct((T, D_MODEL), f32), jax.ShapeDtypeStruct((T, D_MODEL), f32)],
        compiler_params=_cparams(("parallel",)),
        name="merge",
    )(x, oa, ob, z, wpa, wpb, wo, gx, wxq)


def _mem_kv_body(m_ref, g_ref, wk_ref, wv_ref, k_ref, v_ref):
    mn = _rms(m_ref[...], g_ref[...]).astype(bf16)
    k_ref[...] = jnp.dot(mn, wk_ref[...], preferred_element_type=f32)
    v_ref[...] = jnp.dot(mn, wv_ref[...], preferred_element_type=f32)


def _mem_kv(mem, g, wk, wv, tm=256):
    T = mem.shape[0]
    row = lambda i: (i, 0)
    fixed = lambda i: (0, 0)
    return pl.pallas_call(
        _mem_kv_body,
        grid=(T // tm,),
        in_specs=[pl.BlockSpec((tm, D_MODEL), row), pl.BlockSpec((1, D_MODEL), fixed),
                  pl.BlockSpec((D_MODEL, D_MODEL), fixed), pl.BlockSpec((D_MODEL, D_MODEL), fixed)],
        out_specs=[pl.BlockSpec((tm, D_MODEL), row), pl.BlockSpec((tm, D_MODEL), row)],
        out_shape=[jax.ShapeDtypeStruct((T, D_MODEL), f32), jax.ShapeDtypeStruct((T, D_MODEL), f32)],
        compiler_params=_cparams(("parallel",)),
        name="mem_kv",
    )(mem, g, wk, wv)


def _xattn_body(q_ref, k_ref, v_ref, o_ref, *, seqs, tq):
    for s in range(seqs):
        rows = slice(s * tq, (s + 1) * tq)
        q = q_ref[rows, :]
        for h in range(XA_HEADS):
            hs = slice(h * XA_DH, (h + 1) * XA_DH)
            sc = _mm_nt(q[:, hs], k_ref[s, :, hs]) * XA_DH ** -0.5
            p = jnp.exp(sc - jnp.max(sc, axis=-1, keepdims=True))
            p = p / jnp.sum(p, axis=-1, keepdims=True)
            o_ref[rows, hs] = _mm(p, v_ref[s, :, hs])


def _xattn(q, mk, mv, B, S, tq, seqs):
    n = S // tq
    assert seqs == 1 or n == 1
    tok = lambda b, j: (b * n + j, 0)
    mem_spec = pl.BlockSpec((seqs, N_MEM, D_MODEL), lambda b, j: (b, 0, 0))
    return pl.pallas_call(
        functools.partial(_xattn_body, seqs=seqs, tq=tq),
        grid=(B // seqs, n),
        in_specs=[pl.BlockSpec((seqs * tq, D_MODEL), tok), mem_spec, mem_spec],
        out_specs=pl.BlockSpec((seqs * tq, D_MODEL), tok),
        out_shape=jax.ShapeDtypeStruct((B * S, D_MODEL), f32),
        compiler_params=_cparams(("parallel", "parallel")),
        name="xattn",
    )(q, mk, mv)


def _xattn_cache_body(q_ref, k_hbm, v_hbm, o_ref, kbuf, vbuf, sem, *, seqs, tq, layer):
    i = pl.program_id(0)
    n = pl.num_programs(0)

    def copies(step, slot):
        out = []
        for s in range(seqs):
            for h in range(XA_HEADS):
                hs = pl.ds(h * XA_DH, XA_DH)
                b = step * seqs + s
                out.append(pltpu.make_async_copy(k_hbm.at[layer, b, :, h, :], kbuf.at[slot, s, :, hs],
                                                 sem.at[slot, 0, s, h]))
                out.append(pltpu.make_async_copy(v_hbm.at[layer, b, :, h, :], vbuf.at[slot, s, :, hs],
                                                 sem.at[slot, 1, s, h]))
        return out

    @pl.when(i == 0)
    def _():
        for c in copies(0, 0):
            c.start()

    @pl.when(i + 1 < n)
    def _():
        for c in copies(i + 1, (i + 1) % 2):
            c.start()

    slot = i % 2
    for c in copies(i, slot):
        c.wait()
    lane_head = lax.broadcasted_iota(jnp.int32, (tq, D_MODEL), 1) // XA_DH
    for s in range(seqs):
        rows = slice(s * tq, (s + 1) * tq)
        q = q_ref[rows, :]
        stacked = jnp.concatenate([jnp.where(lane_head == h, q, 0.0) for h in range(XA_HEADS)], axis=0)
        sc = _mm_nt(stacked, kbuf[slot, s]) * XA_DH ** -0.5
        p = jnp.exp(sc - jnp.max(sc, axis=-1, keepdims=True))
        p = p / jnp.sum(p, axis=-1, keepdims=True)
        o = _mm(p, vbuf[slot, s])
        for h in range(XA_HEADS):
            hs = slice(h * XA_DH, (h + 1) * XA_DH)
            o_ref[rows, hs] = o[h * tq:(h + 1) * tq, hs]


def _xattn_cache(q, cache_k, cache_v, layer, B, tq, seqs):
    assert B % seqs == 0
    tok = lambda b: (b, 0)
    buf = pltpu.VMEM((2, seqs, N_MEM, D_MODEL), cache_k.dtype)
    return pl.pallas_call(
        functools.partial(_xattn_cache_body, seqs=seqs, tq=tq, layer=layer),
        grid=(B // seqs,),
        in_specs=[pl.BlockSpec((seqs * tq, D_MODEL), tok), pl.BlockSpec(memory_space=pl.ANY),
                  pl.BlockSpec(memory_space=pl.ANY)],
        out_specs=pl.BlockSpec((seqs * tq, D_MODEL), tok),
        out_shape=jax.ShapeDtypeStruct((B * tq, D_MODEL), f32),
        scratch_shapes=[buf, buf, pltpu.SemaphoreType.DMA((2, 2, seqs, XA_HEADS))],
        compiler_params=_cparams(("arbitrary",)),
        name="xattn_cache",
    )(q, cache_k, cache_v)


def _merge_exchange_network(n):
    pairs = []
    p = 1
    while p < n:
        k = p
        while k >= 1:
            for j in range(k % p, n - k, 2 * k):
                for i in range(min(k, n - j - k)):
                    if (i + j) // (2 * p) == (i + j + k) // (2 * p):
                        pairs.append((i + j, i + j + k))
            k //= 2
        p *= 2
    return pairs


def _compare_exchange(x, i, j):
    a, b = x[i], x[j]
    if b is None:
        return
    if a is None:
        x[i], x[j] = b, None
        return
    x[i], x[j] = jnp.maximum(a, b), jnp.minimum(a, b)


def _top16(tiles):
    n = PEER_TOPK
    x = list(tiles) + [None] * (n - len(tiles))
    for i, j in _merge_exchange_network(n):
        _compare_exchange(x, i, j)
    for shift in (4, 2, 1):
        merged = []
        for i in range(n):
            a, b = x[i], x[n - 1 - i]
            b = None if b is None else pltpu.roll(b, shift, 0)
            merged.append(b if a is None else (a if b is None else jnp.maximum(a, b)))
        x = merged
        d = n // 2
        while d >= 1:
            for i in range(n):
                if (i & d) == 0:
                    _compare_exchange(x, i, i + d)
            d //= 2
    return x


def _sublane_sum(x):
    for shift in (4, 2, 1):
        x = x + pltpu.roll(x, shift, 0)
    return x


def _rows_to_sublanes(v, sub):
    out = v[SUBLANES - 1]
    for r in range(SUBLANES - 2, -1, -1):
        out = jnp.where(sub == r, v[r], out)
    return out


def _peer_front_body(h_ref, o_ref, wxo_ref, gf_ref, wq_ref, sk_ref,
                     h2_ref, hn_ref, cn1_ref, e1_ref, rk2_ref, e2_ref, hn_scr, *, tm):
    hd = pl.program_id(1)

    @pl.when(hd == 0)
    def _():
        h2 = h_ref[...] + _mm(o_ref[...], wxo_ref[...])
        h2_ref[...] = h2
        hn = _rms(h2, gf_ref[...]).astype(bf16)
        hn_scr[...] = hn
        hn_ref[...] = hn

    pq = jnp.dot(hn_scr[...], wq_ref[0], preferred_element_type=f32)
    half = PEER_DQ // 2
    s1 = _mm_nt(sk_ref[0, 0], pq[:, :half])
    s2 = _mm_nt(sk_ref[0, 1], pq[:, half:])
    sub = lax.broadcasted_iota(jnp.int32, (SUBLANES, LANES), 0)
    for t in range(tm // LANES):
        ls = slice(t * LANES, (t + 1) * LANES)
        a1 = s1[:, ls]
        a2 = s2[:, ls]
        v1 = _top16([a1[SUBLANES * i:SUBLANES * (i + 1)] for i in range(PEER_NKEYS // SUBLANES)])
        v2 = _top16([a2[SUBLANES * i:SUBLANES * (i + 1)] for i in range(PEER_NKEYS // SUBLANES)])
        v1_hi = _rows_to_sublanes(v1[SUBLANES:], sub)
        v2_lo = _rows_to_sublanes(v2[:SUBLANES], sub)
        v2_hi = _rows_to_sublanes(v2[SUBLANES:], sub)
        cand = [v1[0] + v2_lo, v1[0] + v2_hi]
        cand += [v1[r] + v2_lo for r in range(1, SUBLANES)]
        cand += [v1_hi + v2[0]]
        top = _top16(cand)
        z = _sublane_sum(jnp.exp(_rows_to_sublanes(top[:SUBLANES], sub) - top[0])
                         + jnp.exp(_rows_to_sublanes(top[SUBLANES:], sub) - top[0]))
        tau = top[PEER_TOPK - 1]
        picked = lambda c: jnp.where(c >= tau, 1.0, 0.0)
        count = [_sublane_sum(picked(cand[0]) + picked(cand[1]))]
        count += [_sublane_sum(picked(cand[r + 1])) for r in range(1, SUBLANES)]
        count += [picked(v1[r] + v2[0]) for r in range(SUBLANES, PEER_TOPK)]
        count1 = []
        rank2 = []
        for i in range(PEER_NKEYS // SUBLANES):
            k1 = a1[SUBLANES * i:SUBLANES * (i + 1)]
            k2 = a2[SUBLANES * i:SUBLANES * (i + 1)]
            c1 = jnp.zeros_like(k1)
            r2 = jnp.full_like(k2, float(PEER_TOPK))
            for r in range(PEER_TOPK - 1, -1, -1):
                c1 = jnp.where(k1 == v1[r], count[r], c1)
                r2 = jnp.where(k2 == v2[r], float(r), r2)
            count1.append(c1)
            rank2.append(r2)
        cn1_ref[0, :, ls] = jnp.concatenate(count1, axis=0)
        e1_ref[0, :, ls] = jnp.exp(a1 - v1[0][0:1]) / (2.0 * z[0:1])
        rk2_ref[0, :, ls] = jnp.concatenate(rank2, axis=0).astype(bf16)
        e2_ref[0, :, ls] = jnp.exp(a2 - v2[0][0:1]).astype(bf16)


def _peer_front(h, o, wxo, gf, wq, sk, tm=512):
    T = h.shape[0]
    assert T % tm == 0
    row = lambda i, hd: (i, 0)
    fixed = lambda i, hd: (0, 0)
    head_t = lambda i, hd: (hd, 0, i)
    words = jax.ShapeDtypeStruct((PEER_HEADS, PEER_NKEYS, T), f32)
    halfs = jax.ShapeDtypeStruct((PEER_HEADS, PEER_NKEYS, T), bf16)
    sc_spec = pl.BlockSpec((1, PEER_NKEYS, tm), head_t)
    return pl.pallas_call(
        functools.partial(_peer_front_body, tm=tm),
        grid=(T // tm, PEER_HEADS),
        in_specs=[pl.BlockSpec((tm, D_MODEL), row), pl.BlockSpec((tm, D_MODEL), row),
                  pl.BlockSpec((D_MODEL, D_MODEL), fixed), pl.BlockSpec((1, D_MODEL), fixed),
                  pl.BlockSpec((1, D_MODEL, PEER_DQ), lambda i, hd: (hd, 0, 0)),
                  pl.BlockSpec((1, 2, PEER_NKEYS, PEER_DQ // 2), lambda i, hd: (hd, 0, 0, 0))],
        out_specs=[pl.BlockSpec((tm, D_MODEL), row), pl.BlockSpec((tm, D_MODEL), row),
                   sc_spec, sc_spec, sc_spec, sc_spec],
        out_shape=[jax.ShapeDtypeStruct((T, D_MODEL), f32), jax.ShapeDtypeStruct((T, D_MODEL), bf16),
                   words, words, halfs, halfs],
        scratch_shapes=[pltpu.VMEM((tm, D_MODEL), bf16)],
        compiler_params=_cparams(("parallel", "arbitrary")),
        name="peer_front",
    )(h, o, wxo, gf, wq, sk)


PEER_ROWS_PER_BLOCK = SUBLANES
PEER_BLOCK = PEER_ROWS_PER_BLOCK * PEER_NKEYS
GATE_ROWS = 8 * SUBLANES
PEER_DENSE_TOKENS = 1024


def _peer_dense_body(hn_ref, u_ref, vt_prev_ref, vt_last_ref, cn1_ref, e1_ref, rk2_ref, e2_ref, h2_ref, gfin_ref,
                     y_ref, at_scr, w_scr, acc_scr, *, tm):
    j = pl.program_id(1)

    @pl.when(j == 0)
    def _():
        acc_scr[...] = jnp.zeros_like(acc_scr)
        w_scr[...] = jnp.zeros_like(w_scr)

    acc_scr[...] += jnp.dot(vt_prev_ref[...], w_scr[...], preferred_element_type=f32)
    at_scr[...] = lax.dot_general(u_ref[...].astype(bf16), hn_ref[...], (((1,), (1,)), ((), ())),
                                  preferred_element_type=f32)
    chunks = PEER_NKEYS // GATE_ROWS

    def row_tile(ref, hd, r, ls):
        return jnp.broadcast_to(ref[hd, r:r + 1, ls].astype(bf16), (GATE_ROWS, LANES))

    def gate_tile(i, carry):
        ls = pl.ds(pl.multiple_of((i // chunks) * LANES, LANES), LANES)
        c0 = pl.multiple_of((i % chunks) * GATE_ROWS, GATE_ROWS)
        gates = [None] * PEER_ROWS_PER_BLOCK
        for hd in range(PEER_HEADS):
            rk2 = rk2_ref[hd, pl.ds(c0, GATE_ROWS), ls]
            e2 = e2_ref[hd, pl.ds(c0, GATE_ROWS), ls]
            for r in range(PEER_ROWS_PER_BLOCK):
                room = jnp.maximum(row_tile(cn1_ref, hd, r, ls) - rk2, 0.0)
                term = jnp.minimum(row_tile(e1_ref, hd, r, ls) * e2, room)
                gates[r] = term if gates[r] is None else gates[r] + term
        for r in range(PEER_ROWS_PER_BLOCK):
            rs = pl.ds(pl.multiple_of(r * PEER_NKEYS + c0, GATE_ROWS), GATE_ROWS)
            a = at_scr[rs, ls]
            act = a * (1.0 + lax.erf(a * math.sqrt(0.5)))
            w_scr[rs, ls] = act.astype(bf16) * gates[r]
        return carry

    lax.fori_loop(0, (tm // LANES) * chunks, gate_tile, 0)

    @pl.when(j == pl.num_programs(1) - 1)
    def _():
        out_t = acc_scr[...] + jnp.dot(vt_last_ref[...], w_scr[...], preferred_element_type=f32)
        y_ref[...] = _rms(h2_ref[...] + out_t.T, gfin_ref[...])


def _peer_dense(hn, u, vt, cn1, e1, rk2, e2, h2, gfin):
    T = hn.shape[0]
    nb = PEER_BLOCK
    tm = min(T, PEER_DENSE_TOKENS)
    assert T % tm == 0
    row = lambda i, j: (i, 0)
    once = pl.Buffered(1)
    sc1 = pl.BlockSpec((PEER_HEADS, PEER_ROWS_PER_BLOCK, tm), lambda i, j: (0, j, i))
    sc2 = pl.BlockSpec((PEER_HEADS, PEER_NKEYS, tm), lambda i, j: (0, 0, i), pipeline_mode=once)
    return pl.pallas_call(
        functools.partial(_peer_dense_body, tm=tm),
        grid=(T // tm, PEER_N // nb),
        in_specs=[pl.BlockSpec((tm, D_MODEL), row),
                  pl.BlockSpec((nb, D_MODEL), lambda i, j: (j, 0)),
                  pl.BlockSpec((D_MODEL, nb), lambda i, j: (0, jnp.maximum(j - 1, 0))),
                  pl.BlockSpec((D_MODEL, nb), lambda i, j: (0, PEER_N // nb - 1), pipeline_mode=once),
                  sc1, sc1, sc2, sc2,
                  pl.BlockSpec((tm, D_MODEL), row, pipeline_mode=once),
                  pl.BlockSpec((1, D_MODEL), lambda i, j: (0, 0))],
        out_specs=pl.BlockSpec((tm, D_MODEL), row),
        out_shape=jax.ShapeDtypeStruct((T, D_MODEL), f32),
        scratch_shapes=[pltpu.VMEM((nb, tm), f32), pltpu.VMEM((nb, tm), bf16), pltpu.VMEM((D_MODEL, tm), f32)],
        compiler_params=_cparams(("parallel", "arbitrary")),
        name="peer_dense",
    )(hn, u, vt, vt, cn1, e1, rk2, e2, h2, gfin)


def _rotary_tables(pos):
    half = RET_DK // 2
    inv = ROPE_BASE ** (-jnp.arange(half, dtype=f32) / half)
    ang = pos.astype(f32)[:, None] * inv[None, :]
    cos = jnp.cos(ang)
    sin = jnp.sin(ang)
    cos = jnp.tile(jnp.concatenate([cos, cos], axis=-1), (1, RET_HEADS))
    sin = jnp.tile(jnp.concatenate([-sin, sin], axis=-1), (1, RET_HEADS))
    return cos, sin


def _layer_weights(l, norm_mix, w_in, w_a2, b_a, gla_head_norm, ret_head_norm, w_pa, w_pb, w_o, norm_xattn,
                   norm_mem, w_xq, w_xk, w_xv, w_xo, norm_ffn, peer_wq, peer_subkeys, peer_u, peer_v):
    offs = [0]
    for s in IN_SIZES:
        offs.append(offs[-1] + s)
    cols = [w_in[l][:, offs[i]:offs[i + 1]] for i in range(len(IN_SIZES))]
    gq, gk, gv, glr, gr, rq, rk, rv, rg, za, zb = cols
    glr = jnp.pad(glr, ((0, 0), (0, RANK_PAD - GLA_RANK)))
    half = RET_DK // 2
    j = jnp.arange(RET_QK)
    partner = jnp.where((j % RET_DK) < half, j + half, j - half)
    w = {}
    w["gla"] = jnp.concatenate([gq, gk, gv, glr, gr], axis=1).astype(bf16)
    w["ret"] = jnp.concatenate([rq, rk, rv, rg, rq[:, partner], rk[:, partner]], axis=1).astype(bf16)
    w["z"] = jnp.concatenate([za, zb], axis=1).astype(bf16)
    w["a2"] = jnp.pad(w_a2[l], ((0, RANK_PAD - GLA_RANK), (0, 0))).astype(bf16)
    w["ba"] = b_a[l].reshape(1, GLA_QK)
    w["gn"] = gla_head_norm[l]
    w["rn"] = ret_head_norm[l]
    log_gamma = jnp.log1p(-(2.0 ** (-5.0 - jnp.arange(RET_HEADS, dtype=f32))))
    w["lg"] = jnp.repeat(log_gamma, RET_DK).reshape(1, RET_QK)
    w["norm_mix"] = norm_mix[l].reshape(1, D_MODEL)
    w["pa"] = w_pa[l].astype(bf16)
    w["pb"] = w_pb[l].astype(bf16)
    w["o"] = w_o[l].astype(bf16)
    w["norm_xattn"] = norm_xattn[l].reshape(1, D_MODEL)
    w["norm_mem"] = norm_mem[l].reshape(1, D_MODEL)
    w["xq"] = w_xq[l].astype(bf16)
    w["xk"] = w_xk[l].astype(bf16)
    w["xv"] = w_xv[l].astype(bf16)
    w["xo"] = w_xo[l].astype(bf16)
    w["norm_ffn"] = norm_ffn[l].reshape(1, D_MODEL)
    w["wq"] = peer_wq[l].reshape(D_MODEL, PEER_HEADS, PEER_DQ).transpose(1, 0, 2).astype(bf16)
    w["sk"] = peer_subkeys[l].astype(bf16)
    w["u"] = peer_u[l]
    w["vt"] = peer_v[l].T.astype(bf16)
    return w


def _layer(x, B, S, keep, C, valid, cos, sin, xattn, sg0, sr0, w, gfin):
    gla, ret, z = _in_proj(x, w["norm_mix"], w["gla"], w["ret"], w["z"])
    oa, ob, sg, sr = _scan(gla, ret, w["a2"], w["ba"], w["gn"], w["rn"], cos, sin, w["lg"], sg0, sr0, B, S, C, valid)
    h, q = _merge(x, oa, ob, z, w["pa"], w["pb"], w["o"], w["norm_xattn"], w["xq"])
    o = xattn(q)
    if keep < S:
        h = h.reshape(B, S, D_MODEL)[:, :keep].reshape(B * keep, D_MODEL)
        o = o.reshape(B, S, D_MODEL)[:, :keep].reshape(B * keep, D_MODEL)
    h2, hn, cn1, e1, rk2, e2 = _peer_front(h, o, w["xo"], w["norm_ffn"], w["wq"], w["sk"])
    y = _peer_dense(hn, w["u"], w["vt"], cn1, e1, rk2, e2, h2, gfin)
    return y, sg, sr


def kernel(x_prompt, x_sample, mem_prompt, state_gla, state_ret, cache_mem_k, cache_mem_v, norm_mix, w_in, w_a2,
           b_a, gla_head_norm, ret_head_norm, w_pa, w_pb, w_o, norm_xattn, norm_mem, w_xq, w_xk, w_xv, w_xo,
           norm_ffn, peer_wq, peer_subkeys, peer_u, peer_v, norm_final):
    depth = w_in.shape[0]
    assert depth == 1, "the final norm is fused into the layer's last kernel"
    Bp, Sp, _ = x_prompt.shape
    Bs, Ss, _ = x_sample.shape
    l = 0
    w = _layer_weights(l, norm_mix, w_in, w_a2, b_a, gla_head_norm, ret_head_norm, w_pa, w_pb, w_o, norm_xattn,
                       norm_mem, w_xq, w_xk, w_xv, w_xo, norm_ffn, peer_wq, peer_subkeys, peer_u, peer_v)
    gfin = norm_final.reshape(1, D_MODEL)

    cos_p, sin_p = _rotary_tables(jnp.arange(Sp, dtype=jnp.int32))
    mk, mv = _mem_kv(mem_prompt.reshape(Bp * N_MEM, D_MODEL), w["norm_mem"], w["xk"], w["xv"])
    zeros = jnp.zeros((Bp, GLA_HEADS, GLA_DK, GLA_DV), f32)
    xattn_p = functools.partial(_xattn, mk=mk.reshape(Bp, N_MEM, D_MODEL), mv=mv.reshape(Bp, N_MEM, D_MODEL),
                                B=Bp, S=Sp, tq=XATTN_ROWS, seqs=1)
    yp, sgp, srp = _layer(x_prompt.reshape(Bp * Sp, D_MODEL), Bp, Sp, Sp, SCAN_CHUNK, SCAN_CHUNK, cos_p, sin_p,
                          xattn_p, zeros, zeros, w, gfin)

    pad = SAMPLE_PAD - Ss
    xs = jnp.pad(x_sample, ((0, 0), (0, pad), (0, 0))).reshape(Bs * SAMPLE_PAD, D_MODEL)
    cos_s, sin_s = _rotary_tables(PAST_LEN + jnp.arange(SAMPLE_PAD, dtype=jnp.int32))
    xattn_s = functools.partial(_xattn_cache, cache_k=cache_mem_k, cache_v=cache_mem_v, layer=l, B=Bs,
                                tq=SAMPLE_PAD, seqs=XATTN_SEQS)
    ys, sgs, srs = _layer(xs, Bs, SAMPLE_PAD, Ss, SAMPLE_PAD, Ss, cos_s, sin_s,
                          xattn_s, state_gla[l], state_ret[l], w, gfin)
    ys = ys.reshape(Bs, Ss, D_MODEL)

    kv_shape = (1, Bp, N_MEM, XA_HEADS, XA_DH)
    return (yp.reshape(Bp, Sp, D_MODEL), ys, sgp[None], srp[None], mk.reshape(kv_shape), mv.reshape(kv_shape),
            sgs[None], srs[None])
```

```python
import functools
import math

import jax
import jax.numpy as jnp
from jax import lax
from jax.experimental import pallas as pl
from jax.experimental.pallas import tpu as pltpu

f32 = jnp.float32
bf16 = jnp.bfloat16

D_MODEL = 1024
PAST_LEN = 16384
GLA_HEADS, GLA_DK, GLA_DV, GLA_RANK, GLA_TEMP = 4, 64, 128, 16, 16.0
RET_HEADS, RET_DK, RET_DV = 4, 64, 128
ROPE_BASE = 10000.0
N_MEM = 256
XA_HEADS = 4
XA_DH = D_MODEL // XA_HEADS
PEER_HEADS, PEER_NKEYS, PEER_DQ, PEER_TOPK = 8, 128, 256, 16
PEER_N = PEER_NKEYS * PEER_NKEYS
EPS = 1e-6

GLA_QK = GLA_HEADS * GLA_DK
GLA_V = GLA_HEADS * GLA_DV
RET_QK = RET_HEADS * RET_DK
RET_V = RET_HEADS * RET_DV
IN_SIZES = (GLA_QK, GLA_QK, GLA_V, GLA_RANK, GLA_V, RET_QK, RET_QK, RET_V, RET_V, D_MODEL, D_MODEL)

LANES = 128
SUBLANES = 8
RANK_PAD = LANES
GLA_COLS = 2 * GLA_QK + GLA_V + RANK_PAD + GLA_V
RET_COLS = 2 * RET_QK + 2 * RET_V + 2 * RET_QK
Z_COLS = 2 * D_MODEL
SCAN_CHUNK = 64
SCAN_SEQS = 4
SAMPLE_PAD = SUBLANES
XATTN_ROWS = 512
XATTN_SEQS = 8
VMEM_LIMIT = 52 * 1024 * 1024


def _cparams(sem):
    return pltpu.CompilerParams(dimension_semantics=sem, vmem_limit_bytes=VMEM_LIMIT)


def _rms(x, g):
    return x * lax.rsqrt(jnp.mean(x * x, axis=-1, keepdims=True) + EPS) * g


def _mm(a, b):
    return jnp.dot(a.astype(bf16), b.astype(bf16), preferred_element_type=f32)


def _mm_nt(a, b):
    return lax.dot_general(a.astype(bf16), b.astype(bf16), (((1,), (1,)), ((), ())), preferred_element_type=f32)


def _mm_tn(a, b):
    return lax.dot_general(a.astype(bf16), b.astype(bf16), (((0,), (0,)), ((), ())), preferred_element_type=f32)


def _in_proj_body(x_ref, g_ref, wg_ref, wr_ref, wz_ref, og_ref, or_ref, oz_ref):
    xn = _rms(x_ref[...], g_ref[...]).astype(bf16)
    og_ref[...] = jnp.dot(xn, wg_ref[...], preferred_element_type=f32)
    or_ref[...] = jnp.dot(xn, wr_ref[...], preferred_element_type=f32)
    oz_ref[...] = jnp.dot(xn, wz_ref[...], preferred_element_type=f32)


def _in_proj(x, g, wg, wr, wz, tm=256):
    T = x.shape[0]
    row = lambda i: (i, 0)
    fixed = lambda i: (0, 0)
    return pl.pallas_call(
        _in_proj_body,
        grid=(T // tm,),
        in_specs=[pl.BlockSpec((tm, D_MODEL), row), pl.BlockSpec((1, D_MODEL), fixed),
                  pl.BlockSpec((D_MODEL, GLA_COLS), fixed), pl.BlockSpec((D_MODEL, RET_COLS), fixed),
                  pl.BlockSpec((D_MODEL, Z_COLS), fixed)],
        out_specs=[pl.BlockSpec((tm, GLA_COLS), row), pl.BlockSpec((tm, RET_COLS), row),
                   pl.BlockSpec((tm, Z_COLS), row)],
        out_shape=[jax.ShapeDtypeStruct((T, GLA_COLS), f32), jax.ShapeDtypeStruct((T, RET_COLS), f32),
                   jax.ShapeDtypeStruct((T, Z_COLS), f32)],
        compiler_params=_cparams(("parallel",)),
        name="in_proj",
    )(x, g, wg, wr, wz)


def _chunk_heads(q, k, v, b, gate, hnorm_ref, state_ref, o_ref, C):
    mid = C // 2 - 1
    b_mid = b[mid:mid + 1, :]
    b_last = b[C - 1:C, :]
    q_in = q * jnp.exp(b)
    q_e = q * jnp.exp(b - b_mid)
    k_e = k * jnp.exp(b_mid - b)
    k_d = k * jnp.exp(b_last - b)
    decay_col = jnp.exp(jnp.broadcast_to(b_last, (SUBLANES, b.shape[1])).T[:, 0:1])
    lane_head = lax.broadcasted_iota(jnp.int32, (C, 4 * 64), 1) // 64
    stack = lambda x: jnp.concatenate([jnp.where(lane_head == h, x, 0.0) for h in range(4)], axis=0)
    rows = lax.broadcasted_iota(jnp.int32, (4 * C, 4 * C), 0) % C
    cols = lax.broadcasted_iota(jnp.int32, (4 * C, 4 * C), 1) % C
    state = state_ref[...]
    v_rows = jnp.concatenate([v[:, h * 128:(h + 1) * 128] for h in range(4)], axis=0)
    att = jnp.where(rows >= cols, _mm_nt(stack(q_e), stack(k_e)), 0.0)
    o = _mm(jnp.concatenate([stack(q_in), att], axis=1), jnp.concatenate([state, v_rows], axis=0))
    state_ref[...] = decay_col * state + _mm_tn(stack(k_d), v_rows)
    for h in range(4):
        vs = slice(h * 128, (h + 1) * 128)
        g_h = gate[:, vs]
        o_ref[:, vs] = _rms(o[h * C:(h + 1) * C, :], hnorm_ref[h:h + 1, :]) * (g_h * jax.nn.sigmoid(g_h))


def _scan_body(gla_ref, ret_ref, wa2_ref, ba_ref, gn_ref, rn_ref, cos_ref, sin_ref, lg_ref, sg0_ref, sr0_ref,
               oa_ref, ob_ref, sg_ref, sr_ref, sg_scr, sr_scr, *, C, valid):
    c = pl.program_id(1)

    @pl.when(c == 0)
    def _():
        sg_scr[...] = sg0_ref[...]
        sr_scr[...] = sr0_ref[...]

    row = lax.broadcasted_iota(jnp.int32, (C, 1), 0)
    tri = (lax.broadcasted_iota(jnp.int32, (C, C), 0) >= lax.broadcasted_iota(jnp.int32, (C, C), 1)).astype(bf16)
    cos = cos_ref[...]
    sin = sin_ref[...]

    for s in range(SCAN_SEQS):
        g = gla_ref[s]
        q = g[:, 0:GLA_QK] * GLA_DK ** -0.5
        k = g[:, GLA_QK:2 * GLA_QK]
        v = g[:, 2 * GLA_QK:2 * GLA_QK + GLA_V]
        lowrank = g[:, 2 * GLA_QK + GLA_V:2 * GLA_QK + GLA_V + RANK_PAD]
        gate = g[:, 2 * GLA_QK + GLA_V + RANK_PAD:]
        log_a = jax.nn.log_sigmoid(_mm(lowrank, wa2_ref[...]) + ba_ref[...]) / GLA_TEMP
        if valid < C:
            log_a = jnp.where(row < valid, log_a, 0.0)
        hi = log_a.astype(bf16)
        rest = log_a - hi.astype(f32)
        mid = rest.astype(bf16)
        lo = (rest - mid.astype(f32)).astype(bf16)
        parts = jnp.dot(tri, jnp.concatenate([hi, mid, lo], axis=1), preferred_element_type=f32)
        b = parts[:, 0:GLA_QK] + parts[:, GLA_QK:2 * GLA_QK] + parts[:, 2 * GLA_QK:]
        _chunk_heads(q, k, v, b, gate, gn_ref, sg_scr.at[s], oa_ref.at[s], C)

        r = ret_ref[s]
        q = r[:, 0:RET_QK] * cos + r[:, 2 * RET_QK + 2 * RET_V:3 * RET_QK + 2 * RET_V] * sin
        k = (r[:, RET_QK:2 * RET_QK] * cos + r[:, 3 * RET_QK + 2 * RET_V:] * sin) * RET_DK ** -0.5
        v = r[:, 2 * RET_QK:2 * RET_QK + RET_V]
        gate = r[:, 2 * RET_QK + RET_V:2 * RET_QK + 2 * RET_V]
        steps = jnp.minimum(row + 1, valid).astype(f32)
        b = steps * lg_ref[...]
        _chunk_heads(q, k, v, b, gate, rn_ref, sr_scr.at[s], ob_ref.at[s], C)

    @pl.when(c == pl.num_programs(1) - 1)
    def _():
        sg_ref[...] = sg_scr[...]
        sr_ref[...] = sr_scr[...]


def _scan(gla, ret, wa2, ba, gn, rn, cos, sin, lg, sg0, sr0, B, S, C, valid):
    n = S // C
    G = SCAN_SEQS
    assert B % G == 0
    gla = gla.reshape(B, S, GLA_COLS)
    ret = ret.reshape(B, S, RET_COLS)
    tok = lambda b, c: (b, c, 0)
    fixed = lambda b, c: (0, 0)
    pos = lambda b, c: (c, 0)
    st = lambda b, c: (b, 0, 0)
    st_block = (G, GLA_HEADS * GLA_DK, GLA_DV)
    sg0 = sg0.reshape(B, GLA_HEADS * GLA_DK, GLA_DV)
    sr0 = sr0.reshape(B, RET_HEADS * RET_DK, RET_DV)
    oa, ob, sg, sr = pl.pallas_call(
        functools.partial(_scan_body, C=C, valid=valid),
        grid=(B // G, n),
        in_specs=[pl.BlockSpec((G, C, GLA_COLS), tok), pl.BlockSpec((G, C, RET_COLS), tok),
                  pl.BlockSpec((RANK_PAD, GLA_QK), fixed), pl.BlockSpec((1, GLA_QK), fixed),
                  pl.BlockSpec((GLA_HEADS, GLA_DV), fixed), pl.BlockSpec((RET_HEADS, RET_DV), fixed),
                  pl.BlockSpec((C, RET_QK), pos), pl.BlockSpec((C, RET_QK), pos),
                  pl.BlockSpec((1, RET_QK), fixed),
                  pl.BlockSpec(st_block, st), pl.BlockSpec(st_block, st)],
        out_specs=[pl.BlockSpec((G, C, GLA_V), tok), pl.BlockSpec((G, C, RET_V), tok),
                   pl.BlockSpec(st_block, st), pl.BlockSpec(st_block, st)],
        out_shape=[jax.ShapeDtypeStruct((B, S, GLA_V), f32), jax.ShapeDtypeStruct((B, S, RET_V), f32),
                   jax.ShapeDtypeStruct((B,) + st_block[1:], f32), jax.ShapeDtypeStruct((B,) + st_block[1:], f32)],
        scratch_shapes=[pltpu.VMEM(st_block, f32), pltpu.VMEM(st_block, f32)],
        compiler_params=_cparams(("parallel", "arbitrary")),
        name="scan",
    )(gla, ret, wa2, ba, gn, rn, cos, sin, lg, sg0, sr0)
    state_shape = (B, GLA_HEADS, GLA_DK, GLA_DV)
    return oa.reshape(B * S, GLA_V), ob.reshape(B * S, RET_V), sg.reshape(state_shape), sr.reshape(state_shape)


def _merge_body(x_ref, oa_ref, ob_ref, z_ref, wpa_ref, wpb_ref, wo_ref, gx_ref, wxq_ref, h_ref, q_ref):
    z = z_ref[...]
    merged = (jax.nn.sigmoid(z[:, :D_MODEL]) * _mm(oa_ref[...], wpa_ref[...])
              + jax.nn.sigmoid(z[:, D_MODEL:]) * _mm(ob_ref[...], wpb_ref[...]))
    h = x_ref[...] + _mm(merged, wo_ref[...])
    h_ref[...] = h
    q_ref[...] = _mm(_rms(h, gx_ref[...]), wxq_ref[...])


def _merge(x, oa, ob, z, wpa, wpb, wo, gx, wxq, tm=512):
    T = x.shape[0]
    row = lambda i: (i, 0)
    fixed = lambda i: (0, 0)
    return pl.pallas_call(
        _merge_body,
        grid=(T // tm,),
        in_specs=[pl.BlockSpec((tm, D_MODEL), row), pl.BlockSpec((tm, GLA_V), row), pl.BlockSpec((tm, RET_V), row),
                  pl.BlockSpec((tm, Z_COLS), row),
                  pl.BlockSpec((GLA_V, D_MODEL), fixed), pl.BlockSpec((RET_V, D_MODEL), fixed),
                  pl.BlockSpec((D_MODEL, D_MODEL), fixed), pl.BlockSpec((1, D_MODEL), fixed),
                  pl.BlockSpec((D_MODEL, D_MODEL), fixed)],
        out_specs=[pl.BlockSpec((tm, D_MODEL), row), pl.BlockSpec((tm, D_MODEL), row)],
        out_shape=[jax.ShapeDtypeStruct((T, D_MODEL), f32), jax.ShapeDtypeStruct((T, D_MODEL), f32)],
        compiler_params=_cparams(("parallel",)),
        name="merge",
    )(x, oa, ob, z, wpa, wpb, wo, gx, wxq)


def _mem_kv_body(m_ref, g_ref, wk_ref, wv_ref, k_ref, v_ref):
    mn = _rms(m_ref[...], g_ref[...]).astype(bf16)
    k_ref[...] = jnp.dot(mn, wk_ref[...], preferred_element_type=f32)
    v_ref[...] = jnp.dot(mn, wv_ref[...], preferred_element_type=f32)


def _mem_kv(mem, g, wk, wv, tm=256):
    T = mem.shape[0]
    row = lambda i: (i, 0)
    fixed = lambda i: (0, 0)
    return pl.pallas_call(
        _mem_kv_body,
        grid=(T // tm,),
        in_specs=[pl.BlockSpec((tm, D_MODEL), row), pl.BlockSpec((1, D_MODEL), fixed),
                  pl.BlockSpec((D_MODEL, D_MODEL), fixed), pl.BlockSpec((D_MODEL, D_MODEL), fixed)],
        out_specs=[pl.BlockSpec((tm, D_MODEL), row), pl.BlockSpec((tm, D_MODEL), row)],
        out_shape=[jax.ShapeDtypeStruct((T, D_MODEL), f32), jax.ShapeDtypeStruct((T, D_MODEL), f32)],
        compiler_params=_cparams(("parallel",)),
        name="mem_kv",
    )(mem, g, wk, wv)


def _xattn_body(q_ref, k_ref, v_ref, o_ref, *, seqs, tq):
    for s in range(seqs):
        rows = slice(s * tq, (s + 1) * tq)
        q = q_ref[rows, :]
        for h in range(XA_HEADS):
            hs = slice(h * XA_DH, (h + 1) * XA_DH)
            sc = _mm_nt(q[:, hs], k_ref[s, :, hs]) * XA_DH ** -0.5
            p = jnp.exp(sc - jnp.max(sc, axis=-1, keepdims=True))
            p = p / jnp.sum(p, axis=-1, keepdims=True)
            o_ref[rows, hs] = _mm(p, v_ref[s, :, hs])


def _xattn(q, mk, mv, B, S, tq, seqs):
    n = S // tq
    assert seqs == 1 or n == 1
    tok = lambda b, j: (b * n + j, 0)
    mem_spec = pl.BlockSpec((seqs, N_MEM, D_MODEL), lambda b, j: (b, 0, 0))
    return pl.pallas_call(
        functools.partial(_xattn_body, seqs=seqs, tq=tq),
        grid=(B // seqs, n),
        in_specs=[pl.BlockSpec((seqs * tq, D_MODEL), tok), mem_spec, mem_spec],
        out_specs=pl.BlockSpec((seqs * tq, D_MODEL), tok),
        out_shape=jax.ShapeDtypeStruct((B * S, D_MODEL), f32),
        compiler_params=_cparams(("parallel", "parallel")),
        name="xattn",
    )(q, mk, mv)


def _xattn_cache_body(q_ref, k_hbm, v_hbm, o_ref, kbuf, vbuf, sem, *, seqs, tq, layer):
    i = pl.program_id(0)
    n = pl.num_programs(0)

    def copies(step, slot):
        out = []
        for s in range(seqs):
            for h in range(XA_HEADS):
                hs = pl.ds(h * XA_DH, XA_DH)
                b = step * seqs + s
                out.append(pltpu.make_async_copy(k_hbm.at[layer, b, :, h, :], kbuf.at[slot, s, :, hs],
                                                 sem.at[slot, 0, s, h]))
                out.append(pltpu.make_async_copy(v_hbm.at[layer, b, :, h, :], vbuf.at[slot, s, :, hs],
                                                 sem.at[slot, 1, s, h]))
        return out

    @pl.when(i == 0)
    def _():
        for c in copies(0, 0):
            c.start()

    @pl.when(i + 1 < n)
    def _():
        for c in copies(i + 1, (i + 1) % 2):
            c.start()

    slot = i % 2
    for c in copies(i, slot):
        c.wait()
    lane_head = lax.broadcasted_iota(jnp.int32, (tq, D_MODEL), 1) // XA_DH
    for s in range(seqs):
        rows = slice(s * tq, (s + 1) * tq)
        q = q_ref[rows, :]
        stacked = jnp.concatenate([jnp.where(lane_head == h, q, 0.0) for h in range(XA_HEADS)], axis=0)
        sc = _mm_nt(stacked, kbuf[slot, s]) * XA_DH ** -0.5
        p = jnp.exp(sc - jnp.max(sc, axis=-1, keepdims=True))
        p = p / jnp.sum(p, axis=-1, keepdims=True)
        o = _mm(p, vbuf[slot, s])
        for h in range(XA_HEADS):
            hs = slice(h * XA_DH, (h + 1) * XA_DH)
            o_ref[rows, hs] = o[h * tq:(h + 1) * tq, hs]


def _xattn_cache(q, cache_k, cache_v, layer, B, tq, seqs):
    assert B % seqs == 0
    tok = lambda b: (b, 0)
    buf = pltpu.VMEM((2, seqs, N_MEM, D_MODEL), cache_k.dtype)
    return pl.pallas_call(
        functools.partial(_xattn_cache_body, seqs=seqs, tq=tq, layer=layer),
        grid=(B // seqs,),
        in_specs=[pl.BlockSpec((seqs * tq, D_MODEL), tok), pl.BlockSpec(memory_space=pl.ANY),
                  pl.BlockSpec(memory_space=pl.ANY)],
        out_specs=pl.BlockSpec((seqs * tq, D_MODEL), tok),
        out_shape=jax.ShapeDtypeStruct((B * tq, D_MODEL), f32),
        scratch_shapes=[buf, buf, pltpu.SemaphoreType.DMA((2, 2, seqs, XA_HEADS))],
        compiler_params=_cparams(("arbitrary",)),
        name="xattn_cache",
    )(q, cache_k, cache_v)


def _merge_exchange_network(n):
    pairs = []
    p = 1
    while p < n:
        k = p
        while k >= 1:
            for j in range(k % p, n - k, 2 * k):
                for i in range(min(k, n - j - k)):
                    if (i + j) // (2 * p) == (i + j + k) // (2 * p):
                        pairs.append((i + j, i + j + k))
            k //= 2
        p *= 2
    return pairs


def _compare_exchange(x, i, j):
    a, b = x[i], x[j]
    if b is None:
        return
    if a is None:
        x[i], x[j] = b, None
        return
    x[i], x[j] = jnp.maximum(a, b), jnp.minimum(a, b)


def _top16(tiles):
    n = PEER_TOPK
    x = list(tiles) + [None] * (n - len(tiles))
    for i, j in _merge_exchange_network(n):
        _compare_exchange(x, i, j)
    for shift in (4, 2, 1):
        merged = []
        for i in range(n):
            a, b = x[i], x[n - 1 - i]
            b = None if b is None else pltpu.roll(b, shift, 0)
            merged.append(b if a is None else (a if b is None else jnp.maximum(a, b)))
        x = merged
        d = n // 2
        while d >= 1:
            for i in range(n):
                if (i & d) == 0:
                    _compare_exchange(x, i, i + d)
            d //= 2
    return x


def _sublane_sum(x):
    for shift in (4, 2, 1):
        x = x + pltpu.roll(x, shift, 0)
    return x


def _rows_to_sublanes(v, sub):
    out = v[SUBLANES - 1]
    for r in range(SUBLANES - 2, -1, -1):
        out = jnp.where(sub == r, v[r], out)
    return out


def _peer_front_body(h_ref, o_ref, wxo_ref, gf_ref, wq_ref, sk_ref,
                     h2_ref, hn_ref, cn1_ref, e1_ref, rk2_ref, e2_ref, hn_scr, *, tm):
    hd = pl.program_id(1)

    @pl.when(hd == 0)
    def _():
        h2 = h_ref[...] + _mm(o_ref[...], wxo_ref[...])
        h2_ref[...] = h2
        hn = _rms(h2, gf_ref[...]).astype(bf16)
        hn_scr[...] = hn
        hn_ref[...] = hn

    pq = jnp.dot(hn_scr[...], wq_ref[...], preferred_element_type=f32)
    half = PEER_DQ // 2
    s1 = _mm_nt(sk_ref[0, 0], pq[:, :half])
    s2 = _mm_nt(sk_ref[0, 1], pq[:, half:])
    sub = lax.broadcasted_iota(jnp.int32, (SUBLANES, LANES), 0)
    for t in range(tm // LANES):
        ls = slice(t * LANES, (t + 1) * LANES)
        a1 = s1[:, ls]
        a2 = s2[:, ls]
        v1 = _top16([a1[SUBLANES * i:SUBLANES * (i + 1)] for i in range(PEER_NKEYS // SUBLANES)])
        v2 = _top16([a2[SUBLANES * i:SUBLANES * (i + 1)] for i in range(PEER_NKEYS // SUBLANES)])
        v1_hi = _rows_to_sublanes(v1[SUBLANES:], sub)
        v2_lo = _rows_to_sublanes(v2[:SUBLANES], sub)
        v2_hi = _rows_to_sublanes(v2[SUBLANES:], sub)
        cand = [v1[0] + v2_lo, v1[0] + v2_hi]
        cand += [v1[r] + v2_lo for r in range(1, SUBLANES)]
        cand += [v1_hi + v2[0]]
        top = _top16(cand)
        z = _sublane_sum(jnp.exp(_rows_to_sublanes(top[:SUBLANES], sub) - top[0])
                         + jnp.exp(_rows_to_sublanes(top[SUBLANES:], sub) - top[0]))
        tau = top[PEER_TOPK - 1]
        picked = lambda c: jnp.where(c >= tau, 1.0, 0.0)
        count = [_sublane_sum(picked(cand[0]) + picked(cand[1]))]
        count += [_sublane_sum(picked(cand[r + 1])) for r in range(1, SUBLANES)]
        count += [picked(v1[r] + v2[0]) for r in range(SUBLANES, PEER_TOPK)]
        count1 = []
        rank2 = []
        for i in range(PEER_NKEYS // SUBLANES):
            k1 = a1[SUBLANES * i:SUBLANES * (i + 1)]
            k2 = a2[SUBLANES * i:SUBLANES * (i + 1)]
            c1 = jnp.zeros_like(k1)
            r2 = jnp.full_like(k2, float(PEER_TOPK))
            for r in range(PEER_TOPK - 1, -1, -1):
                c1 = jnp.where(k1 == v1[r], count[r], c1)
                r2 = jnp.where(k2 == v2[r], float(r), r2)
            count1.append(c1)
            rank2.append(r2)
        cn1_ref[0, :, ls] = jnp.concatenate(count1, axis=0)
        e1_ref[0, :, ls] = jnp.exp(a1 - v1[0][0:1]) / (2.0 * z[0:1])
        rk2_ref[0, :, ls] = jnp.concatenate(rank2, axis=0).astype(bf16)
        e2_ref[0, :, ls] = jnp.exp(a2 - v2[0][0:1]).astype(bf16)


def _peer_front(h, o, wxo, gf, wq, sk, tm=512):
    T = h.shape[0]
    assert T % tm == 0
    row = lambda i, hd: (i, 0)
    fixed = lambda i, hd: (0, 0)
    head_t = lambda i, hd: (hd, 0, i)
    words = jax.ShapeDtypeStruct((PEER_HEADS, PEER_NKEYS, T), f32)
    halfs = jax.ShapeDtypeStruct((PEER_HEADS, PEER_NKEYS, T), bf16)
    sc_spec = pl.BlockSpec((1, PEER_NKEYS, tm), head_t)
    return pl.pallas_call(
        functools.partial(_peer_front_body, tm=tm),
        grid=(T // tm, PEER_HEADS),
        in_specs=[pl.BlockSpec((tm, D_MODEL), row), pl.BlockSpec((tm, D_MODEL), row),
                  pl.BlockSpec((D_MODEL, D_MODEL), fixed), pl.BlockSpec((1, D_MODEL), fixed),
                  pl.BlockSpec((D_MODEL, PEER_DQ), lambda i, hd: (0, hd)),
                  pl.BlockSpec((1, 2, PEER_NKEYS, PEER_DQ // 2), lambda i, hd: (hd, 0, 0, 0))],
        out_specs=[pl.BlockSpec((tm, D_MODEL), row), pl.BlockSpec((tm, D_MODEL), row),
                   sc_spec, sc_spec, sc_spec, sc_spec],
        out_shape=[jax.ShapeDtypeStruct((T, D_MODEL), f32), jax.ShapeDtypeStruct((T, D_MODEL), bf16),
                   words, words, halfs, halfs],
        scratch_shapes=[pltpu.VMEM((tm, D_MODEL), bf16)],
        compiler_params=_cparams(("parallel", "arbitrary")),
        name="peer_front",
    )(h, o, wxo, gf, wq, sk)


PEER_ROWS_PER_BLOCK = SUBLANES
PEER_BLOCK = PEER_ROWS_PER_BLOCK * PEER_NKEYS
GATE_ROWS = 8 * SUBLANES
PEER_DENSE_TOKENS = 1024


def _peer_dense_body(hn_ref, u_ref, vt_prev_ref, vt_last_ref, cn1_ref, e1_ref, rk2_ref, e2_ref, h2_ref, gfin_ref,
                     y_ref, at_scr, w_scr, acc_scr, *, tm):
    j = pl.program_id(1)

    @pl.when(j == 0)
    def _():
        acc_scr[...] = jnp.zeros_like(acc_scr)
        w_scr[...] = jnp.zeros_like(w_scr)

    acc_scr[...] += jnp.dot(vt_prev_ref[...], w_scr[...], preferred_element_type=f32)
    at_scr[...] = lax.dot_general(u_ref[...].astype(bf16), hn_ref[...], (((1,), (1,)), ((), ())),
                                  preferred_element_type=f32)
    chunks = PEER_NKEYS // GATE_ROWS

    def row_tile(ref, hd, r, ls):
        return jnp.broadcast_to(ref[hd, r:r + 1, ls].astype(bf16), (GATE_ROWS, LANES))

    def gate_tile(i, carry):
        ls = pl.ds(pl.multiple_of((i // chunks) * LANES, LANES), LANES)
        c0 = pl.multiple_of((i % chunks) * GATE_ROWS, GATE_ROWS)
        gates = [None] * PEER_ROWS_PER_BLOCK
        for hd in range(PEER_HEADS):
            rk2 = rk2_ref[hd, pl.ds(c0, GATE_ROWS), ls]
            e2 = e2_ref[hd, pl.ds(c0, GATE_ROWS), ls]
            for r in range(PEER_ROWS_PER_BLOCK):
                room = jnp.maximum(row_tile(cn1_ref, hd, r, ls) - rk2, 0.0)
                term = jnp.minimum(row_tile(e1_ref, hd, r, ls) * e2, room)
                gates[r] = term if gates[r] is None else gates[r] + term
        for r in range(PEER_ROWS_PER_BLOCK):
            rs = pl.ds(pl.multiple_of(r * PEER_NKEYS + c0, GATE_ROWS), GATE_ROWS)
            a = at_scr[rs, ls]
            act = a * (1.0 + lax.erf(a * math.sqrt(0.5)))
            w_scr[rs, ls] = act.astype(bf16) * gates[r]
        return carry

    lax.fori_loop(0, (tm // LANES) * chunks, gate_tile, 0)

    @pl.when(j == pl.num_programs(1) - 1)
    def _():
        out_t = acc_scr[...] + jnp.dot(vt_last_ref[...], w_scr[...], preferred_element_type=f32)
        y_ref[...] = _rms(h2_ref[...] + out_t.T, gfin_ref[...])


def _peer_dense(hn, u, vt, cn1, e1, rk2, e2, h2, gfin):
    T = hn.shape[0]
    nb = PEER_BLOCK
    tm = min(T, PEER_DENSE_TOKENS)
    assert T % tm == 0
    row = lambda i, j: (i, 0)
    once = pl.Buffered(1)
    sc1 = pl.BlockSpec((PEER_HEADS, PEER_ROWS_PER_BLOCK, tm), lambda i, j: (0, j, i))
    sc2 = pl.BlockSpec((PEER_HEADS, PEER_NKEYS, tm), lambda i, j: (0, 0, i), pipeline_mode=once)
    return pl.pallas_call(
        functools.partial(_peer_dense_body, tm=tm),
        grid=(T // tm, PEER_N // nb),
        in_specs=[pl.BlockSpec((tm, D_MODEL), row),
                  pl.BlockSpec((nb, D_MODEL), lambda i, j: (j, 0)),
                  pl.BlockSpec((D_MODEL, nb), lambda i, j: (0, jnp.maximum(j - 1, 0))),
                  pl.BlockSpec((D_MODEL, nb), lambda i, j: (0, PEER_N // nb - 1), pipeline_mode=once),
                  sc1, sc1, sc2, sc2,
                  pl.BlockSpec((tm, D_MODEL), row, pipeline_mode=once),
                  pl.BlockSpec((1, D_MODEL), lambda i, j: (0, 0))],
        out_specs=pl.BlockSpec((tm, D_MODEL), row),
        out_shape=jax.ShapeDtypeStruct((T, D_MODEL), f32),
        scratch_shapes=[pltpu.VMEM((nb, tm), f32), pltpu.VMEM((nb, tm), bf16), pltpu.VMEM((D_MODEL, tm), f32)],
        compiler_params=_cparams(("parallel", "arbitrary")),
        name="peer_dense",
    )(hn, u, vt, vt, cn1, e1, rk2, e2, h2, gfin)


def _rotary_tables(pos):
    half = RET_DK // 2
    inv = ROPE_BASE ** (-jnp.arange(half, dtype=f32) / half)
    ang = pos.astype(f32)[:, None] * inv[None, :]
    cos = jnp.cos(ang)
    sin = jnp.sin(ang)
    cos = jnp.tile(jnp.concatenate([cos, cos], axis=-1), (1, RET_HEADS))
    sin = jnp.tile(jnp.concatenate([-sin, sin], axis=-1), (1, RET_HEADS))
    return cos, sin


def _layer_weights(l, norm_mix, w_in, w_a2, b_a, gla_head_norm, ret_head_norm, w_pa, w_pb, w_o, norm_xattn,
                   norm_mem, w_xq, w_xk, w_xv, w_xo, norm_ffn, peer_wq, peer_subkeys, peer_u, peer_v):
    offs = [0]
    for s in IN_SIZES:
        offs.append(offs[-1] + s)
    cols = [w_in[l][:, offs[i]:offs[i + 1]] for i in range(len(IN_SIZES))]
    gq, gk, gv, glr, gr, rq, rk, rv, rg, za, zb = cols
    glr = jnp.pad(glr, ((0, 0), (0, RANK_PAD - GLA_RANK)))
    half = RET_DK // 2
    j = jnp.arange(RET_QK)
    partner = jnp.where((j % RET_DK) < half, j + half, j - half)
    w = {}
    w["gla"] = jnp.concatenate([gq, gk, gv, glr, gr], axis=1).astype(bf16)
    w["ret"] = jnp.concatenate([rq, rk, rv, rg, rq[:, partner], rk[:, partner]], axis=1).astype(bf16)
    w["z"] = jnp.concatenate([za, zb], axis=1).astype(bf16)
    w["a2"] = jnp.pad(w_a2[l], ((0, RANK_PAD - GLA_RANK), (0, 0))).astype(bf16)
    w["ba"] = b_a[l].reshape(1, GLA_QK)
    w["gn"] = gla_head_norm[l]
    w["rn"] = ret_head_norm[l]
    log_gamma = jnp.log1p(-(2.0 ** (-5.0 - jnp.arange(RET_HEADS, dtype=f32))))
    w["lg"] = jnp.repeat(log_gamma, RET_DK).reshape(1, RET_QK)
    w["norm_mix"] = norm_mix[l].reshape(1, D_MODEL)
    w["pa"] = w_pa[l].astype(bf16)
    w["pb"] = w_pb[l].astype(bf16)
    w["o"] = w_o[l].astype(bf16)
    w["norm_xattn"] = norm_xattn[l].reshape(1, D_MODEL)
    w["norm_mem"] = norm_mem[l].reshape(1, D_MODEL)
    w["xq"] = w_xq[l].astype(bf16)
    w["xk"] = w_xk[l].astype(bf16)
    w["xv"] = w_xv[l].astype(bf16)
    w["xo"] = w_xo[l].astype(bf16)
    w["norm_ffn"] = norm_ffn[l].reshape(1, D_MODEL)
    w["wq"] = peer_wq[l].astype(bf16)
    w["sk"] = peer_subkeys[l].astype(bf16)
    w["u"] = peer_u[l]
    w["vt"] = peer_v[l].T.astype(bf16)
    return w


def _layer(x, B, S, keep, C, valid, cos, sin, xattn, sg0, sr0, w, gfin):
    gla, ret, z = _in_proj(x, w["norm_mix"], w["gla"], w["ret"], w["z"])
    oa, ob, sg, sr = _scan(gla, ret, w["a2"], w["ba"], w["gn"], w["rn"], cos, sin, w["lg"], sg0, sr0, B, S, C, valid)
    h, q = _merge(x, oa, ob, z, w["pa"], w["pb"], w["o"], w["norm_xattn"], w["xq"])
    o = xattn(q)
    if keep < S:
        h = h.reshape(B, S, D_MODEL)[:, :keep].reshape(B * keep, D_MODEL)
        o = o.reshape(B, S, D_MODEL)[:, :keep].reshape(B * keep, D_MODEL)
    h2, hn, cn1, e1, rk2, e2 = _peer_front(h, o, w["xo"], w["norm_ffn"], w["wq"], w["sk"])
    y = _peer_dense(hn, w["u"], w["vt"], cn1, e1, rk2, e2, h2, gfin)
    return y, sg, sr


def kernel(x_prompt, x_sample, mem_prompt, state_gla, state_ret, cache_mem_k, cache_mem_v, norm_mix, w_in, w_a2,
           b_a, gla_head_norm, ret_head_norm, w_pa, w_pb, w_o, norm_xattn, norm_mem, w_xq, w_xk, w_xv, w_xo,
           norm_ffn, peer_wq, peer_subkeys, peer_u, peer_v, norm_final):
    depth = w_in.shape[0]
    assert depth == 1, "the final norm is fused into the layer's last kernel"
    Bp, Sp, _ = x_prompt.shape
    Bs, Ss, _ = x_sample.shape
    l = 0
    w = _layer_weights(l, norm_mix, w_in, w_a2, b_a, gla_head_norm, ret_head_norm, w_pa, w_pb, w_o, norm_xattn,
                       norm_mem, w_xq, w_xk, w_xv, w_xo, norm_ffn, peer_wq, peer_subkeys, peer_u, peer_v)
    gfin = norm_final.reshape(1, D_MODEL)

    cos_p, sin_p = _rotary_tables(jnp.arange(Sp, dtype=jnp.int32))
    mk, mv = _mem_kv(mem_prompt.reshape(Bp * N_MEM, D_MODEL), w["norm_mem"], w["xk"], w["xv"])
    zeros = jnp.zeros((Bp, GLA_HEADS, GLA_DK, GLA_DV), f32)
    xattn_p = functools.partial(_xattn, mk=mk.reshape(Bp, N_MEM, D_MODEL), mv=mv.reshape(Bp, N_MEM, D_MODEL),
                                B=Bp, S=Sp, tq=XATTN_ROWS, seqs=1)
    yp, sgp, srp = _layer(x_prompt.reshape(Bp * Sp, D_MODEL), Bp, Sp, Sp, SCAN_CHUNK, SCAN_CHUNK, cos_p, sin_p,
                          xattn_p, zeros, zeros, w, gfin)

    pad = SAMPLE_PAD - Ss
    xs = jnp.pad(x_sample, ((0, 0), (0, pad), (0, 0))).reshape(Bs * SAMPLE_PAD, D_MODEL)
    cos_s, sin_s = _rotary_tables(PAST_LEN + jnp.arange(SAMPLE_PAD, dtype=jnp.int32))
    xattn_s = functools.partial(_xattn_cache, cache_k=cache_mem_k, cache_v=cache_mem_v, layer=l, B=Bs,
                                tq=SAMPLE_PAD, seqs=XATTN_SEQS)
    ys, sgs, srs = _layer(xs, Bs, SAMPLE_PAD, Ss, SAMPLE_PAD, Ss, cos_s, sin_s,
                          xattn_s, state_gla[l], state_ret[l], w, gfin)
    ys = ys.reshape(Bs, Ss, D_MODEL)

    kv_shape = (1, Bp, N_MEM, XA_HEADS, XA_DH)
    return (yp.reshape(Bp, Sp, D_MODEL), ys, sgp[None], srp[None], mk.reshape(kv_shape), mv.reshape(kv_shape),
            sgs[None], srs[None])
```

```python
import functools
import math

import jax
import jax.numpy as jnp
from jax import lax
from jax.experimental import pallas as pl
from jax.experimental.pallas import tpu as pltpu

f32 = jnp.float32
bf16 = jnp.bfloat16

D_MODEL = 1024
PAST_LEN = 16384
GLA_HEADS, GLA_DK, GLA_DV, GLA_RANK, GLA_TEMP = 4, 64, 128, 16, 16.0
RET_HEADS, RET_DK, RET_DV = 4, 64, 128
ROPE_BASE = 10000.0
N_MEM = 256
XA_HEADS = 4
XA_DH = D_MODEL // XA_HEADS
PEER_HEADS, PEER_NKEYS, PEER_DQ, PEER_TOPK = 8, 128, 256, 16
PEER_N = PEER_NKEYS * PEER_NKEYS
EPS = 1e-6

GLA_QK = GLA_HEADS * GLA_DK
GLA_V = GLA_HEADS * GLA_DV
RET_QK = RET_HEADS * RET_DK
RET_V = RET_HEADS * RET_DV
IN_SIZES = (GLA_QK, GLA_QK, GLA_V, GLA_RANK, GLA_V, RET_QK, RET_QK, RET_V, RET_V, D_MODEL, D_MODEL)

LANES = 128
SUBLANES = 8
RANK_PAD = LANES
GLA_COLS = 2 * GLA_QK + GLA_V + RANK_PAD + GLA_V
RET_COLS = 2 * RET_QK + 2 * RET_V + 2 * RET_QK
Z_COLS = 2 * D_MODEL
SCAN_CHUNK = 64
SCAN_SEQS = 4
SAMPLE_PAD = SUBLANES
XATTN_ROWS = 512
XATTN_SEQS = 8
VMEM_LIMIT = 52 * 1024 * 1024


def _cparams(sem):
    return pltpu.CompilerParams(dimension_semantics=sem, vmem_limit_bytes=VMEM_LIMIT)


def _rms(x, g):
    return x * lax.rsqrt(jnp.mean(x * x, axis=-1, keepdims=True) + EPS) * g


def _mm(a, b):
    return jnp.dot(a.astype(bf16), b.astype(bf16), preferred_element_type=f32)


def _mm_nt(a, b):
    return lax.dot_general(a.astype(bf16), b.astype(bf16), (((1,), (1,)), ((), ())), preferred_element_type=f32)


def _mm_tn(a, b):
    return lax.dot_general(a.astype(bf16), b.astype(bf16), (((0,), (0,)), ((), ())), preferred_element_type=f32)


def _in_proj_body(x_ref, g_ref, wg_ref, wr_ref, wz_ref, og_ref, or_ref, oz_ref):
    xn = _rms(x_ref[...], g_ref[...]).astype(bf16)
    og_ref[...] = jnp.dot(xn, wg_ref[...], preferred_element_type=f32)
    or_ref[...] = jnp.dot(xn, wr_ref[...], preferred_element_type=f32)
    oz_ref[...] = jnp.dot(xn, wz_ref[...], preferred_element_type=f32)


def _in_proj(x, g, wg, wr, wz, tm=256):
    T = x.shape[0]
    row = lambda i: (i, 0)
    fixed = lambda i: (0, 0)
    return pl.pallas_call(
        _in_proj_body,
        grid=(T // tm,),
        in_specs=[pl.BlockSpec((tm, D_MODEL), row), pl.BlockSpec((1, D_MODEL), fixed),
                  pl.BlockSpec((D_MODEL, GLA_COLS), fixed), pl.BlockSpec((D_MODEL, RET_COLS), fixed),
                  pl.BlockSpec((D_MODEL, Z_COLS), fixed)],
        out_specs=[pl.BlockSpec((tm, GLA_COLS), row), pl.BlockSpec((tm, RET_COLS), row),
                   pl.BlockSpec((tm, Z_COLS), row)],
        out_shape=[jax.ShapeDtypeStruct((T, GLA_COLS), f32), jax.ShapeDtypeStruct((T, RET_COLS), f32),
                   jax.ShapeDtypeStruct((T, Z_COLS), f32)],
        compiler_params=_cparams(("parallel",)),
        name="in_proj",
    )(x, g, wg, wr, wz)


def _chunk_heads(q, k, v, b, gate, hnorm_ref, state_ref, o_ref, C):
    mid = C // 2 - 1
    b_mid = b[mid:mid + 1, :]
    b_last = b[C - 1:C, :]
    q_in = q * jnp.exp(b)
    q_e = q * jnp.exp(b - b_mid)
    k_e = k * jnp.exp(b_mid - b)
    k_d = k * jnp.exp(b_last - b)
    decay_col = jnp.exp(jnp.broadcast_to(b_last, (SUBLANES, b.shape[1])).T[:, 0:1])
    lane_head = lax.broadcasted_iota(jnp.int32, (C, 4 * 64), 1) // 64
    stack = lambda x: jnp.concatenate([jnp.where(lane_head == h, x, 0.0) for h in range(4)], axis=0)
    rows = lax.broadcasted_iota(jnp.int32, (4 * C, 4 * C), 0) % C
    cols = lax.broadcasted_iota(jnp.int32, (4 * C, 4 * C), 1) % C
    state = state_ref[...]
    v_rows = jnp.concatenate([v[:, h * 128:(h + 1) * 128] for h in range(4)], axis=0)
    att = jnp.where(rows >= cols, _mm_nt(stack(q_e), stack(k_e)), 0.0)
    o = _mm(jnp.concatenate([stack(q_in), att], axis=1), jnp.concatenate([state, v_rows], axis=0))
    state_ref[...] = decay_col * state + _mm_tn(stack(k_d), v_rows)
    for h in range(4):
        vs = slice(h * 128, (h + 1) * 128)
        g_h = gate[:, vs]
        o_ref[:, vs] = _rms(o[h * C:(h + 1) * C, :], hnorm_ref[h:h + 1, :]) * (g_h * jax.nn.sigmoid(g_h))


def _scan_body(gla_ref, ret_ref, wa2_ref, ba_ref, gn_ref, rn_ref, cos_ref, sin_ref, lg_ref, sg0_ref, sr0_ref,
               oa_ref, ob_ref, sg_ref, sr_ref, sg_scr, sr_scr, *, C, valid):
    c = pl.program_id(1)

    @pl.when(c == 0)
    def _():
        sg_scr[...] = sg0_ref[...]
        sr_scr[...] = sr0_ref[...]

    row = lax.broadcasted_iota(jnp.int32, (C, 1), 0)
    tri = (lax.broadcasted_iota(jnp.int32, (C, C), 0) >= lax.broadcasted_iota(jnp.int32, (C, C), 1)).astype(bf16)
    cos = cos_ref[...]
    sin = sin_ref[...]

    for s in range(SCAN_SEQS):
        g = gla_ref[s]
        q = g[:, 0:GLA_QK] * GLA_DK ** -0.5
        k = g[:, GLA_QK:2 * GLA_QK]
        v = g[:, 2 * GLA_QK:2 * GLA_QK + GLA_V]
        lowrank = g[:, 2 * GLA_QK + GLA_V:2 * GLA_QK + GLA_V + RANK_PAD]
        gate = g[:, 2 * GLA_QK + GLA_V + RANK_PAD:]
        log_a = jax.nn.log_sigmoid(_mm(lowrank, wa2_ref[...]) + ba_ref[...]) / GLA_TEMP
        if valid < C:
            log_a = jnp.where(row < valid, log_a, 0.0)
        hi = log_a.astype(bf16)
        rest = log_a - hi.astype(f32)
        mid = rest.astype(bf16)
        lo = (rest - mid.astype(f32)).astype(bf16)
        parts = jnp.dot(tri, jnp.concatenate([hi, mid, lo], axis=1), preferred_element_type=f32)
        b = parts[:, 0:GLA_QK] + parts[:, GLA_QK:2 * GLA_QK] + parts[:, 2 * GLA_QK:]
        _chunk_heads(q, k, v, b, gate, gn_ref, sg_scr.at[s], oa_ref.at[s], C)

        r = ret_ref[s]
        q = r[:, 0:RET_QK] * cos + r[:, 2 * RET_QK + 2 * RET_V:3 * RET_QK + 2 * RET_V] * sin
        k = (r[:, RET_QK:2 * RET_QK] * cos + r[:, 3 * RET_QK + 2 * RET_V:] * sin) * RET_DK ** -0.5
        v = r[:, 2 * RET_QK:2 * RET_QK + RET_V]
        gate = r[:, 2 * RET_QK + RET_V:2 * RET_QK + 2 * RET_V]
        steps = jnp.minimum(row + 1, valid).astype(f32)
        b = steps * lg_ref[...]
        _chunk_heads(q, k, v, b, gate, rn_ref, sr_scr.at[s], ob_ref.at[s], C)

    @pl.when(c == pl.num_programs(1) - 1)
    def _():
        sg_ref[...] = sg_scr[...]
        sr_ref[...] = sr_scr[...]


def _scan(gla, ret, wa2, ba, gn, rn, cos, sin, lg, sg0, sr0, B, S, C, valid):
    n = S // C
    G = SCAN_SEQS
    assert B % G == 0
    gla = gla.reshape(B, S, GLA_COLS)
    ret = ret.reshape(B, S, RET_COLS)
    tok = lambda b, c: (b, c, 0)
    fixed = lambda b, c: (0, 0)
    pos = lambda b, c: (c, 0)
    st = lambda b, c: (b, 0, 0)
    st_block = (G, GLA_HEADS * GLA_DK, GLA_DV)
    sg0 = sg0.reshape(B, GLA_HEADS * GLA_DK, GLA_DV)
    sr0 = sr0.reshape(B, RET_HEADS * RET_DK, RET_DV)
    oa, ob, sg, sr = pl.pallas_call(
        functools.partial(_scan_body, C=C, valid=valid),
        grid=(B // G, n),
        in_specs=[pl.BlockSpec((G, C, GLA_COLS), tok), pl.BlockSpec((G, C, RET_COLS), tok),
                  pl.BlockSpec((RANK_PAD, GLA_QK), fixed), pl.BlockSpec((1, GLA_QK), fixed),
                  pl.BlockSpec((GLA_HEADS, GLA_DV), fixed), pl.BlockSpec((RET_HEADS, RET_DV), fixed),
                  pl.BlockSpec((C, RET_QK), pos), pl.BlockSpec((C, RET_QK), pos),
                  pl.BlockSpec((1, RET_QK), fixed),
                  pl.BlockSpec(st_block, st), pl.BlockSpec(st_block, st)],
        out_specs=[pl.BlockSpec((G, C, GLA_V), tok), pl.BlockSpec((G, C, RET_V), tok),
                   pl.BlockSpec(st_block, st), pl.BlockSpec(st_block, st)],
        out_shape=[jax.ShapeDtypeStruct((B, S, GLA_V), f32), jax.ShapeDtypeStruct((B, S, RET_V), f32),
                   jax.ShapeDtypeStruct((B,) + st_block[1:], f32), jax.ShapeDtypeStruct((B,) + st_block[1:], f32)],
        scratch_shapes=[pltpu.VMEM(st_block, f32), pltpu.VMEM(st_block, f32)],
        compiler_params=_cparams(("parallel", "arbitrary")),
        name="scan",
    )(gla, ret, wa2, ba, gn, rn, cos, sin, lg, sg0, sr0)
    state_shape = (B, GLA_HEADS, GLA_DK, GLA_DV)
    return oa.reshape(B * S, GLA_V), ob.reshape(B * S, RET_V), sg.reshape(state_shape), sr.reshape(state_shape)


def _merge_body(x_ref, oa_ref, ob_ref, z_ref, wpa_ref, wpb_ref, wo_ref, gx_ref, wxq_ref, h_ref, q_ref):
    z = z_ref[...]
    merged = (jax.nn.sigmoid(z[:, :D_MODEL]) * _mm(oa_ref[...], wpa_ref[...])
              + jax.nn.sigmoid(z[:, D_MODEL:]) * _mm(ob_ref[...], wpb_ref[...]))
    h = x_ref[...] + _mm(merged, wo_ref[...])
    h_ref[...] = h
    q_ref[...] = _mm(_rms(h, gx_ref[...]), wxq_ref[...])


def _merge(x, oa, ob, z, wpa, wpb, wo, gx, wxq, tm=512):
    T = x.shape[0]
    row = lambda i: (i, 0)
    fixed = lambda i: (0, 0)
    return pl.pallas_call(
        _merge_body,
        grid=(T // tm,),
        in_specs=[pl.BlockSpec((tm, D_MODEL), row), pl.BlockSpec((tm, GLA_V), row), pl.BlockSpec((tm, RET_V), row),
                  pl.BlockSpec((tm, Z_COLS), row),
                  pl.BlockSpec((GLA_V, D_MODEL), fixed), pl.BlockSpec((RET_V, D_MODEL), fixed),
                  pl.BlockSpec((D_MODEL, D_MODEL), fixed), pl.BlockSpec((1, D_MODEL), fixed),
                  pl.BlockSpec((D_MODEL, D_MODEL), fixed)],
        out_specs=[pl.BlockSpec((tm, D_MODEL), row), pl.BlockSpec((tm, D_MODEL), row)],
        out_shape=[jax.ShapeDtypeStruct((T, D_MODEL), f32), jax.ShapeDtypeStruct((T, D_MODEL), f32)],
        compiler_params=_cparams(("parallel",)),
        name="merge",
    )(x, oa, ob, z, wpa, wpb, wo, gx, wxq)


def _mem_kv_body(m_ref, g_ref, wk_ref, wv_ref, k_ref, v_ref):
    mn = _rms(m_ref[...], g_ref[...]).astype(bf16)
    k_ref[...] = jnp.dot(mn, wk_ref[...], preferred_element_type=f32)
    v_ref[...] = jnp.dot(mn, wv_ref[...], preferred_element_type=f32)


def _mem_kv(mem, g, wk, wv, tm=256):
    T = mem.shape[0]
    row = lambda i: (i, 0)
    fixed = lambda i: (0, 0)
    return pl.pallas_call(
        _mem_kv_body,
        grid=(T // tm,),
        in_specs=[pl.BlockSpec((tm, D_MODEL), row), pl.BlockSpec((1, D_MODEL), fixed),
                  pl.BlockSpec((D_MODEL, D_MODEL), fixed), pl.BlockSpec((D_MODEL, D_MODEL), fixed)],
        out_specs=[pl.BlockSpec((tm, D_MODEL), row), pl.BlockSpec((tm, D_MODEL), row)],
        out_shape=[jax.ShapeDtypeStruct((T, D_MODEL), f32), jax.ShapeDtypeStruct((T, D_MODEL), f32)],
        compiler_params=_cparams(("parallel",)),
        name="mem_kv",
    )(mem, g, wk, wv)


def _xattn_body(q_ref, k_ref, v_ref, o_ref, *, seqs, tq):
    for s in range(seqs):
        rows = slice(s * tq, (s + 1) * tq)
        q = q_ref[rows, :]
        for h in range(XA_HEADS):
            hs = slice(h * XA_DH, (h + 1) * XA_DH)
            sc = _mm_nt(q[:, hs], k_ref[s, :, hs]) * XA_DH ** -0.5
            p = jnp.exp(sc - jnp.max(sc, axis=-1, keepdims=True))
            p = p / jnp.sum(p, axis=-1, keepdims=True)
            o_ref[rows, hs] = _mm(p, v_ref[s, :, hs])


def _xattn(q, mk, mv, B, S, tq, seqs):
    n = S // tq
    assert seqs == 1 or n == 1
    tok = lambda b, j: (b * n + j, 0)
    mem_spec = pl.BlockSpec((seqs, N_MEM, D_MODEL), lambda b, j: (b, 0, 0))
    return pl.pallas_call(
        functools.partial(_xattn_body, seqs=seqs, tq=tq),
        grid=(B // seqs, n),
        in_specs=[pl.BlockSpec((seqs * tq, D_MODEL), tok), mem_spec, mem_spec],
        out_specs=pl.BlockSpec((seqs * tq, D_MODEL), tok),
        out_shape=jax.ShapeDtypeStruct((B * S, D_MODEL), f32),
        compiler_params=_cparams(("parallel", "parallel")),
        name="xattn",
    )(q, mk, mv)


def _xattn_cache_body(q_ref, k_hbm, v_hbm, o_ref, kbuf, vbuf, sem, *, seqs, tq, layer):
    i = pl.program_id(0)
    n = pl.num_programs(0)

    def copies(step, slot):
        out = []
        for s in range(seqs):
            for h in range(XA_HEADS):
                hs = pl.ds(h * XA_DH, XA_DH)
                b = step * seqs + s
                out.append(pltpu.make_async_copy(k_hbm.at[layer, b, :, h, :], kbuf.at[slot, s, :, hs],
                                                 sem.at[slot, 0, s, h]))
                out.append(pltpu.make_async_copy(v_hbm.at[layer, b, :, h, :], vbuf.at[slot, s, :, hs],
                                                 sem.at[slot, 1, s, h]))
        return out

    @pl.when(i == 0)
    def _():
        for c in copies(0, 0):
            c.start()

    @pl.when(i + 1 < n)
    def _():
        for c in copies(i + 1, (i + 1) % 2):
            c.start()

    slot = i % 2
    for c in copies(i, slot):
        c.wait()
    lane_head = lax.broadcasted_iota(jnp.int32, (tq, D_MODEL), 1) // XA_DH
    for s in range(seqs):
        rows = slice(s * tq, (s + 1) * tq)
        q = q_ref[rows, :]
        stacked = jnp.concatenate([jnp.where(lane_head == h, q, 0.0) for h in range(XA_HEADS)], axis=0)
        sc = _mm_nt(stacked, kbuf[slot, s]) * XA_DH ** -0.5
        p = jnp.exp(sc - jnp.max(sc, axis=-1, keepdims=True))
        p = p / jnp.sum(p, axis=-1, keepdims=True)
        o = _mm(p, vbuf[slot, s])
        for h in range(XA_HEADS):
            hs = slice(h * XA_DH, (h + 1) * XA_DH)
            o_ref[rows, hs] = o[h * tq:(h + 1) * tq, hs]


def _xattn_cache(q, cache_k, cache_v, layer, B, tq, seqs):
    assert B % seqs == 0
    tok = lambda b: (b, 0)
    buf = pltpu.VMEM((2, seqs, N_MEM, D_MODEL), cache_k.dtype)
    return pl.pallas_call(
        functools.partial(_xattn_cache_body, seqs=seqs, tq=tq, layer=layer),
        grid=(B // seqs,),
        in_specs=[pl.BlockSpec((seqs * tq, D_MODEL), tok), pl.BlockSpec(memory_space=pl.ANY),
                  pl.BlockSpec(memory_space=pl.ANY)],
        out_specs=pl.BlockSpec((seqs * tq, D_MODEL), tok),
        out_shape=jax.ShapeDtypeStruct((B * tq, D_MODEL), f32),
        scratch_shapes=[buf, buf, pltpu.SemaphoreType.DMA((2, 2, seqs, XA_HEADS))],
        compiler_params=_cparams(("arbitrary",)),
        name="xattn_cache",
    )(q, cache_k, cache_v)


def _merge_exchange_network(n):
    pairs = []
    p = 1
    while p < n:
        k = p
        while k >= 1:
            for j in range(k % p, n - k, 2 * k):
                for i in range(min(k, n - j - k)):
                    if (i + j) // (2 * p) == (i + j + k) // (2 * p):
                        pairs.append((i + j, i + j + k))
            k //= 2
        p *= 2
    return pairs


def _compare_exchange(x, i, j):
    a, b = x[i], x[j]
    if b is None:
        return
    if a is None:
        x[i], x[j] = b, None
        return
    x[i], x[j] = jnp.maximum(a, b), jnp.minimum(a, b)


def _top16(tiles):
    n = PEER_TOPK
    x = list(tiles) + [None] * (n - len(tiles))
    for i, j in _merge_exchange_network(n):
        _compare_exchange(x, i, j)
    for shift in (4, 2, 1):
        merged = []
        for i in range(n):
            a, b = x[i], x[n - 1 - i]
            b = None if b is None else pltpu.roll(b, shift, 0)
            merged.append(b if a is None else (a if b is None else jnp.maximum(a, b)))
        x = merged
        d = n // 2
        while d >= 1:
            for i in range(n):
                if (i & d) == 0:
                    _compare_exchange(x, i, i + d)
            d //= 2
    return x


def _sublane_sum(x):
    for shift in (4, 2, 1):
        x = x + pltpu.roll(x, shift, 0)
    return x


def _rows_to_sublanes(v, sub):
    out = v[SUBLANES - 1]
    for r in range(SUBLANES - 2, -1, -1):
        out = jnp.where(sub == r, v[r], out)
    return out


def _peer_front_body(h_ref, o_ref, wxo_ref, gf_ref, wq_ref, sk_ref,
                     h2_ref, hn_ref, cn1_ref, e1_ref, rk2_ref, e2_ref, hn_scr, *, tm):
    hd = pl.program_id(1)

    @pl.when(hd == 0)
    def _():
        h2 = h_ref[...] + _mm(o_ref[...], wxo_ref[...])
        h2_ref[...] = h2
        hn = _rms(h2, gf_ref[...]).astype(bf16)
        hn_scr[...] = hn
        hn_ref[...] = hn

    pq = jnp.dot(hn_scr[...], wq_ref[...], preferred_element_type=f32)
    half = PEER_DQ // 2
    s1 = _mm_nt(sk_ref[0, 0], pq[:, :half])
    s2 = _mm_nt(sk_ref[0, 1], pq[:, half:])
    sub = lax.broadcasted_iota(jnp.int32, (SUBLANES, LANES), 0)
    for t in range(tm // LANES):
        ls = slice(t * LANES, (t + 1) * LANES)
        a1 = s1[:, ls]
        a2 = s2[:, ls]
        v1 = _top16([a1[SUBLANES * i:SUBLANES * (i + 1)] for i in range(PEER_NKEYS // SUBLANES)])
        v2 = _top16([a2[SUBLANES * i:SUBLANES * (i + 1)] for i in range(PEER_NKEYS // SUBLANES)])
        v1_hi = _rows_to_sublanes(v1[SUBLANES:], sub)
        v2_lo = _rows_to_sublanes(v2[:SUBLANES], sub)
        v2_hi = _rows_to_sublanes(v2[SUBLANES:], sub)
        cand = [v1[0] + v2_lo, v1[0] + v2_hi]
        cand += [v1[r] + v2_lo for r in range(1, SUBLANES)]
        cand += [v1_hi + v2[0]]
        top = _top16(cand)
        z = _sublane_sum(jnp.exp(_rows_to_sublanes(top[:SUBLANES], sub) - top[0])
                         + jnp.exp(_rows_to_sublanes(top[SUBLANES:], sub) - top[0]))
        tau = top[PEER_TOPK - 1]
        picked = lambda c: jnp.where(c >= tau, 1.0, 0.0)
        count = [_sublane_sum(picked(cand[0]) + picked(cand[1]))]
        count += [_sublane_sum(picked(cand[r + 1])) for r in range(1, SUBLANES)]
        count += [picked(v1[r] + v2[0]) for r in range(SUBLANES, PEER_TOPK)]
        count1 = []
        rank2 = []
        for i in range(PEER_NKEYS // SUBLANES):
            k1 = a1[SUBLANES * i:SUBLANES * (i + 1)]
            k2 = a2[SUBLANES * i:SUBLANES * (i + 1)]
            c1 = jnp.zeros_like(k1)
            r2 = jnp.full_like(k2, float(PEER_TOPK))
            for r in range(PEER_TOPK - 1, -1, -1):
                c1 = jnp.where(k1 == v1[r], count[r], c1)
                r2 = jnp.where(k2 == v2[r], float(r), r2)
            count1.append(c1)
            rank2.append(r2)
        cn1_ref[0, :, ls] = jnp.concatenate(count1, axis=0)
        e1_ref[0, :, ls] = jnp.exp(a1 - v1[0][0:1]) / (2.0 * z[0:1])
        rk2_ref[0, :, ls] = jnp.concatenate(rank2, axis=0).astype(bf16)
        e2_ref[0, :, ls] = jnp.exp(a2 - v2[0][0:1]).astype(bf16)


def _peer_front(h, o, wxo, gf, wq, sk, tm=512):
    T = h.shape[0]
    assert T % tm == 0
    row = lambda i, hd: (i, 0)
    fixed = lambda i, hd: (0, 0)
    head_t = lambda i, hd: (hd, 0, i)
    words = jax.ShapeDtypeStruct((PEER_HEADS, PEER_NKEYS, T), f32)
    halfs = jax.ShapeDtypeStruct((PEER_HEADS, PEER_NKEYS, T), bf16)
    sc_spec = pl.BlockSpec((1, PEER_NKEYS, tm), head_t)
    return pl.pallas_call(
        functools.partial(_peer_front_body, tm=tm),
        grid=(T // tm, PEER_HEADS),
        in_specs=[pl.BlockSpec((tm, D_MODEL), row), pl.BlockSpec((tm, D_MODEL), row),
                  pl.BlockSpec((D_MODEL, D_MODEL), fixed), pl.BlockSpec((1, D_MODEL), fixed),
                  pl.BlockSpec((D_MODEL, PEER_DQ), lambda i, hd: (0, hd)),
                  pl.BlockSpec((1, 2, PEER_NKEYS, PEER_DQ // 2), lambda i, hd: (hd, 0, 0, 0))],
        out_specs=[pl.BlockSpec((tm, D_MODEL), row), pl.BlockSpec((tm, D_MODEL), row),
                   sc_spec, sc_spec, sc_spec, sc_spec],
        out_shape=[jax.ShapeDtypeStruct((T, D_MODEL), f32), jax.ShapeDtypeStruct((T, D_MODEL), bf16),
                   words, words, halfs, halfs],
        scratch_shapes=[pltpu.VMEM((tm, D_MODEL), bf16)],
        compiler_params=_cparams(("parallel", "arbitrary")),
        name="peer_front",
    )(h, o, wxo, gf, wq, sk)


PEER_ROWS_PER_BLOCK = SUBLANES
PEER_BLOCK = PEER_ROWS_PER_BLOCK * PEER_NKEYS
GATE_ROWS = 8 * SUBLANES
PEER_DENSE_TOKENS = 1024


def _peer_dense_body(hn_ref, u_ref, vt_prev_ref, vt_last_ref, cn1_ref, e1_ref, rk2_ref, e2_ref, h2_ref, gfin_ref,
                     y_ref, act_scr, w_scr, acc_scr, *, tm):
    j = pl.program_id(1)

    @pl.when(j == 0)
    def _():
        acc_scr[...] = jnp.zeros_like(acc_scr)
        w_scr[...] = jnp.zeros_like(w_scr)

    a = lax.dot_general(u_ref[...].astype(bf16), hn_ref[...], (((1,), (1,)), ((), ())), preferred_element_type=f32)
    act_scr[...] = (a * (1.0 + lax.erf(a * math.sqrt(0.5)))).astype(bf16)
    acc_scr[...] += jnp.dot(vt_prev_ref[...], w_scr[...], preferred_element_type=f32)
    chunks = PEER_NKEYS // GATE_ROWS

    def row_tile(ref, hd, r, ls):
        return jnp.broadcast_to(ref[hd, r:r + 1, ls].astype(bf16), (GATE_ROWS, LANES))

    def gate_tile(i, carry):
        ls = pl.ds(pl.multiple_of((i // chunks) * LANES, LANES), LANES)
        c0 = pl.multiple_of((i % chunks) * GATE_ROWS, GATE_ROWS)
        gates = [None] * PEER_ROWS_PER_BLOCK
        for hd in range(PEER_HEADS):
            rk2 = rk2_ref[hd, pl.ds(c0, GATE_ROWS), ls]
            e2 = e2_ref[hd, pl.ds(c0, GATE_ROWS), ls]
            for r in range(PEER_ROWS_PER_BLOCK):
                room = jnp.maximum(row_tile(cn1_ref, hd, r, ls) - rk2, 0.0)
                term = jnp.minimum(row_tile(e1_ref, hd, r, ls) * e2, room)
                gates[r] = term if gates[r] is None else gates[r] + term
        for r in range(PEER_ROWS_PER_BLOCK):
            rs = pl.ds(pl.multiple_of(r * PEER_NKEYS + c0, GATE_ROWS), GATE_ROWS)
            w_scr[rs, ls] = act_scr[rs, ls] * gates[r]
        return carry

    lax.fori_loop(0, (tm // LANES) * chunks, gate_tile, 0)

    @pl.when(j == pl.num_programs(1) - 1)
    def _():
        out_t = acc_scr[...] + jnp.dot(vt_last_ref[...], w_scr[...], preferred_element_type=f32)
        y_ref[...] = _rms(h2_ref[...] + out_t.T, gfin_ref[...])


def _peer_dense(hn, u, vt, cn1, e1, rk2, e2, h2, gfin):
    T = hn.shape[0]
    nb = PEER_BLOCK
    tm = min(T, PEER_DENSE_TOKENS)
    assert T % tm == 0
    row = lambda i, j: (i, 0)
    once = pl.Buffered(1)
    sc1 = pl.BlockSpec((PEER_HEADS, PEER_ROWS_PER_BLOCK, tm), lambda i, j: (0, j, i))
    sc2 = pl.BlockSpec((PEER_HEADS, PEER_NKEYS, tm), lambda i, j: (0, 0, i), pipeline_mode=once)
    return pl.pallas_call(
        functools.partial(_peer_dense_body, tm=tm),
        grid=(T // tm, PEER_N // nb),
        in_specs=[pl.BlockSpec((tm, D_MODEL), row),
                  pl.BlockSpec((nb, D_MODEL), lambda i, j: (j, 0)),
                  pl.BlockSpec((D_MODEL, nb), lambda i, j: (0, jnp.maximum(j - 1, 0))),
                  pl.BlockSpec((D_MODEL, nb), lambda i, j: (0, PEER_N // nb - 1), pipeline_mode=once),
                  sc1, sc1, sc2, sc2,
                  pl.BlockSpec((tm, D_MODEL), row, pipeline_mode=once),
                  pl.BlockSpec((1, D_MODEL), lambda i, j: (0, 0))],
        out_specs=pl.BlockSpec((tm, D_MODEL), row),
        out_shape=jax.ShapeDtypeStruct((T, D_MODEL), f32),
        scratch_shapes=[pltpu.VMEM((nb, tm), bf16), pltpu.VMEM((nb, tm), bf16), pltpu.VMEM((D_MODEL, tm), f32)],
        compiler_params=_cparams(("parallel", "arbitrary")),
        name="peer_dense",
    )(hn, u, vt, vt, cn1, e1, rk2, e2, h2, gfin)


def _rotary_tables(pos):
    half = RET_DK // 2
    inv = ROPE_BASE ** (-jnp.arange(half, dtype=f32) / half)
    ang = pos.astype(f32)[:, None] * inv[None, :]
    cos = jnp.cos(ang)
    sin = jnp.sin(ang)
    cos = jnp.tile(jnp.concatenate([cos, cos], axis=-1), (1, RET_HEADS))
    sin = jnp.tile(jnp.concatenate([-sin, sin], axis=-1), (1, RET_HEADS))
    return cos, sin


def _layer_weights(l, norm_mix, w_in, w_a2, b_a, gla_head_norm, ret_head_norm, w_pa, w_pb, w_o, norm_xattn,
                   norm_mem, w_xq, w_xk, w_xv, w_xo, norm_ffn, peer_wq, peer_subkeys, peer_u, peer_v):
    offs = [0]
    for s in IN_SIZES:
        offs.append(offs[-1] + s)
    cols = [w_in[l][:, offs[i]:offs[i + 1]] for i in range(len(IN_SIZES))]
    gq, gk, gv, glr, gr, rq, rk, rv, rg, za, zb = cols
    glr = jnp.pad(glr, ((0, 0), (0, RANK_PAD - GLA_RANK)))
    half = RET_DK // 2
    j = jnp.arange(RET_QK)
    partner = jnp.where((j % RET_DK) < half, j + half, j - half)
    w = {}
    w["gla"] = jnp.concatenate([gq, gk, gv, glr, gr], axis=1).astype(bf16)
    w["ret"] = jnp.concatenate([rq, rk, rv, rg, rq[:, partner], rk[:, partner]], axis=1).astype(bf16)
    w["z"] = jnp.concatenate([za, zb], axis=1).astype(bf16)
    w["a2"] = jnp.pad(w_a2[l], ((0, RANK_PAD - GLA_RANK), (0, 0))).astype(bf16)
    w["ba"] = b_a[l].reshape(1, GLA_QK)
    w["gn"] = gla_head_norm[l]
    w["rn"] = ret_head_norm[l]
    log_gamma = jnp.log1p(-(2.0 ** (-5.0 - jnp.arange(RET_HEADS, dtype=f32))))
    w["lg"] = jnp.repeat(log_gamma, RET_DK).reshape(1, RET_QK)
    w["norm_mix"] = norm_mix[l].reshape(1, D_MODEL)
    w["pa"] = w_pa[l].astype(bf16)
    w["pb"] = w_pb[l].astype(bf16)
    w["o"] = w_o[l].astype(bf16)
    w["norm_xattn"] = norm_xattn[l].reshape(1, D_MODEL)
    w["norm_mem"] = norm_mem[l].reshape(1, D_MODEL)
    w["xq"] = w_xq[l].astype(bf16)
    w["xk"] = w_xk[l].astype(bf16)
    w["xv"] = w_xv[l].astype(bf16)
    w["xo"] = w_xo[l].astype(bf16)
    w["norm_ffn"] = norm_ffn[l].reshape(1, D_MODEL)
    w["wq"] = peer_wq[l].astype(bf16)
    w["sk"] = peer_subkeys[l].astype(bf16)
    w["u"] = peer_u[l]
    w["vt"] = peer_v[l].T.astype(bf16)
    return w


def _layer(x, B, S, keep, C, valid, cos, sin, xattn, sg0, sr0, w, gfin):
    gla, ret, z = _in_proj(x, w["norm_mix"], w["gla"], w["ret"], w["z"])
    oa, ob, sg, sr = _scan(gla, ret, w["a2"], w["ba"], w["gn"], w["rn"], cos, sin, w["lg"], sg0, sr0, B, S, C, valid)
    h, q = _merge(x, oa, ob, z, w["pa"], w["pb"], w["o"], w["norm_xattn"], w["xq"])
    o = xattn(q)
    if keep < S:
        h = h.reshape(B, S, D_MODEL)[:, :keep].reshape(B * keep, D_MODEL)
        o = o.reshape(B, S, D_MODEL)[:, :keep].reshape(B * keep, D_MODEL)
    h2, hn, cn1, e1, rk2, e2 = _peer_front(h, o, w["xo"], w["norm_ffn"], w["wq"], w["sk"])
    y = _peer_dense(hn, w["u"], w["vt"], cn1, e1, rk2, e2, h2, gfin)
    return y, sg, sr


def kernel(x_prompt, x_sample, mem_prompt, state_gla, state_ret, cache_mem_k, cache_mem_v, norm_mix, w_in, w_a2,
           b_a, gla_head_norm, ret_head_norm, w_pa, w_pb, w_o, norm_xattn, norm_mem, w_xq, w_xk, w_xv, w_xo,
           norm_ffn, peer_wq, peer_subkeys, peer_u, peer_v, norm_final):
    depth = w_in.shape[0]
    assert depth == 1, "the final norm is fused into the layer's last kernel"
    Bp, Sp, _ = x_prompt.shape
    Bs, Ss, _ = x_sample.shape
    l = 0
    w = _layer_weights(l, norm_mix, w_in, w_a2, b_a, gla_head_norm, ret_head_norm, w_pa, w_pb, w_o, norm_xattn,
                       norm_mem, w_xq, w_xk, w_xv, w_xo, norm_ffn, peer_wq, peer_subkeys, peer_u, peer_v)
    gfin = norm_final.reshape(1, D_MODEL)

    cos_p, sin_p = _rotary_tables(jnp.arange(Sp, dtype=jnp.int32))
    mk, mv = _mem_kv(mem_prompt.reshape(Bp * N_MEM, D_MODEL), w["norm_mem"], w["xk"], w["xv"])
    zeros = jnp.zeros((Bp, GLA_HEADS, GLA_DK, GLA_DV), f32)
    xattn_p = functools.partial(_xattn, mk=mk.reshape(Bp, N_MEM, D_MODEL), mv=mv.reshape(Bp, N_MEM, D_MODEL),
                                B=Bp, S=Sp, tq=XATTN_ROWS, seqs=1)
    yp, sgp, srp = _layer(x_prompt.reshape(Bp * Sp, D_MODEL), Bp, Sp, Sp, SCAN_CHUNK, SCAN_CHUNK, cos_p, sin_p,
                          xattn_p, zeros, zeros, w, gfin)

    pad = SAMPLE_PAD - Ss
    xs = jnp.pad(x_sample, ((0, 0), (0, pad), (0, 0))).reshape(Bs * SAMPLE_PAD, D_MODEL)
    cos_s, sin_s = _rotary_tables(PAST_LEN + jnp.arange(SAMPLE_PAD, dtype=jnp.int32))
    xattn_s = functools.partial(_xattn_cache, cache_k=cache_mem_k, cache_v=cache_mem_v, layer=l, B=Bs,
                                tq=SAMPLE_PAD, seqs=XATTN_SEQS)
    ys, sgs, srs = _layer(xs, Bs, SAMPLE_PAD, Ss, SAMPLE_PAD, Ss, cos_s, sin_s,
                          xattn_s, state_gla[l], state_ret[l], w, gfin)
    ys = ys.reshape(Bs, Ss, D_MODEL)

    kv_shape = (1, Bp, N_MEM, XA_HEADS, XA_DH)
    return (yp.reshape(Bp, Sp, D_MODEL), ys, sgp[None], srp[None], mk.reshape(kv_shape), mv.reshape(kv_shape),
            sgs[None], srs[None])
```

```python
import functools
import math

import jax
import jax.numpy as jnp
from jax import lax
from jax.experimental import pallas as pl
from jax.experimental.pallas import tpu as pltpu

f32 = jnp.float32
bf16 = jnp.bfloat16

D_MODEL = 1024
PAST_LEN = 16384
GLA_HEADS, GLA_DK, GLA_DV, GLA_RANK, GLA_TEMP = 4, 64, 128, 16, 16.0
RET_HEADS, RET_DK, RET_DV = 4, 64, 128
ROPE_BASE = 10000.0
N_MEM = 256
XA_HEADS = 4
XA_DH = D_MODEL // XA_HEADS
PEER_HEADS, PEER_NKEYS, PEER_DQ, PEER_TOPK = 8, 128, 256, 16
PEER_N = PEER_NKEYS * PEER_NKEYS
EPS = 1e-6

GLA_QK = GLA_HEADS * GLA_DK
GLA_V = GLA_HEADS * GLA_DV
RET_QK = RET_HEADS * RET_DK
RET_V = RET_HEADS * RET_DV
IN_SIZES = (GLA_QK, GLA_QK, GLA_V, GLA_RANK, GLA_V, RET_QK, RET_QK, RET_V, RET_V, D_MODEL, D_MODEL)

LANES = 128
SUBLANES = 8
RANK_PAD = LANES
GLA_COLS = 2 * GLA_QK + GLA_V + RANK_PAD + GLA_V
RET_COLS = 2 * RET_QK + 2 * RET_V + 2 * RET_QK
Z_COLS = 2 * D_MODEL
SCAN_CHUNK = 64
SCAN_SEQS = 4
SAMPLE_PAD = SUBLANES
XATTN_ROWS = 512
PEER_FRONT_TOKENS = 1024
XATTN_SEQS = 8
VMEM_LIMIT = 52 * 1024 * 1024


def _cparams(sem):
    return pltpu.CompilerParams(dimension_semantics=sem, vmem_limit_bytes=VMEM_LIMIT)


def _rms(x, g):
    return x * lax.rsqrt(jnp.mean(x * x, axis=-1, keepdims=True) + EPS) * g


def _mm(a, b):
    return jnp.dot(a.astype(bf16), b.astype(bf16), preferred_element_type=f32)


def _mm_nt(a, b):
    return lax.dot_general(a.astype(bf16), b.astype(bf16), (((1,), (1,)), ((), ())), preferred_element_type=f32)


def _mm_tn(a, b):
    return lax.dot_general(a.astype(bf16), b.astype(bf16), (((0,), (0,)), ((), ())), preferred_element_type=f32)


def _in_proj_body(x_ref, g_ref, wg_ref, wr_ref, wz_ref, og_ref, or_ref, oz_ref):
    xn = _rms(x_ref[...], g_ref[...]).astype(bf16)
    og_ref[...] = jnp.dot(xn, wg_ref[...], preferred_element_type=f32)
    or_ref[...] = jnp.dot(xn, wr_ref[...], preferred_element_type=f32)
    oz_ref[...] = jnp.dot(xn, wz_ref[...], preferred_element_type=f32)


def _in_proj(x, g, wg, wr, wz, tm=256):
    T = x.shape[0]
    row = lambda i: (i, 0)
    fixed = lambda i: (0, 0)
    return pl.pallas_call(
        _in_proj_body,
        grid=(T // tm,),
        in_specs=[pl.BlockSpec((tm, D_MODEL), row), pl.BlockSpec((1, D_MODEL), fixed),
                  pl.BlockSpec((D_MODEL, GLA_COLS), fixed), pl.BlockSpec((D_MODEL, RET_COLS), fixed),
                  pl.BlockSpec((D_MODEL, Z_COLS), fixed)],
        out_specs=[pl.BlockSpec((tm, GLA_COLS), row), pl.BlockSpec((tm, RET_COLS), row),
                   pl.BlockSpec((tm, Z_COLS), row)],
        out_shape=[jax.ShapeDtypeStruct((T, GLA_COLS), f32), jax.ShapeDtypeStruct((T, RET_COLS), f32),
                   jax.ShapeDtypeStruct((T, Z_COLS), f32)],
        compiler_params=_cparams(("parallel",)),
        name="in_proj",
    )(x, g, wg, wr, wz)


def _chunk_heads(q, k, v, b, gate, hnorm_ref, state_ref, o_ref, C):
    mid = C // 2 - 1
    b_mid = b[mid:mid + 1, :]
    b_last = b[C - 1:C, :]
    q_in = q * jnp.exp(b)
    q_e = q * jnp.exp(b - b_mid)
    k_e = k * jnp.exp(b_mid - b)
    k_d = k * jnp.exp(b_last - b)
    decay_col = jnp.exp(jnp.broadcast_to(b_last, (SUBLANES, b.shape[1])).T[:, 0:1])
    lane_head = lax.broadcasted_iota(jnp.int32, (C, 4 * 64), 1) // 64
    stack = lambda x: jnp.concatenate([jnp.where(lane_head == h, x, 0.0) for h in range(4)], axis=0)
    rows = lax.broadcasted_iota(jnp.int32, (4 * C, 4 * C), 0) % C
    cols = lax.broadcasted_iota(jnp.int32, (4 * C, 4 * C), 1) % C
    state = state_ref[...]
    v_rows = jnp.concatenate([v[:, h * 128:(h + 1) * 128] for h in range(4)], axis=0)
    att = jnp.where(rows >= cols, _mm_nt(stack(q_e), stack(k_e)), 0.0)
    o = _mm(jnp.concatenate([stack(q_in), att], axis=1), jnp.concatenate([state, v_rows], axis=0))
    state_ref[...] = decay_col * state + _mm_tn(stack(k_d), v_rows)
    for h in range(4):
        vs = slice(h * 128, (h + 1) * 128)
        g_h = gate[:, vs]
        o_ref[:, vs] = _rms(o[h * C:(h + 1) * C, :], hnorm_ref[h:h + 1, :]) * (g_h * jax.nn.sigmoid(g_h))


def _scan_body(gla_ref, ret_ref, wa2_ref, ba_ref, gn_ref, rn_ref, cos_ref, sin_ref, lg_ref, sg0_ref, sr0_ref,
               oa_ref, ob_ref, sg_ref, sr_ref, sg_scr, sr_scr, *, C, valid):
    c = pl.program_id(1)

    @pl.when(c == 0)
    def _():
        sg_scr[...] = sg0_ref[...]
        sr_scr[...] = sr0_ref[...]

    row = lax.broadcasted_iota(jnp.int32, (C, 1), 0)
    tri = (lax.broadcasted_iota(jnp.int32, (C, C), 0) >= lax.broadcasted_iota(jnp.int32, (C, C), 1)).astype(bf16)
    cos = cos_ref[...]
    sin = sin_ref[...]

    for s in range(SCAN_SEQS):
        g = gla_ref[s]
        q = g[:, 0:GLA_QK] * GLA_DK ** -0.5
        k = g[:, GLA_QK:2 * GLA_QK]
        v = g[:, 2 * GLA_QK:2 * GLA_QK + GLA_V]
        lowrank = g[:, 2 * GLA_QK + GLA_V:2 * GLA_QK + GLA_V + RANK_PAD]
        gate = g[:, 2 * GLA_QK + GLA_V + RANK_PAD:]
        log_a = jax.nn.log_sigmoid(_mm(lowrank, wa2_ref[...]) + ba_ref[...]) / GLA_TEMP
        if valid < C:
            log_a = jnp.where(row < valid, log_a, 0.0)
        hi = log_a.astype(bf16)
        rest = log_a - hi.astype(f32)
        mid = rest.astype(bf16)
        lo = (rest - mid.astype(f32)).astype(bf16)
        parts = jnp.dot(tri, jnp.concatenate([hi, mid, lo], axis=1), preferred_element_type=f32)
        b = parts[:, 0:GLA_QK] + parts[:, GLA_QK:2 * GLA_QK] + parts[:, 2 * GLA_QK:]
        _chunk_heads(q, k, v, b, gate, gn_ref, sg_scr.at[s], oa_ref.at[s], C)

        r = ret_ref[s]
        q = r[:, 0:RET_QK] * cos + r[:, 2 * RET_QK + 2 * RET_V:3 * RET_QK + 2 * RET_V] * sin
        k = (r[:, RET_QK:2 * RET_QK] * cos + r[:, 3 * RET_QK + 2 * RET_V:] * sin) * RET_DK ** -0.5
        v = r[:, 2 * RET_QK:2 * RET_QK + RET_V]
        gate = r[:, 2 * RET_QK + RET_V:2 * RET_QK + 2 * RET_V]
        steps = jnp.minimum(row + 1, valid).astype(f32)
        b = steps * lg_ref[...]
        _chunk_heads(q, k, v, b, gate, rn_ref, sr_scr.at[s], ob_ref.at[s], C)

    @pl.when(c == pl.num_programs(1) - 1)
    def _():
        sg_ref[...] = sg_scr[...]
        sr_ref[...] = sr_scr[...]


def _scan(gla, ret, wa2, ba, gn, rn, cos, sin, lg, sg0, sr0, B, S, C, valid):
    n = S // C
    G = SCAN_SEQS
    assert B % G == 0
    gla = gla.reshape(B, S, GLA_COLS)
    ret = ret.reshape(B, S, RET_COLS)
    tok = lambda b, c: (b, c, 0)
    fixed = lambda b, c: (0, 0)
    pos = lambda b, c: (c, 0)
    st = lambda b, c: (b, 0, 0)
    st_block = (G, GLA_HEADS * GLA_DK, GLA_DV)
    sg0 = sg0.reshape(B, GLA_HEADS * GLA_DK, GLA_DV)
    sr0 = sr0.reshape(B, RET_HEADS * RET_DK, RET_DV)
    oa, ob, sg, sr = pl.pallas_call(
        functools.partial(_scan_body, C=C, valid=valid),
        grid=(B // G, n),
        in_specs=[pl.BlockSpec((G, C, GLA_COLS), tok), pl.BlockSpec((G, C, RET_COLS), tok),
                  pl.BlockSpec((RANK_PAD, GLA_QK), fixed), pl.BlockSpec((1, GLA_QK), fixed),
                  pl.BlockSpec((GLA_HEADS, GLA_DV), fixed), pl.BlockSpec((RET_HEADS, RET_DV), fixed),
                  pl.BlockSpec((C, RET_QK), pos), pl.BlockSpec((C, RET_QK), pos),
                  pl.BlockSpec((1, RET_QK), fixed),
                  pl.BlockSpec(st_block, st), pl.BlockSpec(st_block, st)],
        out_specs=[pl.BlockSpec((G, C, GLA_V), tok), pl.BlockSpec((G, C, RET_V), tok),
                   pl.BlockSpec(st_block, st), pl.BlockSpec(st_block, st)],
        out_shape=[jax.ShapeDtypeStruct((B, S, GLA_V), f32), jax.ShapeDtypeStruct((B, S, RET_V), f32),
                   jax.ShapeDtypeStruct((B,) + st_block[1:], f32), jax.ShapeDtypeStruct((B,) + st_block[1:], f32)],
        scratch_shapes=[pltpu.VMEM(st_block, f32), pltpu.VMEM(st_block, f32)],
        compiler_params=_cparams(("parallel", "arbitrary")),
        name="scan",
    )(gla, ret, wa2, ba, gn, rn, cos, sin, lg, sg0, sr0)
    state_shape = (B, GLA_HEADS, GLA_DK, GLA_DV)
    return oa.reshape(B * S, GLA_V), ob.reshape(B * S, RET_V), sg.reshape(state_shape), sr.reshape(state_shape)


def _merge_body(x_ref, oa_ref, ob_ref, z_ref, wpa_ref, wpb_ref, wo_ref, gx_ref, wxq_ref, h_ref, q_ref):
    z = z_ref[...]
    merged = (jax.nn.sigmoid(z[:, :D_MODEL]) * _mm(oa_ref[...], wpa_ref[...])
              + jax.nn.sigmoid(z[:, D_MODEL:]) * _mm(ob_ref[...], wpb_ref[...]))
    h = x_ref[...] + _mm(merged, wo_ref[...])
    h_ref[...] = h
    q_ref[...] = _mm(_rms(h, gx_ref[...]), wxq_ref[...])


def _merge(x, oa, ob, z, wpa, wpb, wo, gx, wxq, tm=512):
    T = x.shape[0]
    row = lambda i: (i, 0)
    fixed = lambda i: (0, 0)
    return pl.pallas_call(
        _merge_body,
        grid=(T // tm,),
        in_specs=[pl.BlockSpec((tm, D_MODEL), row), pl.BlockSpec((tm, GLA_V), row), pl.BlockSpec((tm, RET_V), row),
                  pl.BlockSpec((tm, Z_COLS), row),
                  pl.BlockSpec((GLA_V, D_MODEL), fixed), pl.BlockSpec((RET_V, D_MODEL), fixed),
                  pl.BlockSpec((D_MODEL, D_MODEL), fixed), pl.BlockSpec((1, D_MODEL), fixed),
                  pl.BlockSpec((D_MODEL, D_MODEL), fixed)],
        out_specs=[pl.BlockSpec((tm, D_MODEL), row), pl.BlockSpec((tm, D_MODEL), row)],
        out_shape=[jax.ShapeDtypeStruct((T, D_MODEL), f32), jax.ShapeDtypeStruct((T, D_MODEL), f32)],
        compiler_params=_cparams(("parallel",)),
        name="merge",
    )(x, oa, ob, z, wpa, wpb, wo, gx, wxq)


def _mem_kv_body(m_ref, g_ref, wk_ref, wv_ref, k_ref, v_ref):
    mn = _rms(m_ref[...], g_ref[...]).astype(bf16)
    k_ref[...] = jnp.dot(mn, wk_ref[...], preferred_element_type=f32)
    v_ref[...] = jnp.dot(mn, wv_ref[...], preferred_element_type=f32)


def _mem_kv(mem, g, wk, wv, tm=256):
    T = mem.shape[0]
    row = lambda i: (i, 0)
    fixed = lambda i: (0, 0)
    return pl.pallas_call(
        _mem_kv_body,
        grid=(T // tm,),
        in_specs=[pl.BlockSpec((tm, D_MODEL), row), pl.BlockSpec((1, D_MODEL), fixed),
                  pl.BlockSpec((D_MODEL, D_MODEL), fixed), pl.BlockSpec((D_MODEL, D_MODEL), fixed)],
        out_specs=[pl.BlockSpec((tm, D_MODEL), row), pl.BlockSpec((tm, D_MODEL), row)],
        out_shape=[jax.ShapeDtypeStruct((T, D_MODEL), f32), jax.ShapeDtypeStruct((T, D_MODEL), f32)],
        compiler_params=_cparams(("parallel",)),
        name="mem_kv",
    )(mem, g, wk, wv)


def _xattn_body(q_ref, k_ref, v_ref, o_ref, *, seqs, tq):
    for s in range(seqs):
        rows = slice(s * tq, (s + 1) * tq)
        q = q_ref[rows, :]
        for h in range(XA_HEADS):
            hs = slice(h * XA_DH, (h + 1) * XA_DH)
            sc = _mm_nt(q[:, hs], k_ref[s, :, hs]) * XA_DH ** -0.5
            p = jnp.exp(sc - jnp.max(sc, axis=-1, keepdims=True))
            p = p / jnp.sum(p, axis=-1, keepdims=True)
            o_ref[rows, hs] = _mm(p, v_ref[s, :, hs])


def _xattn(q, mk, mv, B, S, tq, seqs):
    n = S // tq
    assert seqs == 1 or n == 1
    tok = lambda b, j: (b * n + j, 0)
    mem_spec = pl.BlockSpec((seqs, N_MEM, D_MODEL), lambda b, j: (b, 0, 0))
    return pl.pallas_call(
        functools.partial(_xattn_body, seqs=seqs, tq=tq),
        grid=(B // seqs, n),
        in_specs=[pl.BlockSpec((seqs * tq, D_MODEL), tok), mem_spec, mem_spec],
        out_specs=pl.BlockSpec((seqs * tq, D_MODEL), tok),
        out_shape=jax.ShapeDtypeStruct((B * S, D_MODEL), f32),
        compiler_params=_cparams(("parallel", "parallel")),
        name="xattn",
    )(q, mk, mv)


def _xattn_cache_body(q_ref, k_hbm, v_hbm, o_ref, kbuf, vbuf, sem, *, seqs, tq, layer):
    i = pl.program_id(0)
    n = pl.num_programs(0)

    def copies(step, slot):
        out = []
        for s in range(seqs):
            for h in range(XA_HEADS):
                hs = pl.ds(h * XA_DH, XA_DH)
                b = step * seqs + s
                out.append(pltpu.make_async_copy(k_hbm.at[layer, b, :, h, :], kbuf.at[slot, s, :, hs],
                                                 sem.at[slot, 0, s, h]))
                out.append(pltpu.make_async_copy(v_hbm.at[layer, b, :, h, :], vbuf.at[slot, s, :, hs],
                                                 sem.at[slot, 1, s, h]))
        return out

    @pl.when(i == 0)
    def _():
        for c in copies(0, 0):
            c.start()

    @pl.when(i + 1 < n)
    def _():
        for c in copies(i + 1, (i + 1) % 2):
            c.start()

    slot = i % 2
    for c in copies(i, slot):
        c.wait()
    lane_head = lax.broadcasted_iota(jnp.int32, (tq, D_MODEL), 1) // XA_DH
    for s in range(seqs):
        rows = slice(s * tq, (s + 1) * tq)
        q = q_ref[rows, :]
        stacked = jnp.concatenate([jnp.where(lane_head == h, q, 0.0) for h in range(XA_HEADS)], axis=0)
        sc = _mm_nt(stacked, kbuf[slot, s]) * XA_DH ** -0.5
        p = jnp.exp(sc - jnp.max(sc, axis=-1, keepdims=True))
        p = p / jnp.sum(p, axis=-1, keepdims=True)
        o = _mm(p, vbuf[slot, s])
        for h in range(XA_HEADS):
            hs = slice(h * XA_DH, (h + 1) * XA_DH)
            o_ref[rows, hs] = o[h * tq:(h + 1) * tq, hs]


def _xattn_cache(q, cache_k, cache_v, layer, B, tq, seqs):
    assert B % seqs == 0
    tok = lambda b: (b, 0)
    buf = pltpu.VMEM((2, seqs, N_MEM, D_MODEL), cache_k.dtype)
    return pl.pallas_call(
        functools.partial(_xattn_cache_body, seqs=seqs, tq=tq, layer=layer),
        grid=(B // seqs,),
        in_specs=[pl.BlockSpec((seqs * tq, D_MODEL), tok), pl.BlockSpec(memory_space=pl.ANY),
                  pl.BlockSpec(memory_space=pl.ANY)],
        out_specs=pl.BlockSpec((seqs * tq, D_MODEL), tok),
        out_shape=jax.ShapeDtypeStruct((B * tq, D_MODEL), f32),
        scratch_shapes=[buf, buf, pltpu.SemaphoreType.DMA((2, 2, seqs, XA_HEADS))],
        compiler_params=_cparams(("arbitrary",)),
        name="xattn_cache",
    )(q, cache_k, cache_v)


def _merge_exchange_network(n):
    pairs = []
    p = 1
    while p < n:
        k = p
        while k >= 1:
            for j in range(k % p, n - k, 2 * k):
                for i in range(min(k, n - j - k)):
                    if (i + j) // (2 * p) == (i + j + k) // (2 * p):
                        pairs.append((i + j, i + j + k))
            k //= 2
        p *= 2
    return pairs


def _compare_exchange(x, i, j):
    a, b = x[i], x[j]
    if b is None:
        return
    if a is None:
        x[i], x[j] = b, None
        return
    x[i], x[j] = jnp.maximum(a, b), jnp.minimum(a, b)


def _top16(tiles):
    n = PEER_TOPK
    x = list(tiles) + [None] * (n - len(tiles))
    for i, j in _merge_exchange_network(n):
        _compare_exchange(x, i, j)
    for shift in (4, 2, 1):
        merged = []
        for i in range(n):
            a, b = x[i], x[n - 1 - i]
            b = None if b is None else pltpu.roll(b, shift, 0)
            merged.append(b if a is None else (a if b is None else jnp.maximum(a, b)))
        x = merged
        d = n // 2
        while d >= 1:
            for i in range(n):
                if (i & d) == 0:
                    _compare_exchange(x, i, i + d)
            d //= 2
    return x


def _sublane_sum(x):
    for shift in (4, 2, 1):
        x = x + pltpu.roll(x, shift, 0)
    return x


def _rows_to_sublanes(v, sub):
    out = v[SUBLANES - 1]
    for r in range(SUBLANES - 2, -1, -1):
        out = jnp.where(sub == r, v[r], out)
    return out


def _peer_front_body(h_ref, o_ref, wxo_ref, gf_ref, wq_ref, sk_ref,
                     h2_ref, hn_ref, cn1_ref, e1_ref, rk2_ref, e2_ref, hn_scr, *, tm):
    hd = pl.program_id(1)

    @pl.when(hd == 0)
    def _():
        h2 = h_ref[...] + _mm(o_ref[...], wxo_ref[...])
        h2_ref[...] = h2
        hn = _rms(h2, gf_ref[...]).astype(bf16)
        hn_scr[...] = hn
        hn_ref[...] = hn

    pq = jnp.dot(hn_scr[...], wq_ref[...], preferred_element_type=f32)
    half = PEER_DQ // 2
    s1 = _mm_nt(sk_ref[0, 0], pq[:, :half])
    s2 = _mm_nt(sk_ref[0, 1], pq[:, half:])
    sub = lax.broadcasted_iota(jnp.int32, (SUBLANES, LANES), 0)
    for t in range(tm // LANES):
        ls = slice(t * LANES, (t + 1) * LANES)
        a1 = s1[:, ls]
        a2 = s2[:, ls]
        v1 = _top16([a1[SUBLANES * i:SUBLANES * (i + 1)] for i in range(PEER_NKEYS // SUBLANES)])
        v2 = _top16([a2[SUBLANES * i:SUBLANES * (i + 1)] for i in range(PEER_NKEYS // SUBLANES)])
        v1_hi = _rows_to_sublanes(v1[SUBLANES:], sub)
        v2_lo = _rows_to_sublanes(v2[:SUBLANES], sub)
        v2_hi = _rows_to_sublanes(v2[SUBLANES:], sub)
        cand = [v1[0] + v2_lo, v1[0] + v2_hi]
        cand += [v1[r] + v2_lo for r in range(1, SUBLANES)]
        cand += [v1_hi + v2[0]]
        top = _top16(cand)
        z = _sublane_sum(jnp.exp(_rows_to_sublanes(top[:SUBLANES], sub) - top[0])
                         + jnp.exp(_rows_to_sublanes(top[SUBLANES:], sub) - top[0]))
        tau = top[PEER_TOPK - 1]
        picked = lambda c: jnp.where(c >= tau, 1.0, 0.0)
        count = [_sublane_sum(picked(cand[0]) + picked(cand[1]))]
        count += [_sublane_sum(picked(cand[r + 1])) for r in range(1, SUBLANES)]
        count += [picked(v1[r] + v2[0]) for r in range(SUBLANES, PEER_TOPK)]
        count1 = []
        rank2 = []
        for i in range(PEER_NKEYS // SUBLANES):
            k1 = a1[SUBLANES * i:SUBLANES * (i + 1)]
            k2 = a2[SUBLANES * i:SUBLANES * (i + 1)]
            c1 = jnp.zeros_like(k1)
            r2 = jnp.full_like(k2, float(PEER_TOPK))
            for r in range(PEER_TOPK - 1, -1, -1):
                c1 = jnp.where(k1 == v1[r], count[r], c1)
                r2 = jnp.where(k2 == v2[r], float(r), r2)
            count1.append(c1)
            rank2.append(r2)
        cn1_ref[0, :, ls] = jnp.concatenate(count1, axis=0)
        e1_ref[0, :, ls] = jnp.exp(a1 - v1[0][0:1]) / (2.0 * z[0:1])
        rk2_ref[0, :, ls] = jnp.concatenate(rank2, axis=0).astype(bf16)
        e2_ref[0, :, ls] = jnp.exp(a2 - v2[0][0:1]).astype(bf16)


def _peer_front(h, o, wxo, gf, wq, sk):
    T = h.shape[0]
    tm = min(T, PEER_FRONT_TOKENS)
    assert T % tm == 0
    row = lambda i, hd: (i, 0)
    fixed = lambda i, hd: (0, 0)
    head_t = lambda i, hd: (hd, 0, i)
    words = jax.ShapeDtypeStruct((PEER_HEADS, PEER_NKEYS, T), f32)
    halfs = jax.ShapeDtypeStruct((PEER_HEADS, PEER_NKEYS, T), bf16)
    sc_spec = pl.BlockSpec((1, PEER_NKEYS, tm), head_t)
    return pl.pallas_call(
        functools.partial(_peer_front_body, tm=tm),
        grid=(T // tm, PEER_HEADS),
        in_specs=[pl.BlockSpec((tm, D_MODEL), row), pl.BlockSpec((tm, D_MODEL), row),
                  pl.BlockSpec((D_MODEL, D_MODEL), fixed), pl.BlockSpec((1, D_MODEL), fixed),
                  pl.BlockSpec((D_MODEL, PEER_DQ), lambda i, hd: (0, hd)),
                  pl.BlockSpec((1, 2, PEER_NKEYS, PEER_DQ // 2), lambda i, hd: (hd, 0, 0, 0))],
        out_specs=[pl.BlockSpec((tm, D_MODEL), row), pl.BlockSpec((tm, D_MODEL), row),
                   sc_spec, sc_spec, sc_spec, sc_spec],
        out_shape=[jax.ShapeDtypeStruct((T, D_MODEL), f32), jax.ShapeDtypeStruct((T, D_MODEL), bf16),
                   words, words, halfs, halfs],
        scratch_shapes=[pltpu.VMEM((tm, D_MODEL), bf16)],
        compiler_params=_cparams(("parallel", "arbitrary")),
        name="peer_front",
    )(h, o, wxo, gf, wq, sk)


PEER_ROWS_PER_BLOCK = SUBLANES
PEER_BLOCK = PEER_ROWS_PER_BLOCK * PEER_NKEYS
GATE_ROWS = 8 * SUBLANES
PEER_DENSE_TOKENS = 1024


def _peer_dense_body(hn_ref, u_ref, vt_prev_ref, vt_last_ref, cn1_ref, e1_ref, rk2_ref, e2_ref, h2_ref, gfin_ref,
                     y_ref, act_scr, w_scr, acc_scr, *, tm):
    j = pl.program_id(1)

    @pl.when(j == 0)
    def _():
        acc_scr[...] = jnp.zeros_like(acc_scr)
        w_scr[...] = jnp.zeros_like(w_scr)

    a = lax.dot_general(u_ref[...].astype(bf16), hn_ref[...], (((1,), (1,)), ((), ())), preferred_element_type=f32)
    act_scr[...] = (a * (1.0 + lax.erf(a * math.sqrt(0.5)))).astype(bf16)
    acc_scr[...] += jnp.dot(vt_prev_ref[...], w_scr[...], preferred_element_type=f32)
    chunks = PEER_NKEYS // GATE_ROWS

    def row_tile(ref, hd, r, ls):
        return jnp.broadcast_to(ref[hd, r:r + 1, ls].astype(bf16), (GATE_ROWS, LANES))

    def gate_tile(i, carry):
        ls = pl.ds(pl.multiple_of((i // chunks) * LANES, LANES), LANES)
        c0 = pl.multiple_of((i % chunks) * GATE_ROWS, GATE_ROWS)
        gates = [None] * PEER_ROWS_PER_BLOCK
        for hd in range(PEER_HEADS):
            rk2 = rk2_ref[hd, pl.ds(c0, GATE_ROWS), ls]
            e2 = e2_ref[hd, pl.ds(c0, GATE_ROWS), ls]
            for r in range(PEER_ROWS_PER_BLOCK):
                room = jnp.maximum(row_tile(cn1_ref, hd, r, ls) - rk2, 0.0)
                term = jnp.minimum(row_tile(e1_ref, hd, r, ls) * e2, room)
                gates[r] = term if gates[r] is None else gates[r] + term
        for r in range(PEER_ROWS_PER_BLOCK):
            rs = pl.ds(pl.multiple_of(r * PEER_NKEYS + c0, GATE_ROWS), GATE_ROWS)
            w_scr[rs, ls] = act_scr[rs, ls] * gates[r]
        return carry

    lax.fori_loop(0, (tm // LANES) * chunks, gate_tile, 0)

    @pl.when(j == pl.num_programs(1) - 1)
    def _():
        out_t = acc_scr[...] + jnp.dot(vt_last_ref[...], w_scr[...], preferred_element_type=f32)
        y_ref[...] = _rms(h2_ref[...] + out_t.T, gfin_ref[...])


def _peer_dense(hn, u, vt, cn1, e1, rk2, e2, h2, gfin):
    T = hn.shape[0]
    nb = PEER_BLOCK
    tm = min(T, PEER_DENSE_TOKENS)
    assert T % tm == 0
    row = lambda i, j: (i, 0)
    once = pl.Buffered(1)
    sc1 = pl.BlockSpec((PEER_HEADS, PEER_ROWS_PER_BLOCK, tm), lambda i, j: (0, j, i))
    sc2 = pl.BlockSpec((PEER_HEADS, PEER_NKEYS, tm), lambda i, j: (0, 0, i), pipeline_mode=once)
    return pl.pallas_call(
        functools.partial(_peer_dense_body, tm=tm),
        grid=(T // tm, PEER_N // nb),
        in_specs=[pl.BlockSpec((tm, D_MODEL), row),
                  pl.BlockSpec((nb, D_MODEL), lambda i, j: (j, 0)),
                  pl.BlockSpec((D_MODEL, nb), lambda i, j: (0, jnp.maximum(j - 1, 0))),
                  pl.BlockSpec((D_MODEL, nb), lambda i, j: (0, PEER_N // nb - 1), pipeline_mode=once),
                  sc1, sc1, sc2, sc2,
                  pl.BlockSpec((tm, D_MODEL), row, pipeline_mode=once),
                  pl.BlockSpec((1, D_MODEL), lambda i, j: (0, 0))],
        out_specs=pl.BlockSpec((tm, D_MODEL), row),
        out_shape=jax.ShapeDtypeStruct((T, D_MODEL), f32),
        scratch_shapes=[pltpu.VMEM((nb, tm), bf16), pltpu.VMEM((nb, tm), bf16), pltpu.VMEM((D_MODEL, tm), f32)],
        compiler_params=_cparams(("parallel", "arbitrary")),
        name="peer_dense",
    )(hn, u, vt, vt, cn1, e1, rk2, e2, h2, gfin)


def _rotary_tables(pos):
    half = RET_DK // 2
    inv = ROPE_BASE ** (-jnp.arange(half, dtype=f32) / half)
    ang = pos.astype(f32)[:, None] * inv[None, :]
    cos = jnp.cos(ang)
    sin = jnp.sin(ang)
    cos = jnp.tile(jnp.concatenate([cos, cos], axis=-1), (1, RET_HEADS))
    sin = jnp.tile(jnp.concatenate([-sin, sin], axis=-1), (1, RET_HEADS))
    return cos, sin


def _layer_weights(l, norm_mix, w_in, w_a2, b_a, gla_head_norm, ret_head_norm, w_pa, w_pb, w_o, norm_xattn,
                   norm_mem, w_xq, w_xk, w_xv, w_xo, norm_ffn, peer_wq, peer_subkeys, peer_u, peer_v):
    offs = [0]
    for s in IN_SIZES:
        offs.append(offs[-1] + s)
    cols = [w_in[l][:, offs[i]:offs[i + 1]] for i in range(len(IN_SIZES))]
    gq, gk, gv, glr, gr, rq, rk, rv, rg, za, zb = cols
    glr = jnp.pad(glr, ((0, 0), (0, RANK_PAD - GLA_RANK)))
    half = RET_DK // 2
    j = jnp.arange(RET_QK)
    partner = jnp.where((j % RET_DK) < half, j + half, j - half)
    w = {}
    w["gla"] = jnp.concatenate([gq, gk, gv, glr, gr], axis=1).astype(bf16)
    w["ret"] = jnp.concatenate([rq, rk, rv, rg, rq[:, partner], rk[:, partner]], axis=1).astype(bf16)
    w["z"] = jnp.concatenate([za, zb], axis=1).astype(bf16)
    w["a2"] = jnp.pad(w_a2[l], ((0, RANK_PAD - GLA_RANK), (0, 0))).astype(bf16)
    w["ba"] = b_a[l].reshape(1, GLA_QK)
    w["gn"] = gla_head_norm[l]
    w["rn"] = ret_head_norm[l]
    log_gamma = jnp.log1p(-(2.0 ** (-5.0 - jnp.arange(RET_HEADS, dtype=f32))))
    w["lg"] = jnp.repeat(log_gamma, RET_DK).reshape(1, RET_QK)
    w["norm_mix"] = norm_mix[l].reshape(1, D_MODEL)
    w["pa"] = w_pa[l].astype(bf16)
    w["pb"] = w_pb[l].astype(bf16)
    w["o"] = w_o[l].astype(bf16)
    w["norm_xattn"] = norm_xattn[l].reshape(1, D_MODEL)
    w["norm_mem"] = norm_mem[l].reshape(1, D_MODEL)
    w["xq"] = w_xq[l].astype(bf16)
    w["xk"] = w_xk[l].astype(bf16)
    w["xv"] = w_xv[l].astype(bf16)
    w["xo"] = w_xo[l].astype(bf16)
    w["norm_ffn"] = norm_ffn[l].reshape(1, D_MODEL)
    w["wq"] = peer_wq[l].astype(bf16)
    w["sk"] = peer_subkeys[l].astype(bf16)
    w["u"] = peer_u[l]
    w["vt"] = peer_v[l].T.astype(bf16)
    return w


def _layer(x, B, S, keep, C, valid, cos, sin, xattn, sg0, sr0, w, gfin):
    gla, ret, z = _in_proj(x, w["norm_mix"], w["gla"], w["ret"], w["z"])
    oa, ob, sg, sr = _scan(gla, ret, w["a2"], w["ba"], w["gn"], w["rn"], cos, sin, w["lg"], sg0, sr0, B, S, C, valid)
    h, q = _merge(x, oa, ob, z, w["pa"], w["pb"], w["o"], w["norm_xattn"], w["xq"])
    o = xattn(q)
    if keep < S:
        h = h.reshape(B, S, D_MODEL)[:, :keep].reshape(B * keep, D_MODEL)
        o = o.reshape(B, S, D_MODEL)[:, :keep].reshape(B * keep, D_MODEL)
    h2, hn, cn1, e1, rk2, e2 = _peer_front(h, o, w["xo"], w["norm_ffn"], w["wq"], w["sk"])
    y = _peer_dense(hn, w["u"], w["vt"], cn1, e1, rk2, e2, h2, gfin)
    return y, sg, sr


def kernel(x_prompt, x_sample, mem_prompt, state_gla, state_ret, cache_mem_k, cache_mem_v, norm_mix, w_in, w_a2,
           b_a, gla_head_norm, ret_head_norm, w_pa, w_pb, w_o, norm_xattn, norm_mem, w_xq, w_xk, w_xv, w_xo,
           norm_ffn, peer_wq, peer_subkeys, peer_u, peer_v, norm_final):
    depth = w_in.shape[0]
    assert depth == 1, "the final norm is fused into the layer's last kernel"
    Bp, Sp, _ = x_prompt.shape
    Bs, Ss, _ = x_sample.shape
    l = 0
    w = _layer_weights(l, norm_mix, w_in, w_a2, b_a, gla_head_norm, ret_head_norm, w_pa, w_pb, w_o, norm_xattn,
                       norm_mem, w_xq, w_xk, w_xv, w_xo, norm_ffn, peer_wq, peer_subkeys, peer_u, peer_v)
    gfin = norm_final.reshape(1, D_MODEL)

    cos_p, sin_p = _rotary_tables(jnp.arange(Sp, dtype=jnp.int32))
    mk, mv = _mem_kv(mem_prompt.reshape(Bp * N_MEM, D_MODEL), w["norm_mem"], w["xk"], w["xv"])
    zeros = jnp.zeros((Bp, GLA_HEADS, GLA_DK, GLA_DV), f32)
    xattn_p = functools.partial(_xattn, mk=mk.reshape(Bp, N_MEM, D_MODEL), mv=mv.reshape(Bp, N_MEM, D_MODEL),
                                B=Bp, S=Sp, tq=XATTN_ROWS, seqs=1)
    yp, sgp, srp = _layer(x_prompt.reshape(Bp * Sp, D_MODEL), Bp, Sp, Sp, SCAN_CHUNK, SCAN_CHUNK, cos_p, sin_p,
                          xattn_p, zeros, zeros, w, gfin)

    pad = SAMPLE_PAD - Ss
    xs = jnp.pad(x_sample, ((0, 0), (0, pad), (0, 0))).reshape(Bs * SAMPLE_PAD, D_MODEL)
    cos_s, sin_s = _rotary_tables(PAST_LEN + jnp.arange(SAMPLE_PAD, dtype=jnp.int32))
    xattn_s = functools.partial(_xattn_cache, cache_k=cache_mem_k, cache_v=cache_mem_v, layer=l, B=Bs,
                                tq=SAMPLE_PAD, seqs=XATTN_SEQS)
    ys, sgs, srs = _layer(xs, Bs, SAMPLE_PAD, Ss, SAMPLE_PAD, Ss, cos_s, sin_s,
                          xattn_s, state_gla[l], state_ret[l], w, gfin)
    ys = ys.reshape(Bs, Ss, D_MODEL)

    kv_shape = (1, Bp, N_MEM, XA_HEADS, XA_DH)
    return (yp.reshape(Bp, Sp, D_MODEL), ys, sgp[None], srp[None], mk.reshape(kv_shape), mv.reshape(kv_shape),
            sgs[None], srs[None])
```

```python
import functools
import math

import jax
import jax.numpy as jnp
from jax import lax
from jax.experimental import pallas as pl
from jax.experimental.pallas import tpu as pltpu

f32 = jnp.float32
bf16 = jnp.bfloat16

D_MODEL = 1024
PAST_LEN = 16384
GLA_HEADS, GLA_DK, GLA_DV, GLA_RANK, GLA_TEMP = 4, 64, 128, 16, 16.0
RET_HEADS, RET_DK, RET_DV = 4, 64, 128
ROPE_BASE = 10000.0
N_MEM = 256
XA_HEADS = 4
XA_DH = D_MODEL // XA_HEADS
PEER_HEADS, PEER_NKEYS, PEER_DQ, PEER_TOPK = 8, 128, 256, 16
PEER_N = PEER_NKEYS * PEER_NKEYS
EPS = 1e-6

GLA_QK = GLA_HEADS * GLA_DK
GLA_V = GLA_HEADS * GLA_DV
RET_QK = RET_HEADS * RET_DK
RET_V = RET_HEADS * RET_DV
IN_SIZES = (GLA_QK, GLA_QK, GLA_V, GLA_RANK, GLA_V, RET_QK, RET_QK, RET_V, RET_V, D_MODEL, D_MODEL)

LANES = 128
SUBLANES = 8
RANK_PAD = LANES
GLA_COLS = 2 * GLA_QK + GLA_V + RANK_PAD + GLA_V
RET_COLS = 2 * RET_QK + 2 * RET_V + 2 * RET_QK
Z_COLS = 2 * D_MODEL
SCAN_CHUNK = 64
SCAN_SEQS = 4
SAMPLE_PAD = SUBLANES
XATTN_ROWS = 1024
PEER_FRONT_TOKENS = 1024
XATTN_SEQS = 8
VMEM_LIMIT = 52 * 1024 * 1024


def _cparams(sem):
    return pltpu.CompilerParams(dimension_semantics=sem, vmem_limit_bytes=VMEM_LIMIT)


def _rms(x, g):
    return x * lax.rsqrt(jnp.mean(x * x, axis=-1, keepdims=True) + EPS) * g


def _mm(a, b):
    return jnp.dot(a.astype(bf16), b.astype(bf16), preferred_element_type=f32)


def _mm_nt(a, b):
    return lax.dot_general(a.astype(bf16), b.astype(bf16), (((1,), (1,)), ((), ())), preferred_element_type=f32)


def _mm_tn(a, b):
    return lax.dot_general(a.astype(bf16), b.astype(bf16), (((0,), (0,)), ((), ())), preferred_element_type=f32)


def _in_proj_body(x_ref, g_ref, wg_ref, wr_ref, wz_ref, og_ref, or_ref, oz_ref):
    xn = _rms(x_ref[...], g_ref[...]).astype(bf16)
    og_ref[...] = jnp.dot(xn, wg_ref[...], preferred_element_type=f32)
    or_ref[...] = jnp.dot(xn, wr_ref[...], preferred_element_type=f32)
    oz_ref[...] = jnp.dot(xn, wz_ref[...], preferred_element_type=f32)


def _in_proj(x, g, wg, wr, wz, tm=256):
    T = x.shape[0]
    row = lambda i: (i, 0)
    fixed = lambda i: (0, 0)
    return pl.pallas_call(
        _in_proj_body,
        grid=(T // tm,),
        in_specs=[pl.BlockSpec((tm, D_MODEL), row), pl.BlockSpec((1, D_MODEL), fixed),
                  pl.BlockSpec((D_MODEL, GLA_COLS), fixed), pl.BlockSpec((D_MODEL, RET_COLS), fixed),
                  pl.BlockSpec((D_MODEL, Z_COLS), fixed)],
        out_specs=[pl.BlockSpec((tm, GLA_COLS), row), pl.BlockSpec((tm, RET_COLS), row),
                   pl.BlockSpec((tm, Z_COLS), row)],
        out_shape=[jax.ShapeDtypeStruct((T, GLA_COLS), f32), jax.ShapeDtypeStruct((T, RET_COLS), f32),
                   jax.ShapeDtypeStruct((T, Z_COLS), f32)],
        compiler_params=_cparams(("parallel",)),
        name="in_proj",
    )(x, g, wg, wr, wz)


def _chunk_heads(q, k, v, b, gate, hnorm_ref, state_ref, o_ref, C):
    mid = C // 2 - 1
    b_mid = b[mid:mid + 1, :]
    b_last = b[C - 1:C, :]
    q_in = q * jnp.exp(b)
    q_e = q * jnp.exp(b - b_mid)
    k_e = k * jnp.exp(b_mid - b)
    k_d = k * jnp.exp(b_last - b)
    decay_col = jnp.exp(jnp.broadcast_to(b_last, (SUBLANES, b.shape[1])).T[:, 0:1])
    lane_head = lax.broadcasted_iota(jnp.int32, (C, 4 * 64), 1) // 64
    stack = lambda x: jnp.concatenate([jnp.where(lane_head == h, x, 0.0) for h in range(4)], axis=0)
    rows = lax.broadcasted_iota(jnp.int32, (4 * C, 4 * C), 0) % C
    cols = lax.broadcasted_iota(jnp.int32, (4 * C, 4 * C), 1) % C
    state = state_ref[...]
    v_rows = jnp.concatenate([v[:, h * 128:(h + 1) * 128] for h in range(4)], axis=0)
    att = jnp.where(rows >= cols, _mm_nt(stack(q_e), stack(k_e)), 0.0)
    o = _mm(jnp.concatenate([stack(q_in), att], axis=1), jnp.concatenate([state, v_rows], axis=0))
    state_ref[...] = decay_col * state + _mm_tn(stack(k_d), v_rows)
    for h in range(4):
        vs = slice(h * 128, (h + 1) * 128)
        g_h = gate[:, vs]
        o_ref[:, vs] = _rms(o[h * C:(h + 1) * C, :], hnorm_ref[h:h + 1, :]) * (g_h * jax.nn.sigmoid(g_h))


def _scan_body(gla_ref, ret_ref, wa2_ref, ba_ref, gn_ref, rn_ref, cos_ref, sin_ref, lg_ref, sg0_ref, sr0_ref,
               oa_ref, ob_ref, sg_ref, sr_ref, sg_scr, sr_scr, *, C, valid):
    c = pl.program_id(1)

    @pl.when(c == 0)
    def _():
        sg_scr[...] = sg0_ref[...]
        sr_scr[...] = sr0_ref[...]

    row = lax.broadcasted_iota(jnp.int32, (C, 1), 0)
    tri = (lax.broadcasted_iota(jnp.int32, (C, C), 0) >= lax.broadcasted_iota(jnp.int32, (C, C), 1)).astype(bf16)
    cos = cos_ref[...]
    sin = sin_ref[...]

    for s in range(SCAN_SEQS):
        g = gla_ref[s]
        q = g[:, 0:GLA_QK] * GLA_DK ** -0.5
        k = g[:, GLA_QK:2 * GLA_QK]
        v = g[:, 2 * GLA_QK:2 * GLA_QK + GLA_V]
        lowrank = g[:, 2 * GLA_QK + GLA_V:2 * GLA_QK + GLA_V + RANK_PAD]
        gate = g[:, 2 * GLA_QK + GLA_V + RANK_PAD:]
        log_a = jax.nn.log_sigmoid(_mm(lowrank, wa2_ref[...]) + ba_ref[...]) / GLA_TEMP
        if valid < C:
            log_a = jnp.where(row < valid, log_a, 0.0)
        hi = log_a.astype(bf16)
        rest = log_a - hi.astype(f32)
        mid = rest.astype(bf16)
        lo = (rest - mid.astype(f32)).astype(bf16)
        parts = jnp.dot(tri, jnp.concatenate([hi, mid, lo], axis=1), preferred_element_type=f32)
        b = parts[:, 0:GLA_QK] + parts[:, GLA_QK:2 * GLA_QK] + parts[:, 2 * GLA_QK:]
        _chunk_heads(q, k, v, b, gate, gn_ref, sg_scr.at[s], oa_ref.at[s], C)

        r = ret_ref[s]
        q = r[:, 0:RET_QK] * cos + r[:, 2 * RET_QK + 2 * RET_V:3 * RET_QK + 2 * RET_V] * sin
        k = (r[:, RET_QK:2 * RET_QK] * cos + r[:, 3 * RET_QK + 2 * RET_V:] * sin) * RET_DK ** -0.5
        v = r[:, 2 * RET_QK:2 * RET_QK + RET_V]
        gate = r[:, 2 * RET_QK + RET_V:2 * RET_QK + 2 * RET_V]
        steps = jnp.minimum(row + 1, valid).astype(f32)
        b = steps * lg_ref[...]
        _chunk_heads(q, k, v, b, gate, rn_ref, sr_scr.at[s], ob_ref.at[s], C)

    @pl.when(c == pl.num_programs(1) - 1)
    def _():
        sg_ref[...] = sg_scr[...]
        sr_ref[...] = sr_scr[...]


def _scan(gla, ret, wa2, ba, gn, rn, cos, sin, lg, sg0, sr0, B, S, C, valid):
    n = S // C
    G = SCAN_SEQS
    assert B % G == 0
    gla = gla.reshape(B, S, GLA_COLS)
    ret = ret.reshape(B, S, RET_COLS)
    tok = lambda b, c: (b, c, 0)
    fixed = lambda b, c: (0, 0)
    pos = lambda b, c: (c, 0)
    st = lambda b, c: (b, 0, 0)
    st_block = (G, GLA_HEADS * GLA_DK, GLA_DV)
    sg0 = sg0.reshape(B, GLA_HEADS * GLA_DK, GLA_DV)
    sr0 = sr0.reshape(B, RET_HEADS * RET_DK, RET_DV)
    oa, ob, sg, sr = pl.pallas_call(
        functools.partial(_scan_body, C=C, valid=valid),
        grid=(B // G, n),
        in_specs=[pl.BlockSpec((G, C, GLA_COLS), tok), pl.BlockSpec((G, C, RET_COLS), tok),
                  pl.BlockSpec((RANK_PAD, GLA_QK), fixed), pl.BlockSpec((1, GLA_QK), fixed),
                  pl.BlockSpec((GLA_HEADS, GLA_DV), fixed), pl.BlockSpec((RET_HEADS, RET_DV), fixed),
                  pl.BlockSpec((C, RET_QK), pos), pl.BlockSpec((C, RET_QK), pos),
                  pl.BlockSpec((1, RET_QK), fixed),
                  pl.BlockSpec(st_block, st), pl.BlockSpec(st_block, st)],
        out_specs=[pl.BlockSpec((G, C, GLA_V), tok), pl.BlockSpec((G, C, RET_V), tok),
                   pl.BlockSpec(st_block, st), pl.BlockSpec(st_block, st)],
        out_shape=[jax.ShapeDtypeStruct((B, S, GLA_V), f32), jax.ShapeDtypeStruct((B, S, RET_V), f32),
                   jax.ShapeDtypeStruct((B,) + st_block[1:], f32), jax.ShapeDtypeStruct((B,) + st_block[1:], f32)],
        scratch_shapes=[pltpu.VMEM(st_block, f32), pltpu.VMEM(st_block, f32)],
        compiler_params=_cparams(("parallel", "arbitrary")),
        name="scan",
    )(gla, ret, wa2, ba, gn, rn, cos, sin, lg, sg0, sr0)
    state_shape = (B, GLA_HEADS, GLA_DK, GLA_DV)
    return oa.reshape(B * S, GLA_V), ob.reshape(B * S, RET_V), sg.reshape(state_shape), sr.reshape(state_shape)


def _merge_body(x_ref, oa_ref, ob_ref, z_ref, wpa_ref, wpb_ref, wo_ref, gx_ref, wxq_ref, h_ref, q_ref):
    z = z_ref[...]
    merged = (jax.nn.sigmoid(z[:, :D_MODEL]) * _mm(oa_ref[...], wpa_ref[...])
              + jax.nn.sigmoid(z[:, D_MODEL:]) * _mm(ob_ref[...], wpb_ref[...]))
    h = x_ref[...] + _mm(merged, wo_ref[...])
    h_ref[...] = h
    q_ref[...] = _mm(_rms(h, gx_ref[...]), wxq_ref[...])


def _merge(x, oa, ob, z, wpa, wpb, wo, gx, wxq, tm=512):
    T = x.shape[0]
    row = lambda i: (i, 0)
    fixed = lambda i: (0, 0)
    return pl.pallas_call(
        _merge_body,
        grid=(T // tm,),
        in_specs=[pl.BlockSpec((tm, D_MODEL), row), pl.BlockSpec((tm, GLA_V), row), pl.BlockSpec((tm, RET_V), row),
                  pl.BlockSpec((tm, Z_COLS), row),
                  pl.BlockSpec((GLA_V, D_MODEL), fixed), pl.BlockSpec((RET_V, D_MODEL), fixed),
                  pl.BlockSpec((D_MODEL, D_MODEL), fixed), pl.BlockSpec((1, D_MODEL), fixed),
                  pl.BlockSpec((D_MODEL, D_MODEL), fixed)],
        out_specs=[pl.BlockSpec((tm, D_MODEL), row), pl.BlockSpec((tm, D_MODEL), row)],
        out_shape=[jax.ShapeDtypeStruct((T, D_MODEL), f32), jax.ShapeDtypeStruct((T, D_MODEL), f32)],
        compiler_params=_cparams(("parallel",)),
        name="merge",
    )(x, oa, ob, z, wpa, wpb, wo, gx, wxq)


def _mem_kv_body(m_ref, g_ref, wk_ref, wv_ref, k_ref, v_ref):
    mn = _rms(m_ref[...], g_ref[...]).astype(bf16)
    k_ref[...] = jnp.dot(mn, wk_ref[...], preferred_element_type=f32)
    v_ref[...] = jnp.dot(mn, wv_ref[...], preferred_element_type=f32)


def _mem_kv(mem, g, wk, wv, tm=256):
    T = mem.shape[0]
    row = lambda i: (i, 0)
    fixed = lambda i: (0, 0)
    return pl.pallas_call(
        _mem_kv_body,
        grid=(T // tm,),
        in_specs=[pl.BlockSpec((tm, D_MODEL), row), pl.BlockSpec((1, D_MODEL), fixed),
                  pl.BlockSpec((D_MODEL, D_MODEL), fixed), pl.BlockSpec((D_MODEL, D_MODEL), fixed)],
        out_specs=[pl.BlockSpec((tm, D_MODEL), row), pl.BlockSpec((tm, D_MODEL), row)],
        out_shape=[jax.ShapeDtypeStruct((T, D_MODEL), f32), jax.ShapeDtypeStruct((T, D_MODEL), f32)],
        compiler_params=_cparams(("parallel",)),
        name="mem_kv",
    )(mem, g, wk, wv)


def _xattn_body(q_ref, k_ref, v_ref, o_ref, *, seqs, tq):
    for s in range(seqs):
        rows = slice(s * tq, (s + 1) * tq)
        q = q_ref[rows, :]
        for h in range(XA_HEADS):
            hs = slice(h * XA_DH, (h + 1) * XA_DH)
            sc = _mm_nt(q[:, hs], k_ref[s, :, hs]) * XA_DH ** -0.5
            p = jnp.exp(sc - jnp.max(sc, axis=-1, keepdims=True))
            p = p / jnp.sum(p, axis=-1, keepdims=True)
            o_ref[rows, hs] = _mm(p, v_ref[s, :, hs])


def _xattn(q, mk, mv, B, S, tq, seqs):
    n = S // tq
    assert seqs == 1 or n == 1
    tok = lambda b, j: (b * n + j, 0)
    mem_spec = pl.BlockSpec((seqs, N_MEM, D_MODEL), lambda b, j: (b, 0, 0))
    return pl.pallas_call(
        functools.partial(_xattn_body, seqs=seqs, tq=tq),
        grid=(B // seqs, n),
        in_specs=[pl.BlockSpec((seqs * tq, D_MODEL), tok), mem_spec, mem_spec],
        out_specs=pl.BlockSpec((seqs * tq, D_MODEL), tok),
        out_shape=jax.ShapeDtypeStruct((B * S, D_MODEL), f32),
        compiler_params=_cparams(("parallel", "parallel")),
        name="xattn",
    )(q, mk, mv)


def _xattn_cache_body(q_ref, k_hbm, v_hbm, o_ref, kbuf, vbuf, sem, *, seqs, tq, layer):
    i = pl.program_id(0)
    n = pl.num_programs(0)

    def copies(step, slot):
        out = []
        for s in range(seqs):
            for h in range(XA_HEADS):
                hs = pl.ds(h * XA_DH, XA_DH)
                b = step * seqs + s
                out.append(pltpu.make_async_copy(k_hbm.at[layer, b, :, h, :], kbuf.at[slot, s, :, hs],
                                                 sem.at[slot, 0, s, h]))
                out.append(pltpu.make_async_copy(v_hbm.at[layer, b, :, h, :], vbuf.at[slot, s, :, hs],
                                                 sem.at[slot, 1, s, h]))
        return out

    @pl.when(i == 0)
    def _():
        for c in copies(0, 0):
            c.start()

    @pl.when(i + 1 < n)
    def _():
        for c in copies(i + 1, (i + 1) % 2):
            c.start()

    slot = i % 2
    for c in copies(i, slot):
        c.wait()
    lane_head = lax.broadcasted_iota(jnp.int32, (tq, D_MODEL), 1) // XA_DH
    for s in range(seqs):
        rows = slice(s * tq, (s + 1) * tq)
        q = q_ref[rows, :]
        stacked = jnp.concatenate([jnp.where(lane_head == h, q, 0.0) for h in range(XA_HEADS)], axis=0)
        sc = _mm_nt(stacked, kbuf[slot, s]) * XA_DH ** -0.5
        p = jnp.exp(sc - jnp.max(sc, axis=-1, keepdims=True))
        p = p / jnp.sum(p, axis=-1, keepdims=True)
        o = _mm(p, vbuf[slot, s])
        for h in range(XA_HEADS):
            hs = slice(h * XA_DH, (h + 1) * XA_DH)
            o_ref[rows, hs] = o[h * tq:(h + 1) * tq, hs]


def _xattn_cache(q, cache_k, cache_v, layer, B, tq, seqs):
    assert B % seqs == 0
    tok = lambda b: (b, 0)
    buf = pltpu.VMEM((2, seqs, N_MEM, D_MODEL), cache_k.dtype)
    return pl.pallas_call(
        functools.partial(_xattn_cache_body, seqs=seqs, tq=tq, layer=layer),
        grid=(B // seqs,),
        in_specs=[pl.BlockSpec((seqs * tq, D_MODEL), tok), pl.BlockSpec(memory_space=pl.ANY),
                  pl.BlockSpec(memory_space=pl.ANY)],
        out_specs=pl.BlockSpec((seqs * tq, D_MODEL), tok),
        out_shape=jax.ShapeDtypeStruct((B * tq, D_MODEL), f32),
        scratch_shapes=[buf, buf, pltpu.SemaphoreType.DMA((2, 2, seqs, XA_HEADS))],
        compiler_params=_cparams(("arbitrary",)),
        name="xattn_cache",
    )(q, cache_k, cache_v)


def _merge_exchange_network(n):
    pairs = []
    p = 1
    while p < n:
        k = p
        while k >= 1:
            for j in range(k % p, n - k, 2 * k):
                for i in range(min(k, n - j - k)):
                    if (i + j) // (2 * p) == (i + j + k) // (2 * p):
                        pairs.append((i + j, i + j + k))
            k //= 2
        p *= 2
    return pairs


def _compare_exchange(x, i, j):
    a, b = x[i], x[j]
    if b is None:
        return
    if a is None:
        x[i], x[j] = b, None
        return
    x[i], x[j] = jnp.maximum(a, b), jnp.minimum(a, b)


def _top16(tiles):
    n = PEER_TOPK
    x = list(tiles) + [None] * (n - len(tiles))
    for i, j in _merge_exchange_network(n):
        _compare_exchange(x, i, j)
    for shift in (4, 2, 1):
        merged = []
        for i in range(n):
            a, b = x[i], x[n - 1 - i]
            b = None if b is None else pltpu.roll(b, shift, 0)
            merged.append(b if a is None else (a if b is None else jnp.maximum(a, b)))
        x = merged
        d = n // 2
        while d >= 1:
            for i in range(n):
                if (i & d) == 0:
                    _compare_exchange(x, i, i + d)
            d //= 2
    return x


def _sublane_sum(x):
    for shift in (4, 2, 1):
        x = x + pltpu.roll(x, shift, 0)
    return x


def _rows_to_sublanes(v, sub):
    out = v[SUBLANES - 1]
    for r in range(SUBLANES - 2, -1, -1):
        out = jnp.where(sub == r, v[r], out)
    return out


def _peer_front_body(h_ref, o_ref, wxo_ref, gf_ref, wq_ref, sk_ref,
                     h2_ref, hn_ref, cn1_ref, e1_ref, rk2_ref, e2_ref, hn_scr, *, tm):
    hd = pl.program_id(1)

    @pl.when(hd == 0)
    def _():
        h2 = h_ref[...] + _mm(o_ref[...], wxo_ref[...])
        h2_ref[...] = h2
        hn = _rms(h2, gf_ref[...]).astype(bf16)
        hn_scr[...] = hn
        hn_ref[...] = hn

    pq = jnp.dot(hn_scr[...], wq_ref[...], preferred_element_type=f32)
    half = PEER_DQ // 2
    s1 = _mm_nt(sk_ref[0, 0], pq[:, :half])
    s2 = _mm_nt(sk_ref[0, 1], pq[:, half:])
    sub = lax.broadcasted_iota(jnp.int32, (SUBLANES, LANES), 0)
    for t in range(tm // LANES):
        ls = slice(t * LANES, (t + 1) * LANES)
        a1 = s1[:, ls]
        a2 = s2[:, ls]
        v1 = _top16([a1[SUBLANES * i:SUBLANES * (i + 1)] for i in range(PEER_NKEYS // SUBLANES)])
        v2 = _top16([a2[SUBLANES * i:SUBLANES * (i + 1)] for i in range(PEER_NKEYS // SUBLANES)])
        v1_hi = _rows_to_sublanes(v1[SUBLANES:], sub)
        v2_lo = _rows_to_sublanes(v2[:SUBLANES], sub)
        v2_hi = _rows_to_sublanes(v2[SUBLANES:], sub)
        cand = [v1[0] + v2_lo, v1[0] + v2_hi]
        cand += [v1[r] + v2_lo for r in range(1, SUBLANES)]
        cand += [v1_hi + v2[0]]
        top = _top16(cand)
        z = _sublane_sum(jnp.exp(_rows_to_sublanes(top[:SUBLANES], sub) - top[0])
                         + jnp.exp(_rows_to_sublanes(top[SUBLANES:], sub) - top[0]))
        tau = top[PEER_TOPK - 1]
        picked = lambda c: jnp.where(c >= tau, 1.0, 0.0)
        count = [_sublane_sum(picked(cand[0]) + picked(cand[1]))]
        count += [_sublane_sum(picked(cand[r + 1])) for r in range(1, SUBLANES)]
        count += [picked(v1[r] + v2[0]) for r in range(SUBLANES, PEER_TOPK)]
        count1 = []
        rank2 = []
        for i in range(PEER_NKEYS // SUBLANES):
            k1 = a1[SUBLANES * i:SUBLANES * (i + 1)]
            k2 = a2[SUBLANES * i:SUBLANES * (i + 1)]
            c1 = jnp.zeros_like(k1)
            r2 = jnp.full_like(k2, float(PEER_TOPK))
            for r in range(PEER_TOPK - 1, -1, -1):
                c1 = jnp.where(k1 == v1[r], count[r], c1)
                r2 = jnp.where(k2 == v2[r], float(r), r2)
            count1.append(c1)
            rank2.append(r2)
        cn1_ref[0, :, ls] = jnp.concatenate(count1, axis=0)
        e1_ref[0, :, ls] = jnp.exp(a1 - v1[0][0:1]) / (2.0 * z[0:1])
        rk2_ref[0, :, ls] = jnp.concatenate(rank2, axis=0).astype(bf16)
        e2_ref[0, :, ls] = jnp.exp(a2 - v2[0][0:1]).astype(bf16)


def _peer_front(h, o, wxo, gf, wq, sk):
    T = h.shape[0]
    tm = min(T, PEER_FRONT_TOKENS)
    assert T % tm == 0
    row = lambda i, hd: (i, 0)
    fixed = lambda i, hd: (0, 0)
    head_t = lambda i, hd: (hd, 0, i)
    words = jax.ShapeDtypeStruct((PEER_HEADS, PEER_NKEYS, T), f32)
    halfs = jax.ShapeDtypeStruct((PEER_HEADS, PEER_NKEYS, T), bf16)
    sc_spec = pl.BlockSpec((1, PEER_NKEYS, tm), head_t)
    return pl.pallas_call(
        functools.partial(_peer_front_body, tm=tm),
        grid=(T // tm, PEER_HEADS),
        in_specs=[pl.BlockSpec((tm, D_MODEL), row), pl.BlockSpec((tm, D_MODEL), row),
                  pl.BlockSpec((D_MODEL, D_MODEL), fixed), pl.BlockSpec((1, D_MODEL), fixed),
                  pl.BlockSpec((D_MODEL, PEER_DQ), lambda i, hd: (0, hd)),
                  pl.BlockSpec((1, 2, PEER_NKEYS, PEER_DQ // 2), lambda i, hd: (hd, 0, 0, 0))],
        out_specs=[pl.BlockSpec((tm, D_MODEL), row), pl.BlockSpec((tm, D_MODEL), row),
                   sc_spec, sc_spec, sc_spec, sc_spec],
        out_shape=[jax.ShapeDtypeStruct((T, D_MODEL), f32), jax.ShapeDtypeStruct((T, D_MODEL), bf16),
                   words, words, halfs, halfs],
        scratch_shapes=[pltpu.VMEM((tm, D_MODEL), bf16)],
        compiler_params=_cparams(("parallel", "arbitrary")),
        name="peer_front",
    )(h, o, wxo, gf, wq, sk)


PEER_ROWS_PER_BLOCK = SUBLANES
PEER_BLOCK = PEER_ROWS_PER_BLOCK * PEER_NKEYS
GATE_ROWS = 8 * SUBLANES
PEER_DENSE_TOKENS = 1024


def _peer_dense_body(hn_ref, u_ref, vt_prev_ref, vt_last_ref, cn1_ref, e1_ref, rk2_ref, e2_ref, h2_ref, gfin_ref,
                     y_ref, act_scr, w_scr, acc_scr, *, tm):
    j = pl.program_id(1)

    @pl.when(j == 0)
    def _():
        acc_scr[...] = jnp.zeros_like(acc_scr)
        w_scr[...] = jnp.zeros_like(w_scr)

    a = lax.dot_general(u_ref[...].astype(bf16), hn_ref[...], (((1,), (1,)), ((), ())), preferred_element_type=f32)
    act_scr[...] = (a * (1.0 + lax.erf(a * math.sqrt(0.5)))).astype(bf16)
    acc_scr[...] += jnp.dot(vt_prev_ref[...], w_scr[...], preferred_element_type=f32)
    chunks = PEER_NKEYS // GATE_ROWS

    def row_tile(ref, hd, r, ls):
        return jnp.broadcast_to(ref[hd, r:r + 1, ls].astype(bf16), (GATE_ROWS, LANES))

    def gate_tile(i, carry):
        ls = pl.ds(pl.multiple_of((i // chunks) * LANES, LANES), LANES)
        c0 = pl.multiple_of((i % chunks) * GATE_ROWS, GATE_ROWS)
        gates = [None] * PEER_ROWS_PER_BLOCK
        for hd in range(PEER_HEADS):
            rk2 = rk2_ref[hd, pl.ds(c0, GATE_ROWS), ls]
            e2 = e2_ref[hd, pl.ds(c0, GATE_ROWS), ls]
            for r in range(PEER_ROWS_PER_BLOCK):
                room = jnp.maximum(row_tile(cn1_ref, hd, r, ls) - rk2, 0.0)
                term = jnp.minimum(row_tile(e1_ref, hd, r, ls) * e2, room)
                gates[r] = term if gates[r] is None else gates[r] + term
        for r in range(PEER_ROWS_PER_BLOCK):
            rs = pl.ds(pl.multiple_of(r * PEER_NKEYS + c0, GATE_ROWS), GATE_ROWS)
            w_scr[rs, ls] = act_scr[rs, ls] * gates[r]
        return carry

    lax.fori_loop(0, (tm // LANES) * chunks, gate_tile, 0)

    @pl.when(j == pl.num_programs(1) - 1)
    def _():
        out_t = acc_scr[...] + jnp.dot(vt_last_ref[...], w_scr[...], preferred_element_type=f32)
        y_ref[...] = _rms(h2_ref[...] + out_t.T, gfin_ref[...])


def _peer_dense(hn, u, vt, cn1, e1, rk2, e2, h2, gfin):
    T = hn.shape[0]
    nb = PEER_BLOCK
    tm = min(T, PEER_DENSE_TOKENS)
    assert T % tm == 0
    row = lambda i, j: (i, 0)
    once = pl.Buffered(1)
    sc1 = pl.BlockSpec((PEER_HEADS, PEER_ROWS_PER_BLOCK, tm), lambda i, j: (0, j, i))
    sc2 = pl.BlockSpec((PEER_HEADS, PEER_NKEYS, tm), lambda i, j: (0, 0, i), pipeline_mode=once)
    return pl.pallas_call(
        functools.partial(_peer_dense_body, tm=tm),
        grid=(T // tm, PEER_N // nb),
        in_specs=[pl.BlockSpec((tm, D_MODEL), row),
                  pl.BlockSpec((nb, D_MODEL), lambda i, j: (j, 0)),
                  pl.BlockSpec((D_MODEL, nb), lambda i, j: (0, jnp.maximum(j - 1, 0))),
                  pl.BlockSpec((D_MODEL, nb), lambda i, j: (0, PEER_N // nb - 1), pipeline_mode=once),
                  sc1, sc1, sc2, sc2,
                  pl.BlockSpec((tm, D_MODEL), row, pipeline_mode=once),
                  pl.BlockSpec((1, D_MODEL), lambda i, j: (0, 0))],
        out_specs=pl.BlockSpec((tm, D_MODEL), row),
        out_shape=jax.ShapeDtypeStruct((T, D_MODEL), f32),
        scratch_shapes=[pltpu.VMEM((nb, tm), bf16), pltpu.VMEM((nb, tm), bf16), pltpu.VMEM((D_MODEL, tm), f32)],
        compiler_params=_cparams(("parallel", "arbitrary")),
        name="peer_dense",
    )(hn, u, vt, vt, cn1, e1, rk2, e2, h2, gfin)


def _rotary_tables(pos):
    half = RET_DK // 2
    inv = ROPE_BASE ** (-jnp.arange(half, dtype=f32) / half)
    ang = pos.astype(f32)[:, None] * inv[None, :]
    cos = jnp.cos(ang)
    sin = jnp.sin(ang)
    cos = jnp.tile(jnp.concatenate([cos, cos], axis=-1), (1, RET_HEADS))
    sin = jnp.tile(jnp.concatenate([-sin, sin], axis=-1), (1, RET_HEADS))
    return cos, sin


def _layer_weights(l, norm_mix, w_in, w_a2, b_a, gla_head_norm, ret_head_norm, w_pa, w_pb, w_o, norm_xattn,
                   norm_mem, w_xq, w_xk, w_xv, w_xo, norm_ffn, peer_wq, peer_subkeys, peer_u, peer_v):
    offs = [0]
    for s in IN_SIZES:
        offs.append(offs[-1] + s)
    cols = [w_in[l][:, offs[i]:offs[i + 1]] for i in range(len(IN_SIZES))]
    gq, gk, gv, glr, gr, rq, rk, rv, rg, za, zb = cols
    glr = jnp.pad(glr, ((0, 0), (0, RANK_PAD - GLA_RANK)))
    half = RET_DK // 2
    j = jnp.arange(RET_QK)
    partner = jnp.where((j % RET_DK) < half, j + half, j - half)
    w = {}
    w["gla"] = jnp.concatenate([gq, gk, gv, glr, gr], axis=1).astype(bf16)
    w["ret"] = jnp.concatenate([rq, rk, rv, rg, rq[:, partner], rk[:, partner]], axis=1).astype(bf16)
    w["z"] = jnp.concatenate([za, zb], axis=1).astype(bf16)
    w["a2"] = jnp.pad(w_a2[l], ((0, RANK_PAD - GLA_RANK), (0, 0))).astype(bf16)
    w["ba"] = b_a[l].reshape(1, GLA_QK)
    w["gn"] = gla_head_norm[l]
    w["rn"] = ret_head_norm[l]
    log_gamma = jnp.log1p(-(2.0 ** (-5.0 - jnp.arange(RET_HEADS, dtype=f32))))
    w["lg"] = jnp.repeat(log_gamma, RET_DK).reshape(1, RET_QK)
    w["norm_mix"] = norm_mix[l].reshape(1, D_MODEL)
    w["pa"] = w_pa[l].astype(bf16)
    w["pb"] = w_pb[l].astype(bf16)
    w["o"] = w_o[l].astype(bf16)
    w["norm_xattn"] = norm_xattn[l].reshape(1, D_MODEL)
    w["norm_mem"] = norm_mem[l].reshape(1, D_MODEL)
    w["xq"] = w_xq[l].astype(bf16)
    w["xk"] = w_xk[l].astype(bf16)
    w["xv"] = w_xv[l].astype(bf16)
    w["xo"] = w_xo[l].astype(bf16)
    w["norm_ffn"] = norm_ffn[l].reshape(1, D_MODEL)
    w["wq"] = peer_wq[l].astype(bf16)
    w["sk"] = peer_subkeys[l].astype(bf16)
    w["u"] = peer_u[l]
    w["vt"] = peer_v[l].T.astype(bf16)
    return w


def _layer(x, B, S, keep, C, valid, cos, sin, xattn, sg0, sr0, w, gfin):
    gla, ret, z = _in_proj(x, w["norm_mix"], w["gla"], w["ret"], w["z"])
    oa, ob, sg, sr = _scan(gla, ret, w["a2"], w["ba"], w["gn"], w["rn"], cos, sin, w["lg"], sg0, sr0, B, S, C, valid)
    h, q = _merge(x, oa, ob, z, w["pa"], w["pb"], w["o"], w["norm_xattn"], w["xq"])
    o = xattn(q)
    if keep < S:
        h = h.reshape(B, S, D_MODEL)[:, :keep].reshape(B * keep, D_MODEL)
        o = o.reshape(B, S, D_MODEL)[:, :keep].reshape(B * keep, D_MODEL)
    h2, hn, cn1, e1, rk2, e2 = _peer_front(h, o, w["xo"], w["norm_ffn"], w["wq"], w["sk"])
    y = _peer_dense(hn, w["u"], w["vt"], cn1, e1, rk2, e2, h2, gfin)
    return y, sg, sr


def kernel(x_prompt, x_sample, mem_prompt, state_gla, state_ret, cache_mem_k, cache_mem_v, norm_mix, w_in, w_a2,
           b_a, gla_head_norm, ret_head_norm, w_pa, w_pb, w_o, norm_xattn, norm_mem, w_xq, w_xk, w_xv, w_xo,
           norm_ffn, peer_wq, peer_subkeys, peer_u, peer_v, norm_final):
    depth = w_in.shape[0]
    assert depth == 1, "the final norm is fused into the layer's last kernel"
    Bp, Sp, _ = x_prompt.shape
    Bs, Ss, _ = x_sample.shape
    l = 0
    w = _layer_weights(l, norm_mix, w_in, w_a2, b_a, gla_head_norm, ret_head_norm, w_pa, w_pb, w_o, norm_xattn,
                       norm_mem, w_xq, w_xk, w_xv, w_xo, norm_ffn, peer_wq, peer_subkeys, peer_u, peer_v)
    gfin = norm_final.reshape(1, D_MODEL)

    cos_p, sin_p = _rotary_tables(jnp.arange(Sp, dtype=jnp.int32))
    mk, mv = _mem_kv(mem_prompt.reshape(Bp * N_MEM, D_MODEL), w["norm_mem"], w["xk"], w["xv"])
    zeros = jnp.zeros((Bp, GLA_HEADS, GLA_DK, GLA_DV), f32)
    xattn_p = functools.partial(_xattn, mk=mk.reshape(Bp, N_MEM, D_MODEL), mv=mv.reshape(Bp, N_MEM, D_MODEL),
                                B=Bp, S=Sp, tq=XATTN_ROWS, seqs=1)
    yp, sgp, srp = _layer(x_prompt.reshape(Bp * Sp, D_MODEL), Bp, Sp, Sp, SCAN_CHUNK, SCAN_CHUNK, cos_p, sin_p,
                          xattn_p, zeros, zeros, w, gfin)

    pad = SAMPLE_PAD - Ss
    xs = jnp.pad(x_sample, ((0, 0), (0, pad), (0, 0))).reshape(Bs * SAMPLE_PAD, D_MODEL)
    cos_s, sin_s = _rotary_tables(PAST_LEN + jnp.arange(SAMPLE_PAD, dtype=jnp.int32))
    xattn_s = functools.partial(_xattn_cache, cache_k=cache_mem_k, cache_v=cache_mem_v, layer=l, B=Bs,
                                tq=SAMPLE_PAD, seqs=XATTN_SEQS)
    ys, sgs, srs = _layer(xs, Bs, SAMPLE_PAD, Ss, SAMPLE_PAD, Ss, cos_s, sin_s,
                          xattn_s, state_gla[l], state_ret[l], w, gfin)
    ys = ys.reshape(Bs, Ss, D_MODEL)

    kv_shape = (1, Bp, N_MEM, XA_HEADS, XA_DH)
    return (yp.reshape(Bp, Sp, D_MODEL), ys, sgp[None], srp[None], mk.reshape(kv_shape), mv.reshape(kv_shape),
            sgs[None], srs[None])
```

```python
import functools
import math

import jax
import jax.numpy as jnp
from jax import lax
from jax.experimental import pallas as pl
from jax.experimental.pallas import tpu as pltpu

f32 = jnp.float32
bf16 = jnp.bfloat16

D_MODEL = 1024
PAST_LEN = 16384
GLA_HEADS, GLA_DK, GLA_DV, GLA_RANK, GLA_TEMP = 4, 64, 128, 16, 16.0
RET_HEADS, RET_DK, RET_DV = 4, 64, 128
ROPE_BASE = 10000.0
N_MEM = 256
XA_HEADS = 4
XA_DH = D_MODEL // XA_HEADS
PEER_HEADS, PEER_NKEYS, PEER_DQ, PEER_TOPK = 8, 128, 256, 16
PEER_N = PEER_NKEYS * PEER_NKEYS
EPS = 1e-6

GLA_QK = GLA_HEADS * GLA_DK
GLA_V = GLA_HEADS * GLA_DV
RET_QK = RET_HEADS * RET_DK
RET_V = RET_HEADS * RET_DV
IN_SIZES = (GLA_QK, GLA_QK, GLA_V, GLA_RANK, GLA_V, RET_QK, RET_QK, RET_V, RET_V, D_MODEL, D_MODEL)

LANES = 128
SUBLANES = 8
RANK_PAD = LANES
GLA_COLS = 2 * GLA_QK + GLA_V + RANK_PAD + GLA_V
RET_COLS = 2 * RET_QK + 2 * RET_V + 2 * RET_QK
Z_COLS = 2 * D_MODEL
SCAN_CHUNK = 64
SCAN_SEQS = 4
SAMPLE_PAD = SUBLANES
XATTN_ROWS = 1024
PEER_FRONT_TOKENS = 1024
XATTN_SEQS = 8
VMEM_LIMIT = 52 * 1024 * 1024


def _cparams(sem):
    return pltpu.CompilerParams(dimension_semantics=sem, vmem_limit_bytes=VMEM_LIMIT)


def _rms(x, g):
    return x * lax.rsqrt(jnp.mean(x * x, axis=-1, keepdims=True) + EPS) * g


def _mm(a, b):
    return jnp.dot(a.astype(bf16), b.astype(bf16), preferred_element_type=f32)


def _mm_nt(a, b):
    return lax.dot_general(a.astype(bf16), b.astype(bf16), (((1,), (1,)), ((), ())), preferred_element_type=f32)


def _mm_tn(a, b):
    return lax.dot_general(a.astype(bf16), b.astype(bf16), (((0,), (0,)), ((), ())), preferred_element_type=f32)


def _in_proj_body(x_ref, g_ref, wg_ref, wr_ref, wz_ref, og_ref, or_ref, oz_ref):
    xn = _rms(x_ref[...], g_ref[...]).astype(bf16)
    og_ref[...] = jnp.dot(xn, wg_ref[...], preferred_element_type=f32)
    or_ref[...] = jnp.dot(xn, wr_ref[...], preferred_element_type=f32)
    oz_ref[...] = jnp.dot(xn, wz_ref[...], preferred_element_type=f32)


def _in_proj(x, g, wg, wr, wz, tm=256):
    T = x.shape[0]
    row = lambda i: (i, 0)
    fixed = lambda i: (0, 0)
    return pl.pallas_call(
        _in_proj_body,
        grid=(T // tm,),
        in_specs=[pl.BlockSpec((tm, D_MODEL), row), pl.BlockSpec((1, D_MODEL), fixed),
                  pl.BlockSpec((D_MODEL, GLA_COLS), fixed), pl.BlockSpec((D_MODEL, RET_COLS), fixed),
                  pl.BlockSpec((D_MODEL, Z_COLS), fixed)],
        out_specs=[pl.BlockSpec((tm, GLA_COLS), row), pl.BlockSpec((tm, RET_COLS), row),
                   pl.BlockSpec((tm, Z_COLS), row)],
        out_shape=[jax.ShapeDtypeStruct((T, GLA_COLS), f32), jax.ShapeDtypeStruct((T, RET_COLS), f32),
                   jax.ShapeDtypeStruct((T, Z_COLS), f32)],
        compiler_params=_cparams(("parallel",)),
        name="in_proj",
    )(x, g, wg, wr, wz)


def _chunk_heads(q, k, v, b, gate, hnorm_ref, state_ref, o_ref, C):
    mid = C // 2 - 1
    b_mid = b[mid:mid + 1, :]
    b_last = b[C - 1:C, :]
    q_in = q * jnp.exp(b)
    q_e = q * jnp.exp(b - b_mid)
    k_e = k * jnp.exp(b_mid - b)
    k_d = k * jnp.exp(b_last - b)
    decay_col = jnp.exp(jnp.broadcast_to(b_last, (SUBLANES, b.shape[1])).T[:, 0:1])
    lane_head = lax.broadcasted_iota(jnp.int32, (C, 4 * 64), 1) // 64
    stack = lambda x: jnp.concatenate([jnp.where(lane_head == h, x, 0.0) for h in range(4)], axis=0)
    rows = lax.broadcasted_iota(jnp.int32, (4 * C, 4 * C), 0) % C
    cols = lax.broadcasted_iota(jnp.int32, (4 * C, 4 * C), 1) % C
    state = state_ref[...]
    v_rows = jnp.concatenate([v[:, h * 128:(h + 1) * 128] for h in range(4)], axis=0)
    att = jnp.where(rows >= cols, _mm_nt(stack(q_e), stack(k_e)), 0.0)
    o = _mm(jnp.concatenate([stack(q_in), att], axis=1), jnp.concatenate([state, v_rows], axis=0))
    state_ref[...] = decay_col * state + _mm_tn(stack(k_d), v_rows)
    for h in range(4):
        vs = slice(h * 128, (h + 1) * 128)
        g_h = gate[:, vs]
        o_ref[:, vs] = _rms(o[h * C:(h + 1) * C, :], hnorm_ref[h:h + 1, :]) * (g_h * jax.nn.sigmoid(g_h))


def _scan_body(gla_ref, ret_ref, wa2_ref, ba_ref, gn_ref, rn_ref, cos_ref, sin_ref, lg_ref, sg0_ref, sr0_ref,
               oa_ref, ob_ref, sg_ref, sr_ref, sg_scr, sr_scr, *, C, valid):
    c = pl.program_id(1)

    @pl.when(c == 0)
    def _():
        sg_scr[...] = sg0_ref[...]
        sr_scr[...] = sr0_ref[...]

    row = lax.broadcasted_iota(jnp.int32, (C, 1), 0)
    tri = (lax.broadcasted_iota(jnp.int32, (C, C), 0) >= lax.broadcasted_iota(jnp.int32, (C, C), 1)).astype(bf16)
    cos = cos_ref[...]
    sin = sin_ref[...]

    for s in range(SCAN_SEQS):
        g = gla_ref[s]
        q = g[:, 0:GLA_QK] * GLA_DK ** -0.5
        k = g[:, GLA_QK:2 * GLA_QK]
        v = g[:, 2 * GLA_QK:2 * GLA_QK + GLA_V]
        lowrank = g[:, 2 * GLA_QK + GLA_V:2 * GLA_QK + GLA_V + RANK_PAD]
        gate = g[:, 2 * GLA_QK + GLA_V + RANK_PAD:]
        log_a = jax.nn.log_sigmoid(_mm(lowrank, wa2_ref[...]) + ba_ref[...]) / GLA_TEMP
        if valid < C:
            log_a = jnp.where(row < valid, log_a, 0.0)
        hi = log_a.astype(bf16)
        rest = log_a - hi.astype(f32)
        mid = rest.astype(bf16)
        lo = (rest - mid.astype(f32)).astype(bf16)
        parts = jnp.dot(tri, jnp.concatenate([hi, mid, lo], axis=1), preferred_element_type=f32)
        b = parts[:, 0:GLA_QK] + parts[:, GLA_QK:2 * GLA_QK] + parts[:, 2 * GLA_QK:]
        _chunk_heads(q, k, v, b, gate, gn_ref, sg_scr.at[s], oa_ref.at[s], C)

        r = ret_ref[s]
        q = r[:, 0:RET_QK] * cos + r[:, 2 * RET_QK + 2 * RET_V:3 * RET_QK + 2 * RET_V] * sin
        k = (r[:, RET_QK:2 * RET_QK] * cos + r[:, 3 * RET_QK + 2 * RET_V:] * sin) * RET_DK ** -0.5
        v = r[:, 2 * RET_QK:2 * RET_QK + RET_V]
        gate = r[:, 2 * RET_QK + RET_V:2 * RET_QK + 2 * RET_V]
        steps = jnp.minimum(row + 1, valid).astype(f32)
        b = steps * lg_ref[...]
        _chunk_heads(q, k, v, b, gate, rn_ref, sr_scr.at[s], ob_ref.at[s], C)

    @pl.when(c == pl.num_programs(1) - 1)
    def _():
        sg_ref[...] = sg_scr[...]
        sr_ref[...] = sr_scr[...]


def _scan(gla, ret, wa2, ba, gn, rn, cos, sin, lg, sg0, sr0, B, S, C, valid):
    n = S // C
    G = SCAN_SEQS
    assert B % G == 0
    gla = gla.reshape(B, S, GLA_COLS)
    ret = ret.reshape(B, S, RET_COLS)
    tok = lambda b, c: (b, c, 0)
    fixed = lambda b, c: (0, 0)
    pos = lambda b, c: (c, 0)
    st = lambda b, c: (b, 0, 0)
    st_block = (G, GLA_HEADS * GLA_DK, GLA_DV)
    sg0 = sg0.reshape(B, GLA_HEADS * GLA_DK, GLA_DV)
    sr0 = sr0.reshape(B, RET_HEADS * RET_DK, RET_DV)
    oa, ob, sg, sr = pl.pallas_call(
        functools.partial(_scan_body, C=C, valid=valid),
        grid=(B // G, n),
        in_specs=[pl.BlockSpec((G, C, GLA_COLS), tok), pl.BlockSpec((G, C, RET_COLS), tok),
                  pl.BlockSpec((RANK_PAD, GLA_QK), fixed), pl.BlockSpec((1, GLA_QK), fixed),
                  pl.BlockSpec((GLA_HEADS, GLA_DV), fixed), pl.BlockSpec((RET_HEADS, RET_DV), fixed),
                  pl.BlockSpec((C, RET_QK), pos), pl.BlockSpec((C, RET_QK), pos),
                  pl.BlockSpec((1, RET_QK), fixed),
                  pl.BlockSpec(st_block, st), pl.BlockSpec(st_block, st)],
        out_specs=[pl.BlockSpec((G, C, GLA_V), tok), pl.BlockSpec((G, C, RET_V), tok),
                   pl.BlockSpec(st_block, st), pl.BlockSpec(st_block, st)],
        out_shape=[jax.ShapeDtypeStruct((B, S, GLA_V), f32), jax.ShapeDtypeStruct((B, S, RET_V), f32),
                   jax.ShapeDtypeStruct((B,) + st_block[1:], f32), jax.ShapeDtypeStruct((B,) + st_block[1:], f32)],
        scratch_shapes=[pltpu.VMEM(st_block, f32), pltpu.VMEM(st_block, f32)],
        compiler_params=_cparams(("parallel", "arbitrary")),
        name="scan",
    )(gla, ret, wa2, ba, gn, rn, cos, sin, lg, sg0, sr0)
    state_shape = (B, GLA_HEADS, GLA_DK, GLA_DV)
    return oa.reshape(B * S, GLA_V), ob.reshape(B * S, RET_V), sg.reshape(state_shape), sr.reshape(state_shape)


def _merge_body(x_ref, oa_ref, ob_ref, z_ref, wpa_ref, wpb_ref, wo_ref, gx_ref, wxq_ref, h_ref, q_ref):
    z = z_ref[...]
    merged = (jax.nn.sigmoid(z[:, :D_MODEL]) * _mm(oa_ref[...], wpa_ref[...])
              + jax.nn.sigmoid(z[:, D_MODEL:]) * _mm(ob_ref[...], wpb_ref[...]))
    h = x_ref[...] + _mm(merged, wo_ref[...])
    h_ref[...] = h
    q_ref[...] = _mm(_rms(h, gx_ref[...]), wxq_ref[...])


def _merge(x, oa, ob, z, wpa, wpb, wo, gx, wxq, tm=512):
    T = x.shape[0]
    row = lambda i: (i, 0)
    fixed = lambda i: (0, 0)
    return pl.pallas_call(
        _merge_body,
        grid=(T // tm,),
        in_specs=[pl.BlockSpec((tm, D_MODEL), row), pl.BlockSpec((tm, GLA_V), row), pl.BlockSpec((tm, RET_V), row),
                  pl.BlockSpec((tm, Z_COLS), row),
                  pl.BlockSpec((GLA_V, D_MODEL), fixed), pl.BlockSpec((RET_V, D_MODEL), fixed),
                  pl.BlockSpec((D_MODEL, D_MODEL), fixed), pl.BlockSpec((1, D_MODEL), fixed),
                  pl.BlockSpec((D_MODEL, D_MODEL), fixed)],
        out_specs=[pl.BlockSpec((tm, D_MODEL), row), pl.BlockSpec((tm, D_MODEL), row)],
        out_shape=[jax.ShapeDtypeStruct((T, D_MODEL), f32), jax.ShapeDtypeStruct((T, D_MODEL), f32)],
        compiler_params=_cparams(("parallel",)),
        name="merge",
    )(x, oa, ob, z, wpa, wpb, wo, gx, wxq)


def _mem_kv_body(m_ref, g_ref, wk_ref, wv_ref, k_ref, v_ref):
    mn = _rms(m_ref[...], g_ref[...]).astype(bf16)
    k_ref[...] = jnp.dot(mn, wk_ref[...], preferred_element_type=f32)
    v_ref[...] = jnp.dot(mn, wv_ref[...], preferred_element_type=f32)


def _mem_kv(mem, g, wk, wv, tm=256):
    T = mem.shape[0]
    row = lambda i: (i, 0)
    fixed = lambda i: (0, 0)
    return pl.pallas_call(
        _mem_kv_body,
        grid=(T // tm,),
        in_specs=[pl.BlockSpec((tm, D_MODEL), row), pl.BlockSpec((1, D_MODEL), fixed),
                  pl.BlockSpec((D_MODEL, D_MODEL), fixed), pl.BlockSpec((D_MODEL, D_MODEL), fixed)],
        out_specs=[pl.BlockSpec((tm, D_MODEL), row), pl.BlockSpec((tm, D_MODEL), row)],
        out_shape=[jax.ShapeDtypeStruct((T, D_MODEL), f32), jax.ShapeDtypeStruct((T, D_MODEL), f32)],
        compiler_params=_cparams(("parallel",)),
        name="mem_kv",
    )(mem, g, wk, wv)


def _xattn_body(q_ref, k_ref, v_ref, o_ref, *, seqs, tq):
    for s in range(seqs):
        rows = slice(s * tq, (s + 1) * tq)
        q = q_ref[rows, :]
        for h in range(XA_HEADS):
            hs = slice(h * XA_DH, (h + 1) * XA_DH)
            sc = _mm_nt(q[:, hs], k_ref[s, :, hs]) * XA_DH ** -0.5
            p = jnp.exp(sc - jnp.max(sc, axis=-1, keepdims=True))
            p = p / jnp.sum(p, axis=-1, keepdims=True)
            o_ref[rows, hs] = _mm(p, v_ref[s, :, hs])


def _xattn(q, mk, mv, B, S, tq, seqs):
    n = S // tq
    assert seqs == 1 or n == 1
    tok = lambda b, j: (b * n + j, 0)
    mem_spec = pl.BlockSpec((seqs, N_MEM, D_MODEL), lambda b, j: (b, 0, 0))
    return pl.pallas_call(
        functools.partial(_xattn_body, seqs=seqs, tq=tq),
        grid=(B // seqs, n),
        in_specs=[pl.BlockSpec((seqs * tq, D_MODEL), tok), mem_spec, mem_spec],
        out_specs=pl.BlockSpec((seqs * tq, D_MODEL), tok),
        out_shape=jax.ShapeDtypeStruct((B * S, D_MODEL), f32),
        compiler_params=_cparams(("parallel", "parallel")),
        name="xattn",
    )(q, mk, mv)


def _xattn_cache_body(q_ref, k_hbm, v_hbm, o_ref, kbuf, vbuf, sem, *, seqs, tq, layer):
    i = pl.program_id(0)
    n = pl.num_programs(0)

    def copies(step, slot):
        out = []
        for s in range(seqs):
            for h in range(XA_HEADS):
                hs = pl.ds(h * XA_DH, XA_DH)
                b = step * seqs + s
                out.append(pltpu.make_async_copy(k_hbm.at[layer, b, :, h, :], kbuf.at[slot, s, :, hs],
                                                 sem.at[slot, 0, s, h]))
                out.append(pltpu.make_async_copy(v_hbm.at[layer, b, :, h, :], vbuf.at[slot, s, :, hs],
                                                 sem.at[slot, 1, s, h]))
        return out

    @pl.when(i == 0)
    def _():
        for c in copies(0, 0):
            c.start()

    @pl.when(i + 1 < n)
    def _():
        for c in copies(i + 1, (i + 1) % 2):
            c.start()

    slot = i % 2
    for c in copies(i, slot):
        c.wait()
    lane_head = lax.broadcasted_iota(jnp.int32, (tq, D_MODEL), 1) // XA_DH
    for s in range(seqs):
        rows = slice(s * tq, (s + 1) * tq)
        q = q_ref[rows, :]
        stacked = jnp.concatenate([jnp.where(lane_head == h, q, 0.0) for h in range(XA_HEADS)], axis=0)
        sc = _mm_nt(stacked, kbuf[slot, s]) * XA_DH ** -0.5
        p = jnp.exp(sc - jnp.max(sc, axis=-1, keepdims=True))
        p = p / jnp.sum(p, axis=-1, keepdims=True)
        o = _mm(p, vbuf[slot, s])
        for h in range(XA_HEADS):
            hs = slice(h * XA_DH, (h + 1) * XA_DH)
            o_ref[rows, hs] = o[h * tq:(h + 1) * tq, hs]


def _xattn_cache(q, cache_k, cache_v, layer, B, tq, seqs):
    assert B % seqs == 0
    tok = lambda b: (b, 0)
    buf = pltpu.VMEM((2, seqs, N_MEM, D_MODEL), cache_k.dtype)
    return pl.pallas_call(
        functools.partial(_xattn_cache_body, seqs=seqs, tq=tq, layer=layer),
        grid=(B // seqs,),
        in_specs=[pl.BlockSpec((seqs * tq, D_MODEL), tok), pl.BlockSpec(memory_space=pl.ANY),
                  pl.BlockSpec(memory_space=pl.ANY)],
        out_specs=pl.BlockSpec((seqs * tq, D_MODEL), tok),
        out_shape=jax.ShapeDtypeStruct((B * tq, D_MODEL), f32),
        scratch_shapes=[buf, buf, pltpu.SemaphoreType.DMA((2, 2, seqs, XA_HEADS))],
        compiler_params=_cparams(("arbitrary",)),
        name="xattn_cache",
    )(q, cache_k, cache_v)


def _merge_exchange_network(n):
    pairs = []
    p = 1
    while p < n:
        k = p
        while k >= 1:
            for j in range(k % p, n - k, 2 * k):
                for i in range(min(k, n - j - k)):
                    if (i + j) // (2 * p) == (i + j + k) // (2 * p):
                        pairs.append((i + j, i + j + k))
            k //= 2
        p *= 2
    return pairs


def _compare_exchange(x, i, j):
    a, b = x[i], x[j]
    if b is None:
        return
    if a is None:
        x[i], x[j] = b, None
        return
    x[i], x[j] = jnp.maximum(a, b), jnp.minimum(a, b)


def _top16(tiles):
    n = PEER_TOPK
    x = list(tiles) + [None] * (n - len(tiles))
    for i, j in _merge_exchange_network(n):
        _compare_exchange(x, i, j)
    for shift in (4, 2, 1):
        merged = []
        for i in range(n):
            a, b = x[i], x[n - 1 - i]
            b = None if b is None else pltpu.roll(b, shift, 0)
            merged.append(b if a is None else (a if b is None else jnp.maximum(a, b)))
        x = merged
        d = n // 2
        while d >= 1:
            for i in range(n):
                if (i & d) == 0:
                    _compare_exchange(x, i, i + d)
            d //= 2
    return x


def _sublane_sum(x):
    for shift in (4, 2, 1):
        x = x + pltpu.roll(x, shift, 0)
    return x


def _rows_to_sublanes(v, sub):
    out = v[SUBLANES - 1]
    for r in range(SUBLANES - 2, -1, -1):
        out = jnp.where(sub == r, v[r], out)
    return out


def _peer_front_body(h_ref, o_ref, wxo_ref, gf_ref, wq_ref, sk_ref,
                     h2_ref, hn_ref, cn1_ref, e1_ref, rk2_ref, e2_ref, hn_scr, *, tm):
    hd = pl.program_id(1)

    @pl.when(hd == 0)
    def _():
        h2 = h_ref[...] + _mm(o_ref[...], wxo_ref[...])
        h2_ref[...] = h2
        hn = _rms(h2, gf_ref[...]).astype(bf16)
        hn_scr[...] = hn
        hn_ref[...] = hn

    pq = jnp.dot(hn_scr[...], wq_ref[...], preferred_element_type=f32)
    half = PEER_DQ // 2
    s1 = _mm_nt(sk_ref[0, 0], pq[:, :half])
    s2 = _mm_nt(sk_ref[0, 1], pq[:, half:])
    sub = lax.broadcasted_iota(jnp.int32, (SUBLANES, LANES), 0)
    for t in range(tm // LANES):
        ls = slice(t * LANES, (t + 1) * LANES)
        a1 = s1[:, ls]
        a2 = s2[:, ls]
        v1 = _top16([a1[SUBLANES * i:SUBLANES * (i + 1)] for i in range(PEER_NKEYS // SUBLANES)])
        v2 = _top16([a2[SUBLANES * i:SUBLANES * (i + 1)] for i in range(PEER_NKEYS // SUBLANES)])
        v1_hi = _rows_to_sublanes(v1[SUBLANES:], sub)
        v2_lo = _rows_to_sublanes(v2[:SUBLANES], sub)
        v2_hi = _rows_to_sublanes(v2[SUBLANES:], sub)
        cand = [v1[0] + v2_lo, v1[0] + v2_hi]
        cand += [v1[r] + v2_lo for r in range(1, SUBLANES)]
        cand += [v1_hi + v2[0]]
        top = _top16(cand)
        z = _sublane_sum(jnp.exp(_rows_to_sublanes(top[:SUBLANES], sub) - top[0])
                         + jnp.exp(_rows_to_sublanes(top[SUBLANES:], sub) - top[0]))
        tau = top[PEER_TOPK - 1]
        picked = lambda c: jnp.where(c >= tau, 1.0, 0.0)
        count = [_sublane_sum(picked(cand[0]) + picked(cand[1]))]
        count += [_sublane_sum(picked(cand[r + 1])) for r in range(1, SUBLANES)]
        count += [picked(v1[r] + v2[0]) for r in range(SUBLANES, PEER_TOPK)]
        count1 = []
        rank2 = []
        for i in range(PEER_NKEYS // SUBLANES):
            k1 = a1[SUBLANES * i:SUBLANES * (i + 1)]
            k2 = a2[SUBLANES * i:SUBLANES * (i + 1)]
            c1 = jnp.zeros_like(k1)
            r2 = jnp.full_like(k2, float(PEER_TOPK))
            for r in range(PEER_TOPK - 1, -1, -1):
                c1 = jnp.where(k1 == v1[r], count[r], c1)
                r2 = jnp.where(k2 == v2[r], float(r), r2)
            count1.append(c1)
            rank2.append(r2)
        cn1_ref[0, :, ls] = jnp.concatenate(count1, axis=0)
        e1_ref[0, :, ls] = jnp.exp(a1 - v1[0][0:1]) / (2.0 * z[0:1])
        rk2_ref[0, :, ls] = jnp.concatenate(rank2, axis=0).astype(bf16)
        e2_ref[0, :, ls] = jnp.exp(a2 - v2[0][0:1]).astype(bf16)


def _peer_front(h, o, wxo, gf, wq, sk):
    T = h.shape[0]
    tm = min(T, PEER_FRONT_TOKENS)
    assert T % tm == 0
    row = lambda i, hd: (i, 0)
    fixed = lambda i, hd: (0, 0)
    head_t = lambda i, hd: (hd, 0, i)
    words = jax.ShapeDtypeStruct((PEER_HEADS, PEER_NKEYS, T), f32)
    halfs = jax.ShapeDtypeStruct((PEER_HEADS, PEER_NKEYS, T), bf16)
    sc_spec = pl.BlockSpec((1, PEER_NKEYS, tm), head_t)
    return pl.pallas_call(
        functools.partial(_peer_front_body, tm=tm),
        grid=(T // tm, PEER_HEADS),
        in_specs=[pl.BlockSpec((tm, D_MODEL), row), pl.BlockSpec((tm, D_MODEL), row),
                  pl.BlockSpec((D_MODEL, D_MODEL), fixed), pl.BlockSpec((1, D_MODEL), fixed),
                  pl.BlockSpec((D_MODEL, PEER_DQ), lambda i, hd: (0, hd)),
                  pl.BlockSpec((1, 2, PEER_NKEYS, PEER_DQ // 2), lambda i, hd: (hd, 0, 0, 0))],
        out_specs=[pl.BlockSpec((tm, D_MODEL), row), pl.BlockSpec((tm, D_MODEL), row),
                   sc_spec, sc_spec, sc_spec, sc_spec],
        out_shape=[jax.ShapeDtypeStruct((T, D_MODEL), f32), jax.ShapeDtypeStruct((T, D_MODEL), bf16),
                   words, words, halfs, halfs],
        scratch_shapes=[pltpu.VMEM((tm, D_MODEL), bf16)],
        compiler_params=_cparams(("parallel", "arbitrary")),
        name="peer_front",
    )(h, o, wxo, gf, wq, sk)


PEER_ROWS_PER_BLOCK = SUBLANES
PEER_BLOCK = PEER_ROWS_PER_BLOCK * PEER_NKEYS
GATE_ROWS = 8 * SUBLANES
PEER_DENSE_TOKENS = 1024
PEER_DENSE_VMEM = 60 * 1024 * 1024


def _peer_dense_body(hn_ref, u_ref, vt_prev_ref, vt_last_ref, cn1_ref, e1_ref, rk2_ref, e2_ref, h2_ref, gfin_ref,
                     y_ref, act_scr, w_scr, acc_scr, *, tm):
    j = pl.program_id(1)

    @pl.when(j == 0)
    def _():
        acc_scr[...] = jnp.zeros_like(acc_scr)
        w_scr[...] = jnp.zeros_like(w_scr)

    a = lax.dot_general(u_ref[...].astype(bf16), hn_ref[...], (((1,), (1,)), ((), ())), preferred_element_type=f32)
    act_scr[...] = (a * (1.0 + lax.erf(a * math.sqrt(0.5)))).astype(bf16)
    acc_scr[...] += jnp.dot(vt_prev_ref[...], w_scr[...], preferred_element_type=f32)
    chunks = PEER_NKEYS // GATE_ROWS

    def row_tile(ref, hd, r, ls):
        return jnp.broadcast_to(ref[hd, r:r + 1, ls].astype(bf16), (GATE_ROWS, LANES))

    def gate_tile(i, carry):
        ls = pl.ds(pl.multiple_of((i // chunks) * LANES, LANES), LANES)
        c0 = pl.multiple_of((i % chunks) * GATE_ROWS, GATE_ROWS)
        gates = [None] * PEER_ROWS_PER_BLOCK
        for hd in range(PEER_HEADS):
            rk2 = rk2_ref[hd, pl.ds(c0, GATE_ROWS), ls]
            e2 = e2_ref[hd, pl.ds(c0, GATE_ROWS), ls]
            for r in range(PEER_ROWS_PER_BLOCK):
                room = jnp.maximum(row_tile(cn1_ref, hd, r, ls) - rk2, 0.0)
                term = jnp.minimum(row_tile(e1_ref, hd, r, ls) * e2, room)
                gates[r] = term if gates[r] is None else gates[r] + term
        for r in range(PEER_ROWS_PER_BLOCK):
            rs = pl.ds(pl.multiple_of(r * PEER_NKEYS + c0, GATE_ROWS), GATE_ROWS)
            w_scr[rs, ls] = act_scr[rs, ls] * gates[r]
        return carry

    lax.fori_loop(0, (tm // LANES) * chunks, gate_tile, 0)

    @pl.when(j == pl.num_programs(1) - 1)
    def _():
        out_t = acc_scr[...] + jnp.dot(vt_last_ref[...], w_scr[...], preferred_element_type=f32)
        y_ref[...] = _rms(h2_ref[...] + out_t.T, gfin_ref[...])


def _peer_dense(hn, u, vt, cn1, e1, rk2, e2, h2, gfin):
    T = hn.shape[0]
    nb = PEER_BLOCK
    tm = min(T, PEER_DENSE_TOKENS)
    assert T % tm == 0
    row = lambda i, j: (i, 0)
    once = pl.Buffered(1)
    params = pltpu.CompilerParams(dimension_semantics=("parallel", "arbitrary"), vmem_limit_bytes=PEER_DENSE_VMEM)
    sc1 = pl.BlockSpec((PEER_HEADS, PEER_ROWS_PER_BLOCK, tm), lambda i, j: (0, j, i))
    sc2 = pl.BlockSpec((PEER_HEADS, PEER_NKEYS, tm), lambda i, j: (0, 0, i))
    return pl.pallas_call(
        functools.partial(_peer_dense_body, tm=tm),
        grid=(T // tm, PEER_N // nb),
        in_specs=[pl.BlockSpec((tm, D_MODEL), row),
                  pl.BlockSpec((nb, D_MODEL), lambda i, j: (j, 0)),
                  pl.BlockSpec((D_MODEL, nb), lambda i, j: (0, jnp.maximum(j - 1, 0))),
                  pl.BlockSpec((D_MODEL, nb), lambda i, j: (0, PEER_N // nb - 1), pipeline_mode=once),
                  sc1, sc1, sc2, sc2,
                  pl.BlockSpec((tm, D_MODEL), row),
                  pl.BlockSpec((1, D_MODEL), lambda i, j: (0, 0))],
        out_specs=pl.BlockSpec((tm, D_MODEL), row),
        out_shape=jax.ShapeDtypeStruct((T, D_MODEL), f32),
        scratch_shapes=[pltpu.VMEM((nb, tm), bf16), pltpu.VMEM((nb, tm), bf16), pltpu.VMEM((D_MODEL, tm), f32)],
        compiler_params=params,
        name="peer_dense",
    )(hn, u, vt, vt, cn1, e1, rk2, e2, h2, gfin)


def _rotary_tables(pos):
    half = RET_DK // 2
    inv = ROPE_BASE ** (-jnp.arange(half, dtype=f32) / half)
    ang = pos.astype(f32)[:, None] * inv[None, :]
    cos = jnp.cos(ang)
    sin = jnp.sin(ang)
    cos = jnp.tile(jnp.concatenate([cos, cos], axis=-1), (1, RET_HEADS))
    sin = jnp.tile(jnp.concatenate([-sin, sin], axis=-1), (1, RET_HEADS))
    return cos, sin


def _layer_weights(l, norm_mix, w_in, w_a2, b_a, gla_head_norm, ret_head_norm, w_pa, w_pb, w_o, norm_xattn,
                   norm_mem, w_xq, w_xk, w_xv, w_xo, norm_ffn, peer_wq, peer_subkeys, peer_u, peer_v):
    offs = [0]
    for s in IN_SIZES:
        offs.append(offs[-1] + s)
    cols = [w_in[l][:, offs[i]:offs[i + 1]] for i in range(len(IN_SIZES))]
    gq, gk, gv, glr, gr, rq, rk, rv, rg, za, zb = cols
    glr = jnp.pad(glr, ((0, 0), (0, RANK_PAD - GLA_RANK)))
    half = RET_DK // 2
    j = jnp.arange(RET_QK)
    partner = jnp.where((j % RET_DK) < half, j + half, j - half)
    w = {}
    w["gla"] = jnp.concatenate([gq, gk, gv, glr, gr], axis=1).astype(bf16)
    w["ret"] = jnp.concatenate([rq, rk, rv, rg, rq[:, partner], rk[:, partner]], axis=1).astype(bf16)
    w["z"] = jnp.concatenate([za, zb], axis=1).astype(bf16)
    w["a2"] = jnp.pad(w_a2[l], ((0, RANK_PAD - GLA_RANK), (0, 0))).astype(bf16)
    w["ba"] = b_a[l].reshape(1, GLA_QK)
    w["gn"] = gla_head_norm[l]
    w["rn"] = ret_head_norm[l]
    log_gamma = jnp.log1p(-(2.0 ** (-5.0 - jnp.arange(RET_HEADS, dtype=f32))))
    w["lg"] = jnp.repeat(log_gamma, RET_DK).reshape(1, RET_QK)
    w["norm_mix"] = norm_mix[l].reshape(1, D_MODEL)
    w["pa"] = w_pa[l].astype(bf16)
    w["pb"] = w_pb[l].astype(bf16)
    w["o"] = w_o[l].astype(bf16)
    w["norm_xattn"] = norm_xattn[l].reshape(1, D_MODEL)
    w["norm_mem"] = norm_mem[l].reshape(1, D_MODEL)
    w["xq"] = w_xq[l].astype(bf16)
    w["xk"] = w_xk[l].astype(bf16)
    w["xv"] = w_xv[l].astype(bf16)
    w["xo"] = w_xo[l].astype(bf16)
    w["norm_ffn"] = norm_ffn[l].reshape(1, D_MODEL)
    w["wq"] = peer_wq[l].astype(bf16)
    w["sk"] = peer_subkeys[l].astype(bf16)
    w["u"] = peer_u[l]
    w["vt"] = peer_v[l].T.astype(bf16)
    return w


def _layer(x, B, S, keep, C, valid, cos, sin, xattn, sg0, sr0, w, gfin):
    gla, ret, z = _in_proj(x, w["norm_mix"], w["gla"], w["ret"], w["z"])
    oa, ob, sg, sr = _scan(gla, ret, w["a2"], w["ba"], w["gn"], w["rn"], cos, sin, w["lg"], sg0, sr0, B, S, C, valid)
    h, q = _merge(x, oa, ob, z, w["pa"], w["pb"], w["o"], w["norm_xattn"], w["xq"])
    o = xattn(q)
    if keep < S:
        h = h.reshape(B, S, D_MODEL)[:, :keep].reshape(B * keep, D_MODEL)
        o = o.reshape(B, S, D_MODEL)[:, :keep].reshape(B * keep, D_MODEL)
    h2, hn, cn1, e1, rk2, e2 = _peer_front(h, o, w["xo"], w["norm_ffn"], w["wq"], w["sk"])
    y = _peer_dense(hn, w["u"], w["vt"], cn1, e1, rk2, e2, h2, gfin)
    return y, sg, sr


def kernel(x_prompt, x_sample, mem_prompt, state_gla, state_ret, cache_mem_k, cache_mem_v, norm_mix, w_in, w_a2,
           b_a, gla_head_norm, ret_head_norm, w_pa, w_pb, w_o, norm_xattn, norm_mem, w_xq, w_xk, w_xv, w_xo,
           norm_ffn, peer_wq, peer_subkeys, peer_u, peer_v, norm_final):
    depth = w_in.shape[0]
    assert depth == 1, "the final norm is fused into the layer's last kernel"
    Bp, Sp, _ = x_prompt.shape
    Bs, Ss, _ = x_sample.shape
    l = 0
    w = _layer_weights(l, norm_mix, w_in, w_a2, b_a, gla_head_norm, ret_head_norm, w_pa, w_pb, w_o, norm_xattn,
                       norm_mem, w_xq, w_xk, w_xv, w_xo, norm_ffn, peer_wq, peer_subkeys, peer_u, peer_v)
    gfin = norm_final.reshape(1, D_MODEL)

    cos_p, sin_p = _rotary_tables(jnp.arange(Sp, dtype=jnp.int32))
    mk, mv = _mem_kv(mem_prompt.reshape(Bp * N_MEM, D_MODEL), w["norm_mem"], w["xk"], w["xv"])
    zeros = jnp.zeros((Bp, GLA_HEADS, GLA_DK, GLA_DV), f32)
    xattn_p = functools.partial(_xattn, mk=mk.reshape(Bp, N_MEM, D_MODEL), mv=mv.reshape(Bp, N_MEM, D_MODEL),
                                B=Bp, S=Sp, tq=XATTN_ROWS, seqs=1)
    yp, sgp, srp = _layer(x_prompt.reshape(Bp * Sp, D_MODEL), Bp, Sp, Sp, SCAN_CHUNK, SCAN_CHUNK, cos_p, sin_p,
                          xattn_p, zeros, zeros, w, gfin)

    pad = SAMPLE_PAD - Ss
    xs = jnp.pad(x_sample, ((0, 0), (0, pad), (0, 0))).reshape(Bs * SAMPLE_PAD, D_MODEL)
    cos_s, sin_s = _rotary_tables(PAST_LEN + jnp.arange(SAMPLE_PAD, dtype=jnp.int32))
    xattn_s = functools.partial(_xattn_cache, cache_k=cache_mem_k, cache_v=cache_mem_v, layer=l, B=Bs,
                                tq=SAMPLE_PAD, seqs=XATTN_SEQS)
    ys, sgs, srs = _layer(xs, Bs, SAMPLE_PAD, Ss, SAMPLE_PAD, Ss, cos_s, sin_s,
                          xattn_s, state_gla[l], state_ret[l], w, gfin)
    ys = ys.reshape(Bs, Ss, D_MODEL)

    kv_shape = (1, Bp, N_MEM, XA_HEADS, XA_DH)
    return (yp.reshape(Bp, Sp, D_MODEL), ys, sgp[None], srp[None], mk.reshape(kv_shape), mv.reshape(kv_shape),
            sgs[None], srs[None])
```
